```python
import jax, jax.numpy as jnp
from jax import lax
import numpy as np

D_MODEL = 1024
BATCH = 4
SEQ = 8192
DEPTH = 2

GRID_W = 64
CTX_LEN = 256
HEAD_DIM = 64
BRANCH_W = 512
N_BRANCH = 3
A_HEADS = 8
A_KV_HEADS = 2
A_GROUP = A_HEADS // A_KV_HEADS
WINDOW = 128
A_KV_W = A_KV_HEADS * HEAD_DIM
CONV_K = 3
NA_HEADS = 8
NA_ROWS_MAX = 8
NA_COLS = 16
ROPE_BASE = 10000.0
IN_SIZES = (BRANCH_W, A_KV_W, A_KV_W, BRANCH_W, BRANCH_W, BRANCH_W, BRANCH_W, BRANCH_W, BRANCH_W, N_BRANCH * D_MODEL)
IN_COLS = sum(IN_SIZES)
FFN_DENSE = 2816
N_EXPERTS = 8
TOP_K = 2
FFN_EXPERT = 3584
MOE_BLOCK = 128
N_DENSE = (DEPTH + 1) // 2
N_MOE = DEPTH // 2
RMS_EPS = 1e-6

kernel_name = "hybrid_gated_window_conv_natten_moe_dit"


def rmsnorm(x, g):
    xf = x.astype(jnp.float32)
    y = xf * lax.rsqrt(jnp.mean(xf * xf, axis=-1, keepdims=True) + RMS_EPS)
    return (y * g.astype(jnp.float32)).astype(x.dtype)


def modulate(h, shift, scale):
    return h * (1 + scale) + shift


def split_cols(z):
    parts, off = [], 0
    for n in IN_SIZES:
        parts.append(z[..., off:off + n])
        off += n
    return parts


def to_heads(t, n_heads):
    return t.reshape(t.shape[0], t.shape[1], n_heads, HEAD_DIM)


def axial_rope_tables(n_tokens, dtype):
    pos = jnp.arange(n_tokens)
    row = (pos // GRID_W).astype(jnp.float32)
    col = (pos % GRID_W).astype(jnp.float32)
    half = HEAD_DIM // 2
    inv = 1.0 / (ROPE_BASE ** (jnp.arange(0, half, 2, dtype=jnp.float32) / half))
    ar = row[:, None] * inv[None]
    ac = col[:, None] * inv[None]
    ang = jnp.concatenate([ar, ar, ac, ac], axis=-1)[:, None, :]
    return jnp.cos(ang).astype(dtype), jnp.sin(ang).astype(dtype)


def axial_rope(t, cos, sin):
    half, quarter = HEAD_DIM // 2, HEAD_DIM // 4
    def rot(u):
        return jnp.concatenate([-u[..., quarter:], u[..., :quarter]], axis=-1)
    rotated = jnp.concatenate([rot(t[..., :half]), rot(t[..., half:])], axis=-1)
    return t * cos + rotated * sin


def joint_softmax(parts, sink=None):
    m = parts[0].max(axis=-1, keepdims=True)
    for s in parts[1:]:
        m = jnp.maximum(m, s.max(axis=-1, keepdims=True))
    if sink is not None:
        m = jnp.maximum(m, sink)
    es = [jnp.exp(s - m) for s in parts]
    den = es[0].sum(axis=-1, keepdims=True)
    for e in es[1:]:
        den = den + e.sum(axis=-1, keepdims=True)
    if sink is not None:
        den = den + jnp.exp(sink - m)
    return [e / den for e in es]


def window_gqa(q, k, v, kc, vc, sink):
    b, s = q.shape[:2]
    nb = s // WINDOW
    scale = HEAD_DIM ** -0.5
    qb = q.reshape(b, nb, WINDOW, A_KV_HEADS, A_GROUP, HEAD_DIM)
    def band(t):
        tp = jnp.pad(t.reshape(b, nb, WINDOW, A_KV_HEADS, HEAD_DIM), ((0, 0), (1, 1), (0, 0), (0, 0), (0, 0)))
        return jnp.concatenate([tp[:, :-2], tp[:, 1:-1], tp[:, 2:]], axis=2)
    kb, vb = band(k), band(v)
    qpos = jnp.arange(s).reshape(nb, WINDOW)
    kpos = (jnp.arange(nb)[:, None] - 1) * WINDOW + jnp.arange(3 * WINDOW)[None]
    valid = ((jnp.abs(qpos[:, :, None] - kpos[:, None, :]) <= WINDOW)
             & (kpos[:, None, :] >= 0) & (kpos[:, None, :] < s))
    s_win = jnp.einsum('bnqkgd,bnjkd->bnkgqj', qb, kb, preferred_element_type=jnp.float32) * scale
    s_win = jnp.where(valid[None, :, None, None], s_win, -jnp.inf)
    s_ctx = jnp.einsum('bnqkgd,bjkd->bnkgqj', qb, kc, preferred_element_type=jnp.float32) * scale
    sk = sink.astype(jnp.float32).reshape(A_KV_HEADS, A_GROUP)[:, :, None, None]
    p_win, p_ctx = joint_softmax([s_win, s_ctx], sk)
    o = (jnp.einsum('bnkgqj,bnjkd->bnqkgd', p_win.astype(vb.dtype), vb)
         + jnp.einsum('bnkgqj,bjkd->bnqkgd', p_ctx.astype(vc.dtype), vc))
    return o.reshape(b, s, A_HEADS * HEAD_DIM)


def context_gqa(q, k, v, sink):
    b, n = q.shape[:2]
    qg = q.reshape(b, n, A_KV_HEADS, A_GROUP, HEAD_DIM)
    s = jnp.einsum('blkgd,bjkd->bkglj', qg, k, preferred_element_type=jnp.float32) * HEAD_DIM ** -0.5
    sk = sink.astype(jnp.float32).reshape(A_KV_HEADS, A_GROUP)[:, :, None, None]
    (p,) = joint_softmax([s], sk)
    o = jnp.einsum('bkglj,bjkd->blkgd', p.astype(v.dtype), v)
    return o.reshape(b, n, A_HEADS * HEAD_DIM)


def short_conv(x_in, gate_b, gate_c, w):
    u = gate_c * x_in
    v = lax.conv_general_dilated(u, w[:, None, :].astype(u.dtype), window_strides=(1,),
                                 padding=((CONV_K // 2, CONV_K // 2),),
                                 dimension_numbers=('NWC', 'WIO', 'NWC'),
                                 feature_group_count=u.shape[-1])
    return gate_b * v


def neighbourhood_attn(q, k, v, kc, vc, rpb):
    b, s = q.shape[:2]
    rows = s // GRID_W
    kr = min(NA_ROWS_MAX, rows)
    kcn = NA_COLS
    scale = HEAD_DIM ** -0.5
    qg = q.reshape(b, rows, GRID_W, NA_HEADS, HEAD_DIM)
    kg = k.reshape(b, rows, GRID_W, NA_HEADS, HEAD_DIM)
    vg = v.reshape(b, rows, GRID_W, NA_HEADS, HEAD_DIM)
    r = jnp.arange(rows)
    row_idx = jnp.clip(r - kr // 2, 0, rows - kr)[:, None] + jnp.arange(kr)[None]
    k_rows = kg[:, row_idx]
    v_rows = vg[:, row_idx]
    col = jnp.arange(GRID_W)
    col_start = jnp.clip(col - kcn // 2, 0, GRID_W - kcn)
    col_ok = (col[None, :] >= col_start[:, None]) & (col[None, :] < col_start[:, None] + kcn)
    dr = row_idx - r[:, None] + NA_ROWS_MAX - 1
    dc = jnp.clip(col[None, :] - col[:, None], -(kcn - 1), kcn - 1) + kcn - 1
    bias = rpb.astype(jnp.float32)[:, dr[:, None, :, None], dc[None, :, None, :]]
    bias = jnp.where(col_ok[None, None, :, None, :], bias, -jnp.inf).transpose(1, 0, 2, 3, 4)
    s_nb = jnp.einsum('brqhd,brkjhd->brhqkj', qg, k_rows, preferred_element_type=jnp.float32) * scale + bias[None]
    s_nb = s_nb.reshape(b, rows, NA_HEADS, GRID_W, kr * GRID_W)
    s_ctx = jnp.einsum('brqhd,bjhd->brhqj', qg, kc, preferred_element_type=jnp.float32) * scale
    p_nb, p_ctx = joint_softmax([s_nb, s_ctx])
    p_nb = p_nb.reshape(b, rows, NA_HEADS, GRID_W, kr, GRID_W)
    o = (jnp.einsum('brhqkj,brkjhd->brqhd', p_nb.astype(v_rows.dtype), v_rows)
         + jnp.einsum('brhqj,bjhd->brqhd', p_ctx.astype(vc.dtype), vc))
    return o.reshape(b, s, NA_HEADS * HEAD_DIM)


def context_mha(q, k, v):
    s = jnp.einsum('blhd,bjhd->bhlj', q, k, preferred_element_type=jnp.float32) * HEAD_DIM ** -0.5
    (p,) = joint_softmax([s])
    o = jnp.einsum('bhlj,bjhd->blhd', p.astype(v.dtype), v)
    return o.reshape(q.shape[0], q.shape[1], NA_HEADS * HEAD_DIM)


def branch_merge(o_a, o_b, o_c, gate_pre, b_gate, w_branch, w_out):
    o = jnp.stack([o_a, o_b, o_c], axis=-2)
    proj = jnp.einsum('bnrc,rcd->bnrd', o, w_branch)
    g = jax.nn.sigmoid(gate_pre.reshape(*gate_pre.shape[:-1], N_BRANCH, D_MODEL) + b_gate.reshape(N_BRANCH, D_MODEL))
    return jnp.sum(g * proj, axis=-2) @ w_out


def mixer_sublayer(hx, hc, cos, sin, w_in, b_gate, sink, conv_w, rpb, w_branch, w_out, with_ctx_out):
    qa, ka, va, ux, gbx, gcx, qn, kn, vn, gx = split_cols(hx @ w_in)
    qa_c, ka_c, va_c, uc, gbc, gcc, qn_c, kn_c, vn_c, gc = split_cols(hc @ w_in)
    ka_c, va_c = to_heads(ka_c, A_KV_HEADS), to_heads(va_c, A_KV_HEADS)
    kn_c, vn_c = to_heads(kn_c, NA_HEADS), to_heads(vn_c, NA_HEADS)
    o_a = window_gqa(axial_rope(to_heads(qa, A_HEADS), cos, sin), axial_rope(to_heads(ka, A_KV_HEADS), cos, sin),
                     to_heads(va, A_KV_HEADS), ka_c, va_c, sink)
    o_b = short_conv(ux, gbx, gcx, conv_w)
    o_c = neighbourhood_attn(to_heads(qn, NA_HEADS), to_heads(kn, NA_HEADS), to_heads(vn, NA_HEADS), kn_c, vn_c, rpb)
    y_x = branch_merge(o_a, o_b, o_c, gx, b_gate, w_branch, w_out)
    if not with_ctx_out:
        return y_x, None
    o_a_c = context_gqa(to_heads(qa_c, A_HEADS), ka_c, va_c, sink)
    o_b_c = short_conv(uc, gbc, gcc, conv_w)
    o_c_c = context_mha(to_heads(qn_c, NA_HEADS), kn_c, vn_c)
    y_c = branch_merge(o_a_c, o_b_c, o_c_c, gc, b_gate, w_branch, w_out)
    return y_x, y_c


def swiglu(h, w_gu, w_dn):
    g, u = jnp.split(h @ w_gu, 2, axis=-1)
    return (jax.nn.silu(g) * u) @ w_dn


def moe_swiglu(xt, w_router, b_router, w_gu, w_dn):
    t = xt.shape[0]
    logits = (xt @ w_router).astype(jnp.float32) + b_router.astype(jnp.float32)
    top_v, top_e = lax.top_k(logits, TOP_K)
    gates = jax.nn.softmax(top_v, axis=-1)
    n_assign = t * TOP_K
    e_flat = top_e.reshape(-1)
    tok_flat = jnp.repeat(jnp.arange(t, dtype=jnp.int32), TOP_K)
    g_flat = gates.reshape(-1)
    order = jnp.argsort(e_flat)
    e_s, tok_s, g_s = e_flat[order], tok_flat[order], g_flat[order]
    counts = jnp.bincount(e_flat, length=N_EXPERTS)
    padded = (counts + MOE_BLOCK - 1) // MOE_BLOCK * MOE_BLOCK
    start = jnp.cumsum(counts) - counts
    ends = jnp.cumsum(padded)
    pstart = ends - padded
    dest = pstart[e_s] + jnp.arange(n_assign) - start[e_s]
    n_blocks = -(-n_assign // MOE_BLOCK) + N_EXPERTS
    n_rows = n_blocks * MOE_BLOCK
    row_tok = jnp.full((n_rows,), t, jnp.int32).at[dest].set(tok_s)
    row_gate = jnp.zeros((n_rows,), jnp.float32).at[dest].set(g_s)
    block_e = jnp.clip(jnp.searchsorted(ends, jnp.arange(n_blocks) * MOE_BLOCK, side='right'), 0, N_EXPERTS - 1)
    x_pad = jnp.concatenate([xt, jnp.zeros((1, xt.shape[1]), xt.dtype)], axis=0)
    xb = x_pad[row_tok].reshape(n_blocks, MOE_BLOCK, xt.shape[1])
    def expert_block(args):
        xblk, e = args
        g, u = jnp.split(xblk @ w_gu[e], 2, axis=-1)
        return (jax.nn.silu(g) * u) @ w_dn[e]
    yb = lax.map(expert_block, (xb, block_e)).reshape(n_rows, xt.shape[1])
    yb = yb * row_gate[:, None].astype(yb.dtype)
    return jnp.zeros((t + 1, xt.shape[1]), yb.dtype).at[row_tok].add(yb)[:t]


def channel_mixer(h, layer, w_gu_dense, w_dn_dense, w_router, b_router, w_gu_moe, w_dn_moe):
    if layer % 2 == 0:
        return swiglu(h, w_gu_dense[layer // 2], w_dn_dense[layer // 2])
    b, n, d = h.shape
    i = layer // 2
    return moe_swiglu(h.reshape(b * n, d), w_router[i], b_router[i], w_gu_moe[i], w_dn_moe[i]).reshape(b, n, d)


def setup_inputs(seed: int = 0) -> dict:
    key = jax.random.key(seed)
    ks = jax.random.split(key, 24)
    D = D_MODEL
    def nrm(k, shape, s):
        return jax.random.normal(k, shape, jnp.float32) * s
    return {
        "x": nrm(ks[0], (BATCH, SEQ, D), 1.0),
        "c": nrm(ks[1], (BATCH, D), 1.0),
        "ctx": nrm(ks[2], (BATCH, CTX_LEN, D), 1.0),
        "c_ctx": nrm(ks[3], (D,), 1.0),
        "w_ada": nrm(ks[4], (DEPTH, D, 6 * D), 0.5 * D ** -0.5),
        "b_ada": nrm(ks[5], (DEPTH, 6 * D), 0.02),
        "g_pre_mix": 1.0 + nrm(ks[6], (DEPTH, D), 0.02),
        "g_post_mix": 1.0 + nrm(ks[7], (DEPTH, D), 0.02),
        "g_pre_ffn": 1.0 + nrm(ks[8], (DEPTH, D), 0.02),
        "g_post_ffn": 1.0 + nrm(ks[9], (DEPTH, D), 0.02),
        "w_in": nrm(ks[10], (DEPTH, D, IN_COLS), D ** -0.5),
        "b_gate": nrm(ks[11], (DEPTH, N_BRANCH * D), 0.02),
        "sink": nrm(ks[12], (DEPTH, A_HEADS), 0.5),
        "conv_w": nrm(ks[13], (DEPTH, CONV_K, BRANCH_W), CONV_K ** -0.5),
        "rpb": nrm(ks[14], (DEPTH, NA_HEADS, 2 * NA_ROWS_MAX - 1, 2 * NA_COLS - 1), 0.1),
        "w_branch": nrm(ks[15], (DEPTH, N_BRANCH, BRANCH_W, D), BRANCH_W ** -0.5),
        "w_out": nrm(ks[16], (DEPTH, D, D), D ** -0.5),
        "w_gu_dense": nrm(ks[17], (N_DENSE, D, 2 * FFN_DENSE), D ** -0.5),
        "w_dn_dense": nrm(ks[18], (N_DENSE, FFN_DENSE, D), FFN_DENSE ** -0.5),
        "w_router": nrm(ks[19], (N_MOE, D, N_EXPERTS), D ** -0.5),
        "b_router": nrm(ks[20], (N_MOE, N_EXPERTS), 0.01),
        "w_gu_moe": nrm(ks[21], (N_MOE, N_EXPERTS, D, 2 * FFN_EXPERT), D ** -0.5),
        "w_dn_moe": nrm(ks[22], (N_MOE, N_EXPERTS, FFN_EXPERT, D), FFN_EXPERT ** -0.5),
    }


def reference(x, c, ctx, c_ctx, w_ada, b_ada, g_pre_mix, g_post_mix, g_pre_ffn, g_post_ffn, w_in, b_gate,
              sink, conv_w, rpb, w_branch, w_out, w_gu_dense, w_dn_dense, w_router, b_router, w_gu_moe, w_dn_moe):
    cos, sin = axial_rope_tables(x.shape[1], x.dtype)
    h_ctx = ctx
    for l in range(DEPTH):
        update_ctx = l < DEPTH - 1
        mod_x = (jax.nn.silu(c) @ w_ada[l] + b_ada[l])[:, None, :]
        mod_c = jax.nn.silu(c_ctx) @ w_ada[l] + b_ada[l]
        sh1, sc1, gt1, sh2, sc2, gt2 = jnp.split(mod_x, 6, axis=-1)
        csh1, csc1, cgt1, csh2, csc2, cgt2 = jnp.split(mod_c, 6, axis=-1)
        hx = modulate(rmsnorm(x, g_pre_mix[l]), sh1, sc1)
        hc = modulate(rmsnorm(h_ctx, g_pre_mix[l]), csh1, csc1)
        y_x, y_c = mixer_sublayer(hx, hc, cos, sin, w_in[l], b_gate[l], sink[l], conv_w[l], rpb[l],
                                  w_branch[l], w_out[l], update_ctx)
        x = x + gt1 * rmsnorm(y_x, g_post_mix[l])
        hx = modulate(rmsnorm(x, g_pre_ffn[l]), sh2, sc2)
        x = x + gt2 * rmsnorm(channel_mixer(hx, l, w_gu_dense, w_dn_dense, w_router, b_router, w_gu_moe, w_dn_moe), g_post_ffn[l])
        if update_ctx:
            h_ctx = h_ctx + cgt1 * rmsnorm(y_c, g_post_mix[l])
            hc = modulate(rmsnorm(h_ctx, g_pre_ffn[l]), csh2, csc2)
            h_ctx = h_ctx + cgt2 * rmsnorm(channel_mixer(hc, l, w_gu_dense, w_dn_dense, w_router, b_router, w_gu_moe, w_dn_moe), g_post_ffn[l])
    return x
```

```python
import functools

import numpy as np
import jax
import jax.numpy as jnp
from jax import lax
from jax.experimental import pallas as pl
from jax.experimental.pallas import tpu as pltpu

F32 = jnp.float32
BF16 = jnp.bfloat16
I32 = jnp.int32

D_MODEL = 1024
DEPTH = 2
GRID_W = 64
HEAD_DIM = 64
BRANCH_W = 512
N_BRANCH = 3
A_HEADS = 8
A_KV_HEADS = 2
WINDOW = 128
CONV_K = 3
NA_HEADS = 8
NA_ROWS_MAX = 8
NA_COLS = 16
ROPE_BASE = 10000.0
FFN_DENSE = 2816
N_EXPERTS = 8
TOP_K = 2
FFN_EXPERT = 3584
RMS_EPS = 1e-6

LANES = 128
MXU_COLS = 256
VMEM_LIMIT_BYTES = 56 * 1024 * 1024

NEG_BIG = -1e30
TOKEN_TILE = 512
WIN_Q_TILE = 512
NA_ROW_GROUP = 4
MOE_BLOCK_ROWS = 512

_QA = (0, 512)
_KVA = (512, 1024)
_CONV = (1024, 2560)
_NA = (2560, 4096)
_GATE = (4096, 7168)
IN_COLS_ARRANGED = 7168


def _cparams(n_grid_dims, vmem=VMEM_LIMIT_BYTES):
    return pltpu.CompilerParams(dimension_semantics=("arbitrary",) * n_grid_dims, vmem_limit_bytes=vmem)


def _dot(a, b):
    return jnp.dot(a, b, preferred_element_type=F32)


def _dot_nt(a, b):
    return lax.dot_general(a, b, (((1,), (1,)), ((), ())), preferred_element_type=F32)


def _rms(x, g):
    return x * lax.rsqrt(jnp.mean(x * x, axis=-1, keepdims=True) + RMS_EPS) * g


def _col_chunks(n, step=512):
    return [(lo, min(lo + step, n)) for lo in range(0, n, step)]


def _ada_kernel(c_ref, w_ref, b_ref, o_ref):
    cv = c_ref[...]
    a = cv * jax.nn.sigmoid(cv)
    o_ref[0] = jnp.dot(a, w_ref[0], preferred_element_type=F32, precision=lax.Precision.HIGHEST) + b_ref[0]


def _ada_mods(cvecs, w_ada, b_ada):
    n_rows = cvecs.shape[0]
    tn = 1536
    out = pl.pallas_call(
        _ada_kernel,
        out_shape=jax.ShapeDtypeStruct((DEPTH, n_rows, 6 * D_MODEL), F32),
        grid=(DEPTH, 6 * D_MODEL // tn),
        in_specs=[
            pl.BlockSpec((n_rows, D_MODEL), lambda l, j: (0, 0)),
            pl.BlockSpec((1, D_MODEL, tn), lambda l, j: (l, 0, j)),
            pl.BlockSpec((1, 1, tn), lambda l, j: (l, 0, j)),
        ],
        out_specs=pl.BlockSpec((1, n_rows, tn), lambda l, j: (l, 0, j)),
        compiler_params=_cparams(2),
        name="ada_mod",
    )(cvecs, w_ada, b_ada.reshape(DEPTH, 1, 6 * D_MODEL))
    return out.reshape(DEPTH, n_rows, 6, D_MODEL)


def _inproj_kernel(x_ref, mod_ref, g_ref, cos_ref, sin_ref, w_ref, qa_ref, kva_ref, conv_ref, na_ref, gate_ref):
    x = x_ref[...]
    h = (_rms(x, g_ref[...]) * (1.0 + mod_ref[0, 1:2, :]) + mod_ref[0, 0:1, :]).astype(BF16)
    cos = cos_ref[...]
    sin = sin_ref[...]
    lane = lax.broadcasted_iota(I32, (1, LANES), 1)
    first_quarter = (lane & 16) == 0

    def mm(lo, hi):
        return _dot(h, w_ref[:, lo:hi])

    def rope(a):
        outs = []
        for j in range(a.shape[1] // LANES):
            c = a[:, j * LANES:(j + 1) * LANES]
            up = pltpu.roll(c, LANES - 16, 1)
            dn = pltpu.roll(c, 16, 1)
            outs.append(c * cos + jnp.where(first_quarter, up, dn) * sin)
        return jnp.concatenate(outs, axis=1)

    qa_ref[...] = rope(mm(*_QA)).astype(BF16)
    kv = mm(*_KVA)
    kva_ref[:, 0:256] = rope(kv[:, 0:256]).astype(BF16)
    kva_ref[:, 256:512] = kv[:, 256:512].astype(BF16)
    for ref, (base, end) in ((conv_ref, _CONV), (na_ref, _NA), (gate_ref, _GATE)):
        for lo, hi in _col_chunks(end - base):
            ref[:, lo:hi] = mm(base + lo, base + hi).astype(BF16)


def _inproj(x2d, mods, g_pre, cos_t, sin_t, w_arr, seq_len, tm):
    n_tok = x2d.shape[0]
    tiles_per_seq = seq_len // tm
    if mods.shape[0] == 1:
        mod_map = lambda i: (0, 0, 0)
    else:
        mod_map = lambda i: (i // tiles_per_seq, 0, 0)
    widths = (512, 512, 1536, 1536, 3072)
    return pl.pallas_call(
        _inproj_kernel,
        out_shape=[jax.ShapeDtypeStruct((n_tok, w), BF16) for w in widths],
        grid=(n_tok // tm,),
        in_specs=[
            pl.BlockSpec((tm, D_MODEL), lambda i: (i, 0)),
            pl.BlockSpec((1, 6, D_MODEL), mod_map),
            pl.BlockSpec((1, D_MODEL), lambda i: (0, 0)),
            pl.BlockSpec((tm, LANES), lambda i: (i % tiles_per_seq, 0)),
            pl.BlockSpec((tm, LANES), lambda i: (i % tiles_per_seq, 0)),
            pl.BlockSpec((D_MODEL, IN_COLS_ARRANGED), lambda i: (0, 0)),
        ],
        out_specs=[pl.BlockSpec((tm, w), lambda i: (i, 0)) for w in widths],
        compiler_params=_cparams(1),
        name="in_proj",
    )(x2d, mods, g_pre, cos_t, sin_t, w_arr)


def _pair_stack(q, even):
    zero = jnp.zeros_like(q)
    return jnp.concatenate([jnp.where(even, q, zero), jnp.where(even, zero, q)], axis=0)


def _softmax_pv(score_parts, value_parts, sink_col):
    m = score_parts[0].max(axis=1, keepdims=True)
    for s in score_parts[1:]:
        m = jnp.maximum(m, s.max(axis=1, keepdims=True))
    if sink_col is not None:
        m = jnp.maximum(m, sink_col)
    den = None
    acc = None
    for s, v in zip(score_parts, value_parts):
        e = jnp.exp(s - m)
        d = e.sum(axis=1, keepdims=True)
        den = d if den is None else den + d
        r = _dot(e.astype(BF16), v)
        acc = r if acc is None else acc + r
    if sink_col is not None:
        den = den + jnp.exp(sink_col - m)
    return acc / den


def _win_attn_kernel(sink_ref, q_ref, kp_ref, km_ref, kn_ref, vp_ref, vm_ref, vn_ref, kc_ref, vc_ref, o_ref,
                     *, seq_len, tq):
    blk0 = pl.program_id(1) * (tq // WINDOW)
    kh = pl.program_id(2)
    kspan = jnp.concatenate([kp_ref[...], km_ref[...], kn_ref[...]], axis=0)
    vspan = jnp.concatenate([vp_ref[...], vm_ref[...], vn_ref[...]], axis=0)
    kc = kc_ref[...]
    vc = vc_ref[...]
    lane = lax.broadcasted_iota(I32, (1, LANES), 1)
    even = lane < HEAD_DIM
    n_stack = 4 * WINDOW
    rows = lax.broadcasted_iota(I32, (n_stack, 1), 0)
    piece = rows // WINDOW
    qi = rows - piece * WINDOW
    kj = lax.broadcasted_iota(I32, (1, 3 * WINDOW), 1)
    rel = kj - WINDOW - qi
    in_window = (rel <= WINDOW) & (rel >= -WINDOW)
    sink_col = jnp.zeros((n_stack, 1), F32)
    for j in range(4):
        sink_col = jnp.where(piece == j, sink_ref[kh * 4 + j], sink_col)
    for n in range(tq // WINDOW):
        qb = q_ref[n * WINDOW:(n + 1) * WINDOW, :]
        lq = jnp.concatenate([_pair_stack(qb[:, 0:LANES], even), _pair_stack(qb[:, LANES:2 * LANES], even)], axis=0)
        kw = kspan[n * WINDOW:(n + 3) * WINDOW]
        vw = vspan[n * WINDOW:(n + 3) * WINDOW]
        kpos = (blk0 + n - 1) * WINDOW + kj
        valid = in_window & (kpos >= 0) & (kpos < seq_len)
        s_w = jnp.where(valid, _dot_nt(lq, kw), NEG_BIG)
        s_c = _dot_nt(lq, kc)
        r = _softmax_pv([s_w, s_c], [vw, vc], sink_col)
        o_ref[n * WINDOW:(n + 1) * WINDOW, 0:LANES] = jnp.where(even, r[0:WINDOW], r[WINDOW:2 * WINDOW]).astype(BF16)
        o_ref[n * WINDOW:(n + 1) * WINDOW, LANES:2 * LANES] = jnp.where(
            even, r[2 * WINDOW:3 * WINDOW], r[3 * WINDOW:4 * WINDOW]).astype(BF16)


def _win_attn(sink, qa, kva, kva_c, batch, seq_len, ctx_len):
    tq = min(WIN_Q_TILE, seq_len)
    nblk = seq_len // WINDOW
    per = tq // WINDOW
    nq = seq_len // tq

    def main_map(lane_off):
        return lambda b, i, kh, s: (b * nq + i, lane_off + kh)

    def prev_map(lane_off):
        return lambda b, i, kh, s: (b * nblk + jnp.maximum(i * per - 1, 0), lane_off + kh)

    def next_map(lane_off):
        return lambda b, i, kh, s: (b * nblk + jnp.minimum((i + 1) * per, nblk - 1), lane_off + kh)

    def ctx_map(lane_off):
        return lambda b, i, kh, s: (b, lane_off + kh)

    kern = functools.partial(_win_attn_kernel, seq_len=seq_len, tq=tq)
    return pl.pallas_call(
        kern,
        out_shape=jax.ShapeDtypeStruct((batch * seq_len, BRANCH_W), BF16),
        grid_spec=pltpu.PrefetchScalarGridSpec(
            num_scalar_prefetch=1,
            grid=(batch, nq, A_KV_HEADS),
            in_specs=[
                pl.BlockSpec((tq, 2 * LANES), lambda b, i, kh, s: (b * nq + i, kh)),
                pl.BlockSpec((WINDOW, LANES), prev_map(0)),
                pl.BlockSpec((tq, LANES), main_map(0)),
                pl.BlockSpec((WINDOW, LANES), next_map(0)),
                pl.BlockSpec((WINDOW, LANES), prev_map(2)),
                pl.BlockSpec((tq, LANES), main_map(2)),
                pl.BlockSpec((WINDOW, LANES), next_map(2)),
                pl.BlockSpec((ctx_len, LANES), ctx_map(0)),
                pl.BlockSpec((ctx_len, LANES), ctx_map(2)),
            ],
            out_specs=pl.BlockSpec((tq, 2 * LANES), lambda b, i, kh, s: (b * nq + i, kh)),
        ),
        compiler_params=_cparams(3),
        name="win_attn",
    )(sink, qa, kva, kva, kva, kva, kva, kva, kva_c, kva_c)


def _ctx_attn_kernel(sink_ref, q_ref, k_ref, v_ref, o_ref, *, use_sink):
    p = pl.program_id(1)
    n = q_ref.shape[0]
    lane = lax.broadcasted_iota(I32, (1, LANES), 1)
    even = lane < HEAD_DIM
    lq = _pair_stack(q_ref[...], even)
    s = _dot_nt(lq, k_ref[...])
    sink_col = None
    if use_sink:
        rows = lax.broadcasted_iota(I32, (2 * n, 1), 0)
        sink_col = jnp.where(rows < n, sink_ref[2 * p], sink_ref[2 * p + 1])
    r = _softmax_pv([s], [v_ref[...]], sink_col)
    o_ref[...] = jnp.where(even, r[0:n], r[n:2 * n]).astype(BF16)


def _ctx_attn(sink, q_arr, q_lane0, kv_arr, k_lane0, v_lane0, kv_shared, batch, ctx_len, use_sink):
    def kv_map(lane0):
        if kv_shared:
            return lambda b, p, s: (b, lane0 + p // 2)
        return lambda b, p, s: (b, lane0 + p)

    kern = functools.partial(_ctx_attn_kernel, use_sink=use_sink)
    return pl.pallas_call(
        kern,
        out_shape=jax.ShapeDtypeStruct((batch * ctx_len, BRANCH_W), BF16),
        grid_spec=pltpu.PrefetchScalarGridSpec(
            num_scalar_prefetch=1,
            grid=(batch, BRANCH_W // LANES),
            in_specs=[
                pl.BlockSpec((ctx_len, LANES), lambda b, p, s: (b, q_lane0 + p)),
                pl.BlockSpec((ctx_len, LANES), kv_map(k_lane0)),
                pl.BlockSpec((ctx_len, LANES), kv_map(v_lane0)),
            ],
            out_specs=pl.BlockSpec((ctx_len, LANES), lambda b, p, s: (b, p)),
        ),
        compiler_params=_cparams(2),
        name="ctx_attn_sink" if use_sink else "ctx_attn",
    )(sink, q_arr, kv_arr, kv_arr)


def _na_kernel(q_ref, kp_ref, km_ref, kn_ref, vp_ref, vm_ref, vn_ref, kc_ref, vc_ref, b0_ref, b1_ref, o_ref):
    gq = NA_ROW_GROUP * GRID_W
    lane = lax.broadcasted_iota(I32, (1, LANES), 1)
    even = lane < HEAD_DIM
    kc = kc_ref[...]
    vc = vc_ref[...]
    spans = (
        (jnp.concatenate([kp_ref[gq:2 * gq], km_ref[...]], axis=0), jnp.concatenate([vp_ref[gq:2 * gq], vm_ref[...]], axis=0)),
        (jnp.concatenate([km_ref[...], kn_ref[0:gq]], axis=0), jnp.concatenate([vm_ref[...], vn_ref[0:gq]], axis=0)),
    )
    for g, b_ref in enumerate((b0_ref, b1_ref)):
        kspan, vspan = spans[g]
        lq = _pair_stack(q_ref[g * gq:(g + 1) * gq, :], even)
        bias = jnp.concatenate([b_ref[0, 0], b_ref[0, 1]], axis=0)
        s_b = _dot_nt(lq, kspan) + bias
        s_c = _dot_nt(lq, kc)
        r = _softmax_pv([s_b, s_c], [vspan, vc], None)
        o_ref[g * gq:(g + 1) * gq, :] = jnp.where(even, r[0:gq], r[gq:2 * gq]).astype(BF16)


def _na_bias_tables(rpb):
    i = np.arange(NA_ROW_GROUP)[:, None]
    j = np.arange(3 * NA_ROW_GROUP)[None, :]
    dr = np.clip(j - i + 3, 0, 2 * NA_ROWS_MAX - 2)
    row_ok = np.stack([
        np.broadcast_to((j >= 4) & (j < 12), dr.shape),
        (j >= i) & (j < i + NA_ROWS_MAX),
        np.broadcast_to((j >= 0) & (j < 8), dr.shape),
    ])
    col = np.arange(GRID_W)
    col_start = np.clip(col - NA_COLS // 2, 0, GRID_W - NA_COLS)
    col_ok = (col[None, :] >= col_start[:, None]) & (col[None, :] < col_start[:, None] + NA_COLS)
    dc = np.clip(col[None, :] - col[:, None], -(NA_COLS - 1), NA_COLS - 1) + NA_COLS - 1
    bias = rpb.astype(F32)[:, dr[:, None, :, None], dc[None, :, None, :]]
    ok = row_ok[:, :, None, :, None] & col_ok[None, None, :, None, :]
    out = jnp.where(ok[:, None], bias[None], NEG_BIG)
    return out.reshape(3, NA_HEADS, NA_ROW_GROUP * GRID_W, 3 * NA_ROW_GROUP * GRID_W)


def _na_attn(na, na_c, bias_tab, batch, seq_len, ctx_len):
    rows = seq_len // GRID_W
    assert rows % (2 * NA_ROW_GROUP) == 0 and rows >= 4 * NA_ROW_GROUP
    tq = 2 * NA_ROW_GROUP * GRID_W
    nq = seq_len // tq
    n_pairs = NA_HEADS // 2

    def tok_map(lane0, shift):
        return lambda b, i, p: (b * nq + jnp.clip(i + shift, 0, nq - 1), lane0 + p)

    def ctx_map(lane0):
        return lambda b, i, p: (b, lane0 + p)

    kq = BRANCH_W // LANES
    bias_block = (1, 2, NA_ROW_GROUP * GRID_W, 3 * NA_ROW_GROUP * GRID_W)
    return pl.pallas_call(
        _na_kernel,
        out_shape=jax.ShapeDtypeStruct((batch * seq_len, BRANCH_W), BF16),
        grid=(batch, nq, n_pairs),
        in_specs=[
            pl.BlockSpec((tq, LANES), tok_map(0, 0)),
            pl.BlockSpec((tq, LANES), tok_map(kq, -1)),
            pl.BlockSpec((tq, LANES), tok_map(kq, 0)),
            pl.BlockSpec((tq, LANES), tok_map(kq, 1)),
            pl.BlockSpec((tq, LANES), tok_map(2 * kq, -1)),
            pl.BlockSpec((tq, LANES), tok_map(2 * kq, 0)),
            pl.BlockSpec((tq, LANES), tok_map(2 * kq, 1)),
            pl.BlockSpec((ctx_len, LANES), ctx_map(kq)),
            pl.BlockSpec((ctx_len, LANES), ctx_map(2 * kq)),
            pl.BlockSpec(bias_block, lambda b, i, p: (jnp.where(i == 0, 0, 1), p, 0, 0)),
            pl.BlockSpec(bias_block, lambda b, i, p: (jnp.where(i == nq - 1, 2, 1), p, 0, 0)),
        ],
        out_specs=pl.BlockSpec((tq, LANES), lambda b, i, p: (b * nq + i, p)),
        compiler_params=_cparams(3),
        name="na_attn",
    )(na, na, na, na, na, na, na, na_c, na_c, bias_tab, bias_tab)


def _merge_kernel(oa_ref, conv_ref, cprev_ref, cnext_ref, oc_ref, gate_ref, x_ref, mod_ref, wb_ref, bg_ref, cw_ref,
                  wo_ref, gpost_ref, out_ref, y_ref, *, tiles_per_seq):
    tm = x_ref.shape[0]
    ti = pl.program_id(0) % tiles_per_seq
    ux = conv_ref[:, 0:512].astype(F32)
    gb = conv_ref[:, 512:1024].astype(F32)
    gc = conv_ref[:, 1024:1536].astype(F32)
    u = gc * ux
    cprev = cprev_ref[...].astype(F32)
    cnext = cnext_ref[...].astype(F32)
    halo = cprev.shape[0]
    u_before = cprev[halo - 1:halo, 1024:1536] * cprev[halo - 1:halo, 0:512]
    u_after = cnext[0:1, 1024:1536] * cnext[0:1, 0:512]
    u_before = jnp.where(ti == 0, 0.0, u_before)
    u_after = jnp.where(ti == tiles_per_seq - 1, 0.0, u_after)
    row = lax.broadcasted_iota(I32, (tm, 1), 0)
    u_dn = jnp.where(row == 0, u_before, pltpu.roll(u, 1, 0))
    u_up = jnp.where(row == tm - 1, u_after, pltpu.roll(u, tm - 1, 0))
    cw = cw_ref[...]
    ob = (gb * (cw[0:1] * u_dn + cw[1:2] * u + cw[2:3] * u_up)).astype(BF16)
    oa = oa_ref[...]
    oc = oc_ref[...]
    for lo, hi in _col_chunks(D_MODEL):
        acc = None
        for r, o in enumerate((oa, ob, oc)):
            gate = jax.nn.sigmoid(gate_ref[:, r * D_MODEL + lo:r * D_MODEL + hi].astype(F32)
                                  + bg_ref[:, r * D_MODEL + lo:r * D_MODEL + hi])
            term = gate * _dot(o, wb_ref[r, :, lo:hi])
            acc = term if acc is None else acc + term
        y_ref[:, lo:hi] = acc.astype(BF16)
    y2 = _dot(y_ref[...], wo_ref[...])
    out_ref[...] = x_ref[...] + mod_ref[0, 2:3, :] * _rms(y2, gpost_ref[...])


def _merge(oa, conv, oc, gate, x2d, mods, wb, b_gate, conv_w, wo, g_post, seq_len, tm):
    n_tok = x2d.shape[0]
    tiles_per_seq = seq_len // tm
    halo = 16
    hb = tm // halo
    n_halo = n_tok // halo
    if mods.shape[0] == 1:
        mod_map = lambda i: (0, 0, 0)
    else:
        mod_map = lambda i: (i // tiles_per_seq, 0, 0)
    kern = functools.partial(_merge_kernel, tiles_per_seq=tiles_per_seq)
    const2 = lambda i: (0, 0)
    return pl.pallas_call(
        kern,
        out_shape=jax.ShapeDtypeStruct((n_tok, D_MODEL), F32),
        grid=(n_tok // tm,),
        in_specs=[
            pl.BlockSpec((tm, BRANCH_W), lambda i: (i, 0)),
            pl.BlockSpec((tm, 3 * BRANCH_W), lambda i: (i, 0)),
            pl.BlockSpec((halo, 3 * BRANCH_W), lambda i: (jnp.maximum(i * hb - 1, 0), 0)),
            pl.BlockSpec((halo, 3 * BRANCH_W), lambda i: (jnp.minimum((i + 1) * hb, n_halo - 1), 0)),
            pl.BlockSpec((tm, BRANCH_W), lambda i: (i, 0)),
            pl.BlockSpec((tm, N_BRANCH * D_MODEL), lambda i: (i, 0)),
            pl.BlockSpec((tm, D_MODEL), lambda i: (i, 0)),
            pl.BlockSpec((1, 6, D_MODEL), mod_map),
            pl.BlockSpec((N_BRANCH, BRANCH_W, D_MODEL), lambda i: (0, 0, 0)),
            pl.BlockSpec((1, N_BRANCH * D_MODEL), const2),
            pl.BlockSpec((CONV_K, BRANCH_W), const2),
            pl.BlockSpec((D_MODEL, D_MODEL), const2),
            pl.BlockSpec((1, D_MODEL), const2),
        ],
        out_specs=pl.BlockSpec((tm, D_MODEL), lambda i: (i, 0)),
        scratch_shapes=[pltpu.VMEM((tm, D_MODEL), BF16)],
        compiler_params=_cparams(1),
        name="branch_merge",
    )(oa, conv, conv, conv, oc, gate, x2d, mods, wb, b_gate, conv_w, wo, g_post)


def _ffn_kernel(x_ref, mod_ref, gpre_ref, wgu_ref, wdn_ref, gpost_ref, out_ref, act_ref):
    x = x_ref[...]
    h = (_rms(x, gpre_ref[...]) * (1.0 + mod_ref[0, 4:5, :]) + mod_ref[0, 3:4, :]).astype(BF16)
    for lo, hi in _col_chunks(FFN_DENSE):
        g = _dot(h, wgu_ref[:, lo:hi])
        u = _dot(h, wgu_ref[:, FFN_DENSE + lo:FFN_DENSE + hi])
        act_ref[:, lo:hi] = (g * jax.nn.sigmoid(g) * u).astype(BF16)
    y = _dot(act_ref[...], wdn_ref[...])
    out_ref[...] = x + mod_ref[0, 5:6, :] * _rms(y, gpost_ref[...])


def _ffn_dense(x2d, mods, g_pre, wgu, wdn, g_post, seq_len, tm):
    n_tok = x2d.shape[0]
    tiles_per_seq = seq_len // tm
    if mods.shape[0] == 1:
        mod_map = lambda i: (0, 0, 0)
    else:
        mod_map = lambda i: (i // tiles_per_seq, 0, 0)
    const2 = lambda i: (0, 0)
    return pl.pallas_call(
        _ffn_kernel,
        out_shape=jax.ShapeDtypeStruct((n_tok, D_MODEL), F32),
        grid=(n_tok // tm,),
        in_specs=[
            pl.BlockSpec((tm, D_MODEL), lambda i: (i, 0)),
            pl.BlockSpec((1, 6, D_MODEL), mod_map),
            pl.BlockSpec((1, D_MODEL), const2),
            pl.BlockSpec((D_MODEL, 2 * FFN_DENSE), const2),
            pl.BlockSpec((FFN_DENSE, D_MODEL), const2),
            pl.BlockSpec((1, D_MODEL), const2),
        ],
        out_specs=pl.BlockSpec((tm, D_MODEL), lambda i: (i, 0)),
        scratch_shapes=[pltpu.VMEM((tm, FFN_DENSE), BF16)],
        compiler_params=_cparams(1),
        name="ffn_dense",
    )(x2d, mods, g_pre, wgu, wdn, g_post)


def _router_kernel(x_ref, mod_ref, gpre_ref, wr_ref, br_ref, h_ref, e_ref, gt_ref):
    x = x_ref[...]
    h = _rms(x, gpre_ref[...]) * (1.0 + mod_ref[0, 4:5, :]) + mod_ref[0, 3:4, :]
    h_ref[...] = h
    logits = lax.dot_general(wr_ref[...], h, (((1,), (1,)), ((), ())), preferred_element_type=F32,
                             precision=lax.Precision.HIGHEST) + br_ref[...]
    eid = lax.broadcasted_iota(I32, logits.shape, 0)
    m1 = logits.max(axis=0, keepdims=True)
    i1 = jnp.min(jnp.where(logits == m1, eid, N_EXPERTS), axis=0, keepdims=True)
    rest = jnp.where(eid == i1, -jnp.inf, logits)
    m2 = rest.max(axis=0, keepdims=True)
    i2 = jnp.min(jnp.where(rest == m2, eid, N_EXPERTS), axis=0, keepdims=True)
    e2 = jnp.exp(m2 - m1)
    den = 1.0 + e2
    e_ref[...] = jnp.concatenate([i1, i2], axis=0)
    gt_ref[...] = jnp.concatenate([1.0 / den, e2 / den], axis=0)


def _router(x2d, mods, g_pre, w_router_t, b_router, seq_len, tm):
    n_tok = x2d.shape[0]
    tiles_per_seq = seq_len // tm
    const2 = lambda i: (0, 0)
    return pl.pallas_call(
        _router_kernel,
        out_shape=[
            jax.ShapeDtypeStruct((n_tok, D_MODEL), F32),
            jax.ShapeDtypeStruct((TOP_K, n_tok), I32),
            jax.ShapeDtypeStruct((TOP_K, n_tok), F32),
        ],
        grid=(n_tok // tm,),
        in_specs=[
            pl.BlockSpec((tm, D_MODEL), lambda i: (i, 0)),
            pl.BlockSpec((1, 6, D_MODEL), lambda i: (i // tiles_per_seq, 0, 0)),
            pl.BlockSpec((1, D_MODEL), const2),
            pl.BlockSpec((N_EXPERTS, D_MODEL), const2),
            pl.BlockSpec((N_EXPERTS, 1), const2),
        ],
        out_specs=[
            pl.BlockSpec((tm, D_MODEL), lambda i: (i, 0)),
            pl.BlockSpec((TOP_K, tm), lambda i: (0, i)),
            pl.BlockSpec((TOP_K, tm), lambda i: (0, i)),
        ],
        compiler_params=_cparams(1),
        name="moe_router",
    )(x2d, mods, g_pre, w_router_t, b_router)


def _rank_kernel(e_ref, rank_ref, cnt_ref, carry_ref):
    tm = e_ref.shape[1]

    @pl.when(pl.program_id(0) == 0)
    def _():
        carry_ref[...] = jnp.zeros_like(carry_ref)

    e = e_ref[...]
    eid = lax.broadcasted_iota(I32, (N_EXPERTS, tm), 0)
    oh0 = eid == e[0:1, :]
    oh1 = eid == e[1:2, :]
    oh = jnp.concatenate([oh0, oh1], axis=0).astype(F32)
    tri = (lax.broadcasted_iota(I32, (tm, tm), 0) <= lax.broadcasted_iota(I32, (tm, tm), 1)).astype(BF16)
    incl = _dot(oh.astype(BF16), tri)
    excl = incl - oh
    tot = incl[:, tm - 1:tm]
    carry = carry_ref[:, 0:1]
    rank0 = carry + excl[0:N_EXPERTS]
    rank1 = carry + tot[0:N_EXPERTS] + excl[N_EXPERTS:2 * N_EXPERTS]
    r0 = jnp.sum(jnp.where(oh0, rank0, 0.0), axis=0, keepdims=True)
    r1 = jnp.sum(jnp.where(oh1, rank1, 0.0), axis=0, keepdims=True)
    rank_ref[...] = jnp.concatenate([r0, r1], axis=0).astype(I32)
    new_carry = carry + tot[0:N_EXPERTS] + tot[N_EXPERTS:2 * N_EXPERTS]
    carry_ref[...] = jnp.broadcast_to(new_carry, carry_ref.shape)
    cnt_ref[...] = jnp.broadcast_to(new_carry, cnt_ref.shape).astype(I32)


def _ranks(e_idx, tm):
    n_tok = e_idx.shape[1]
    return pl.pallas_call(
        _rank_kernel,
        out_shape=[
            jax.ShapeDtypeStruct((TOP_K, n_tok), I32),
            jax.ShapeDtypeStruct((N_EXPERTS, LANES), I32),
        ],
        grid=(n_tok // tm,),
        in_specs=[pl.BlockSpec((TOP_K, tm), lambda i: (0, i))],
        out_specs=[
            pl.BlockSpec((TOP_K, tm), lambda i: (0, i)),
            pl.BlockSpec((N_EXPERTS, LANES), lambda i: (0, 0)),
        ],
        scratch_shapes=[pltpu.VMEM((N_EXPERTS, LANES), F32)],
        compiler_params=_cparams(1),
        name="moe_rank",
    )(e_idx)


def _dispatch_kernel(dest_ref, h_ref, xs_in_ref, xs_ref, sem):
    del xs_in_ref
    tm = h_ref.shape[0]
    n_tok = dest_ref.shape[0] // TOP_K
    base = pl.program_id(0) * tm

    def start(t, carry):
        for k in range(TOP_K):
            d = dest_ref[k * n_tok + base + t]
            pltpu.make_async_copy(h_ref.at[pl.ds(t, 1)], xs_ref.at[pl.ds(d, 1)], sem).start()
        return carry

    lax.fori_loop(0, tm, start, 0)
    for k in range(TOP_K):
        pltpu.make_async_copy(h_ref, xs_ref.at[pl.ds(0, tm)], sem).wait()


def _dispatch(dest_flat, h, n_rows, tm):
    n_tok = h.shape[0]
    xs0 = jnp.zeros((n_rows, D_MODEL), F32)
    return pl.pallas_call(
        _dispatch_kernel,
        out_shape=jax.ShapeDtypeStruct((n_rows, D_MODEL), F32),
        grid_spec=pltpu.PrefetchScalarGridSpec(
            num_scalar_prefetch=1,
            grid=(n_tok // tm,),
            in_specs=[
                pl.BlockSpec((tm, D_MODEL), lambda i, d: (i, 0)),
                pl.BlockSpec(memory_space=pl.ANY),
            ],
            out_specs=pl.BlockSpec(memory_space=pl.ANY),
            scratch_shapes=[pltpu.SemaphoreType.DMA],
        ),
        input_output_aliases={2: 0},
        compiler_params=_cparams(1),
        name="moe_dispatch",
    )(dest_flat, h, xs0)


def _expert_kernel(be_ref, nb_ref, xs_ref, wgu_ref, wdn_ref, y_ref, act_ref):
    @pl.when(pl.program_id(0) < nb_ref[0])
    def _():
        xb = xs_ref[...].astype(BF16)
        for lo, hi in _col_chunks(FFN_EXPERT):
            g = _dot(xb, wgu_ref[0, :, lo:hi])
            u = _dot(xb, wgu_ref[0, :, FFN_EXPERT + lo:FFN_EXPERT + hi])
            act_ref[:, lo:hi] = (g * jax.nn.sigmoid(g) * u).astype(BF16)
        y_ref[...] = _dot(act_ref[...], wdn_ref[0])

    @pl.when(pl.program_id(0) >= nb_ref[0])
    def _():
        y_ref[...] = jnp.zeros_like(y_ref)


def _expert_ffn(block_e, n_used, xs, wgu, wdn, bm):
    n_rows = xs.shape[0]
    n_blocks = n_rows // bm

    def blk(j, be, nb):
        return jnp.minimum(j, nb[0] - 1)

    return pl.pallas_call(
        _expert_kernel,
        out_shape=jax.ShapeDtypeStruct((n_rows, D_MODEL), F32),
        grid_spec=pltpu.PrefetchScalarGridSpec(
            num_scalar_prefetch=2,
            grid=(n_blocks,),
            in_specs=[
                pl.BlockSpec((bm, D_MODEL), lambda j, be, nb: (blk(j, be, nb), 0)),
                pl.BlockSpec((1, D_MODEL, 2 * FFN_EXPERT), lambda j, be, nb: (be[blk(j, be, nb)], 0, 0),
                             pipeline_mode=pl.Buffered(1)),
                pl.BlockSpec((1, FFN_EXPERT, D_MODEL), lambda j, be, nb: (be[blk(j, be, nb)], 0, 0),
                             pipeline_mode=pl.Buffered(1)),
            ],
            out_specs=pl.BlockSpec((bm, D_MODEL), lambda j, be, nb: (j, 0)),
            scratch_shapes=[pltpu.VMEM((bm, FFN_EXPERT), BF16)],
        ),
        compiler_params=_cparams(1),
        name="moe_experts",
    )(block_e, n_used, xs, wgu, wdn)


def _combine_kernel(dest_ref, ys_ref, gt_ref, x_ref, mod_ref, gpost_ref, out_ref, y0_ref, y1_ref, sem):
    tm = x_ref.shape[0]
    n_tok = dest_ref.shape[0] // TOP_K
    base = pl.program_id(0) * tm
    bufs = (y0_ref, y1_ref)

    def start(t, carry):
        for k in range(TOP_K):
            d = dest_ref[k * n_tok + base + t]
            pltpu.make_async_copy(ys_ref.at[pl.ds(d, 1)], bufs[k].at[pl.ds(t, 1)], sem).start()
        return carry

    lax.fori_loop(0, tm, start, 0)
    for k in range(TOP_K):
        pltpu.make_async_copy(ys_ref.at[pl.ds(0, tm)], bufs[k], sem).wait()
    gt = gt_ref[...]
    y = gt[:, 0:1] * y0_ref[...] + gt[:, 1:2] * y1_ref[...]
    out_ref[...] = x_ref[...] + mod_ref[0, 5:6, :] * _rms(y, gpost_ref[...])


def _combine(dest_flat, ys, gates_t, x2d, mods, g_post, seq_len, tm):
    n_tok = x2d.shape[0]
    tiles_per_seq = seq_len // tm
    return pl.pallas_call(
        _combine_kernel,
        out_shape=jax.ShapeDtypeStruct((n_tok, D_MODEL), F32),
        grid_spec=pltpu.PrefetchScalarGridSpec(
            num_scalar_prefetch=1,
            grid=(n_tok // tm,),
            in_specs=[
                pl.BlockSpec(memory_space=pl.ANY),
                pl.BlockSpec((tm, TOP_K), lambda i, d: (i, 0)),
                pl.BlockSpec((tm, D_MODEL), lambda i, d: (i, 0)),
                pl.BlockSpec((1, 6, D_MODEL), lambda i, d: (i // tiles_per_seq, 0, 0)),
                pl.BlockSpec((1, D_MODEL), lambda i, d: (0, 0)),
            ],
            out_specs=pl.BlockSpec((tm, D_MODEL), lambda i, d: (i, 0)),
            scratch_shapes=[
                pltpu.VMEM((tm, D_MODEL), F32),
                pltpu.VMEM((tm, D_MODEL), F32),
                pltpu.SemaphoreType.DMA,
            ],
        ),
        compiler_params=_cparams(1),
        name="moe_combine",
    )(dest_flat, ys, gates_t, x2d, mods, g_post)


def _moe(x2d, mods, g_pre, g_post, w_router, b_router, wgu, wdn, seq_len, tm):
    n_tok = x2d.shape[0]
    bm = MOE_BLOCK_ROWS
    h, e_idx, gates = _router(x2d, mods, g_pre, w_router.T, b_router.reshape(N_EXPERTS, 1), seq_len, tm)
    rank, cnt = _ranks(e_idx, tm)
    counts = cnt[:, 0]
    padded = (counts + bm - 1) // bm * bm
    ends = jnp.cumsum(padded)
    pstart = ends - padded
    dest = rank
    for e in range(N_EXPERTS):
        dest = dest + jnp.where(e_idx == e, pstart[e], 0)
    dest_flat = dest.reshape(-1).astype(I32)
    n_blocks = n_tok * TOP_K // bm + N_EXPERTS
    blk_start = jnp.arange(n_blocks, dtype=I32) * bm
    block_e = jnp.minimum(jnp.sum(blk_start[:, None] >= ends[None, :], axis=1), N_EXPERTS - 1).astype(I32)
    n_used = (ends[-1:] // bm).astype(I32)
    xs = _dispatch(dest_flat, h, n_blocks * bm, tm)
    ys = _expert_ffn(block_e, n_used, xs, wgu, wdn, bm)
    return _combine(dest_flat, ys, gates.T, x2d, mods, g_post, seq_len, tm)


def _rope_tables(n_tokens):
    pos = jnp.arange(n_tokens)
    row = (pos // GRID_W).astype(F32)
    col = (pos % GRID_W).astype(F32)
    half = HEAD_DIM // 2
    inv = 1.0 / (ROPE_BASE ** (jnp.arange(0, half, 2, dtype=F32) / half))
    ar = row[:, None] * inv[None]
    ac = col[:, None] * inv[None]
    ang = jnp.concatenate([ar, ar, ac, ac], axis=-1)
    ang = jnp.concatenate([ang, ang], axis=-1)
    sign = jnp.where((jnp.arange(LANES) & 16) == 0, -1.0, 1.0).astype(F32)
    return jnp.cos(ang), jnp.sin(ang) * sign[None]


def _arrange_w_in(w):
    scale = HEAD_DIM ** -0.5
    offs = np.cumsum([0, BRANCH_W, 128, 128, BRANCH_W, BRANCH_W, BRANCH_W, BRANCH_W, BRANCH_W, BRANCH_W, N_BRANCH * D_MODEL])
    parts = [w[:, offs[i]:offs[i + 1]] for i in range(10)]
    qa, ka, va, ux, gb, gc, qn, kn, vn, gx = parts

    def dup(t):
        return jnp.concatenate([t[:, 0:64], t[:, 0:64], t[:, 64:128], t[:, 64:128]], axis=1)

    return jnp.concatenate([qa * scale, dup(ka), dup(va), ux, gb, gc, qn * scale, kn, vn, gx], axis=1).astype(BF16)


def kernel(x, c, ctx, c_ctx, w_ada, b_ada, g_pre_mix, g_post_mix, g_pre_ffn, g_post_ffn, w_in, b_gate, sink, conv_w,
           rpb, w_branch, w_out, w_gu_dense, w_dn_dense, w_router, b_router, w_gu_moe, w_dn_moe):
    batch, seq_len, d = x.shape
    ctx_len = ctx.shape[1]
    tm = min(TOKEN_TILE, seq_len)
    tmc = min(TOKEN_TILE, ctx_len)

    n_vec = batch + 1
    n_vec_pad = -(-n_vec // 8) * 8
    cvecs = jnp.concatenate([c, c_ctx[None], jnp.zeros((n_vec_pad - n_vec, d), F32)], axis=0)
    mods = _ada_mods(cvecs, w_ada, b_ada)

    cos_t, sin_t = _rope_tables(seq_len)
    cos_c = jnp.ones((ctx_len, LANES), F32)
    sin_c = jnp.zeros((ctx_len, LANES), F32)

    x2 = x.reshape(batch * seq_len, d)
    hc2 = ctx.reshape(batch * ctx_len, d)
    for l in range(DEPTH):
        update_ctx = l < DEPTH - 1
        mods_x = mods[l, :batch]
        mods_c = mods[l, batch:batch + 1]
        w_arr = _arrange_w_in(w_in[l])
        wb = w_branch[l].astype(BF16)
        wo = w_out[l].astype(BF16)
        bg = b_gate[l].reshape(1, N_BRANCH * D_MODEL)
        g1 = g_pre_mix[l].reshape(1, d)
        g2 = g_post_mix[l].reshape(1, d)
        g3 = g_pre_ffn[l].reshape(1, d)
        g4 = g_post_ffn[l].reshape(1, d)

        qa, kva, conv, na, gate = _inproj(x2, mods_x, g1, cos_t, sin_t, w_arr, seq_len, tm)
        qa_c, kva_c, conv_c, na_c, gate_c = _inproj(hc2, mods_c, g1, cos_c, sin_c, w_arr, ctx_len, tmc)

        o_a = _win_attn(sink[l], qa, kva, kva_c, batch, seq_len, ctx_len)
        o_c = _na_attn(na, na_c, _na_bias_tables(rpb[l]), batch, seq_len, ctx_len)
        x2 = _merge(o_a, conv, o_c, gate, x2, mods_x, wb, bg, conv_w[l], wo, g2, seq_len, tm)

        if l % 2 == 0:
            wgu = w_gu_dense[l // 2].astype(BF16)
            wdn = w_dn_dense[l // 2].astype(BF16)
            x2 = _ffn_dense(x2, mods_x, g3, wgu, wdn, g4, seq_len, tm)
        else:
            x2 = _moe(x2, mods_x, g3, g4, w_router[l // 2], b_router[l // 2], w_gu_moe[l // 2].astype(BF16),
                      w_dn_moe[l // 2].astype(BF16), seq_len, tm)

        if update_ctx:
            o_a_c = _ctx_attn(sink[l], qa_c, 0, kva_c, 0, 2, True, batch, ctx_len, True)
            o_c_c = _ctx_attn(sink[l], na_c, 0, na_c, 4, 8, False, batch, ctx_len, False)
            hc2 = _merge(o_a_c, conv_c, o_c_c, gate_c, hc2, mods_c, wb, bg, conv_w[l], wo, g2, ctx_len, tmc)
            if l % 2 == 0:
                hc2 = _ffn_dense(hc2, mods_c, g3, wgu, wdn, g4, ctx_len, tmc)
            else:
                hc2 = _moe(hc2, jnp.broadcast_to(mods_c, (batch,) + mods_c.shape[1:]), g3, g4, w_router[l // 2],
                           b_router[l // 2], w_gu_moe[l // 2].astype(BF16), w_dn_moe[l // 2].astype(BF16),
                           ctx_len, tmc)
    return x2.reshape(batch, seq_len, d)
```

```python
import functools

import numpy as np
import jax
import jax.numpy as jnp
from jax import lax
from jax.experimental import pallas as pl
from jax.experimental.pallas import tpu as pltpu

F32 = jnp.float32
BF16 = jnp.bfloat16
I32 = jnp.int32

D_MODEL = 1024
DEPTH = 2
GRID_W = 64
HEAD_DIM = 64
BRANCH_W = 512
N_BRANCH = 3
A_HEADS = 8
A_KV_HEADS = 2
WINDOW = 128
CONV_K = 3
NA_HEADS = 8
NA_ROWS_MAX = 8
NA_COLS = 16
ROPE_BASE = 10000.0
FFN_DENSE = 2816
N_EXPERTS = 8
TOP_K = 2
FFN_EXPERT = 3584
RMS_EPS = 1e-6

LANES = 128
MXU_COLS = 256
VMEM_LIMIT_BYTES = 56 * 1024 * 1024

NEG_BIG = -1e30
TOKEN_TILE = 512
WIN_Q_TILE = 512
NA_ROW_GROUP = 4
MOE_BLOCK_ROWS = 512

_QA = (0, 512)
_KVA = (512, 1024)
_CONV = (1024, 2560)
_NA = (2560, 4096)
_GATE = (4096, 7168)
IN_COLS_ARRANGED = 7168


def _cparams(n_grid_dims, vmem=VMEM_LIMIT_BYTES):
    return pltpu.CompilerParams(dimension_semantics=("arbitrary",) * n_grid_dims, vmem_limit_bytes=vmem)


def _dot(a, b):
    return jnp.dot(a, b, preferred_element_type=F32)


def _dot_nt(a, b):
    return lax.dot_general(a, b, (((1,), (1,)), ((), ())), preferred_element_type=F32)


def _rms(x, g):
    return x * lax.rsqrt(jnp.mean(x * x, axis=-1, keepdims=True) + RMS_EPS) * g


def _col_chunks(n, step=512):
    return [(lo, min(lo + step, n)) for lo in range(0, n, step)]


def _ada_kernel(c_ref, w_ref, b_ref, o_ref):
    cv = c_ref[...]
    a = cv * jax.nn.sigmoid(cv)
    o_ref[0] = jnp.dot(a, w_ref[0], preferred_element_type=F32, precision=lax.Precision.HIGHEST) + b_ref[0]


def _ada_mods(cvecs, w_ada, b_ada):
    n_rows = cvecs.shape[0]
    tn = 1536
    out = pl.pallas_call(
        _ada_kernel,
        out_shape=jax.ShapeDtypeStruct((DEPTH, n_rows, 6 * D_MODEL), F32),
        grid=(DEPTH, 6 * D_MODEL // tn),
        in_specs=[
            pl.BlockSpec((n_rows, D_MODEL), lambda l, j: (0, 0)),
            pl.BlockSpec((1, D_MODEL, tn), lambda l, j: (l, 0, j)),
            pl.BlockSpec((1, 1, tn), lambda l, j: (l, 0, j)),
        ],
        out_specs=pl.BlockSpec((1, n_rows, tn), lambda l, j: (l, 0, j)),
        compiler_params=_cparams(2),
        name="ada_mod",
    )(cvecs, w_ada, b_ada.reshape(DEPTH, 1, 6 * D_MODEL))
    return out.reshape(DEPTH, n_rows, 6, D_MODEL)


def _inproj_kernel(x_ref, mod_ref, g_ref, cos_ref, sin_ref, w_ref, qa_ref, kva_ref, conv_ref, na_ref, gate_ref):
    x = x_ref[...]
    h = (_rms(x, g_ref[...]) * (1.0 + mod_ref[0, 1:2, :]) + mod_ref[0, 0:1, :]).astype(BF16)
    cos = cos_ref[...]
    sin = sin_ref[...]
    lane = lax.broadcasted_iota(I32, (1, LANES), 1)
    first_quarter = (lane & 16) == 0

    def mm(lo, hi):
        return _dot(h, w_ref[:, lo:hi])

    def rope(a):
        outs = []
        for j in range(a.shape[1] // LANES):
            c = a[:, j * LANES:(j + 1) * LANES]
            up = pltpu.roll(c, LANES - 16, 1)
            dn = pltpu.roll(c, 16, 1)
            outs.append(c * cos + jnp.where(first_quarter, up, dn) * sin)
        return jnp.concatenate(outs, axis=1)

    qa_ref[...] = rope(mm(*_QA)).astype(BF16)
    kv = mm(*_KVA)
    kva_ref[:, 0:256] = rope(kv[:, 0:256]).astype(BF16)
    kva_ref[:, 256:512] = kv[:, 256:512].astype(BF16)
    for ref, (base, end) in ((conv_ref, _CONV), (na_ref, _NA), (gate_ref, _GATE)):
        for lo, hi in _col_chunks(end - base):
            ref[:, lo:hi] = mm(base + lo, base + hi).astype(BF16)


def _inproj(x2d, mods, g_pre, cos_t, sin_t, w_arr, seq_len, tm):
    n_tok = x2d.shape[0]
    tiles_per_seq = seq_len // tm
    if mods.shape[0] == 1:
        mod_map = lambda i: (0, 0, 0)
    else:
        mod_map = lambda i: (i // tiles_per_seq, 0, 0)
    widths = (512, 512, 1536, 1536, 3072)
    return pl.pallas_call(
        _inproj_kernel,
        out_shape=[jax.ShapeDtypeStruct((n_tok, w), BF16) for w in widths],
        grid=(n_tok // tm,),
        in_specs=[
            pl.BlockSpec((tm, D_MODEL), lambda i: (i, 0)),
            pl.BlockSpec((1, 6, D_MODEL), mod_map),
            pl.BlockSpec((1, D_MODEL), lambda i: (0, 0)),
            pl.BlockSpec((tm, LANES), lambda i: (i % tiles_per_seq, 0)),
            pl.BlockSpec((tm, LANES), lambda i: (i % tiles_per_seq, 0)),
            pl.BlockSpec((D_MODEL, IN_COLS_ARRANGED), lambda i: (0, 0)),
        ],
        out_specs=[pl.BlockSpec((tm, w), lambda i: (i, 0)) for w in widths],
        compiler_params=_cparams(1),
        name="in_proj",
    )(x2d, mods, g_pre, cos_t, sin_t, w_arr)


def _pair_stack(q, even):
    zero = jnp.zeros_like(q)
    return jnp.concatenate([jnp.where(even, q, zero), jnp.where(even, zero, q)], axis=0)


def _softmax_pv(score_parts, value_parts, sink_col):
    m = score_parts[0].max(axis=1, keepdims=True)
    for s in score_parts[1:]:
        m = jnp.maximum(m, s.max(axis=1, keepdims=True))
    if sink_col is not None:
        m = jnp.maximum(m, sink_col)
    den = None
    acc = None
    for s, v in zip(score_parts, value_parts):
        e = jnp.exp(s - m)
        d = e.sum(axis=1, keepdims=True)
        den = d if den is None else den + d
        r = _dot(e.astype(BF16), v)
        acc = r if acc is None else acc + r
    if sink_col is not None:
        den = den + jnp.exp(sink_col - m)
    return acc / den


def _win_attn_kernel(sink_ref, q_ref, kp_ref, km_ref, kn_ref, vp_ref, vm_ref, vn_ref, kc_ref, vc_ref, o_ref,
                     *, seq_len, tq):
    blk0 = pl.program_id(1) * (tq // WINDOW)
    kh = pl.program_id(2)
    kspan = jnp.concatenate([kp_ref[...], km_ref[...], kn_ref[...]], axis=0)
    vspan = jnp.concatenate([vp_ref[...], vm_ref[...], vn_ref[...]], axis=0)
    kc = kc_ref[...]
    vc = vc_ref[...]
    lane = lax.broadcasted_iota(I32, (1, LANES), 1)
    even = lane < HEAD_DIM
    n_stack = 4 * WINDOW
    rows = lax.broadcasted_iota(I32, (n_stack, 1), 0)
    piece = rows // WINDOW
    qi = rows - piece * WINDOW
    kj = lax.broadcasted_iota(I32, (1, 3 * WINDOW), 1)
    rel = kj - WINDOW - qi
    in_window = (rel <= WINDOW) & (rel >= -WINDOW)
    sink_col = jnp.zeros((n_stack, 1), F32)
    for j in range(4):
        sink_col = jnp.where(piece == j, sink_ref[kh * 4 + j], sink_col)
    for n in range(tq // WINDOW):
        qb = q_ref[n * WINDOW:(n + 1) * WINDOW, :]
        lq = jnp.concatenate([_pair_stack(qb[:, 0:LANES], even), _pair_stack(qb[:, LANES:2 * LANES], even)], axis=0)
        kw = kspan[n * WINDOW:(n + 3) * WINDOW]
        vw = vspan[n * WINDOW:(n + 3) * WINDOW]
        kpos = (blk0 + n - 1) * WINDOW + kj
        valid = in_window & (kpos >= 0) & (kpos < seq_len)
        s_w = jnp.where(valid, _dot_nt(lq, kw), NEG_BIG)
        s_c = _dot_nt(lq, kc)
        r = _softmax_pv([s_w, s_c], [vw, vc], sink_col)
        o_ref[n * WINDOW:(n + 1) * WINDOW, 0:LANES] = jnp.where(even, r[0:WINDOW], r[WINDOW:2 * WINDOW]).astype(BF16)
        o_ref[n * WINDOW:(n + 1) * WINDOW, LANES:2 * LANES] = jnp.where(
            even, r[2 * WINDOW:3 * WINDOW], r[3 * WINDOW:4 * WINDOW]).astype(BF16)


def _win_attn(sink, qa, kva, kva_c, batch, seq_len, ctx_len):
    tq = min(WIN_Q_TILE, seq_len)
    nblk = seq_len // WINDOW
    per = tq // WINDOW
    nq = seq_len // tq

    def main_map(lane_off):
        return lambda b, i, kh, s: (b * nq + i, lane_off + kh)

    def prev_map(lane_off):
        return lambda b, i, kh, s: (b * nblk + jnp.maximum(i * per - 1, 0), lane_off + kh)

    def next_map(lane_off):
        return lambda b, i, kh, s: (b * nblk + jnp.minimum((i + 1) * per, nblk - 1), lane_off + kh)

    def ctx_map(lane_off):
        return lambda b, i, kh, s: (b, lane_off + kh)

    kern = functools.partial(_win_attn_kernel, seq_len=seq_len, tq=tq)
    return pl.pallas_call(
        kern,
        out_shape=jax.ShapeDtypeStruct((batch * seq_len, BRANCH_W), BF16),
        grid_spec=pltpu.PrefetchScalarGridSpec(
            num_scalar_prefetch=1,
            grid=(batch, nq, A_KV_HEADS),
            in_specs=[
                pl.BlockSpec((tq, 2 * LANES), lambda b, i, kh, s: (b * nq + i, kh)),
                pl.BlockSpec((WINDOW, LANES), prev_map(0)),
                pl.BlockSpec((tq, LANES), main_map(0)),
                pl.BlockSpec((WINDOW, LANES), next_map(0)),
                pl.BlockSpec((WINDOW, LANES), prev_map(2)),
                pl.BlockSpec((tq, LANES), main_map(2)),
                pl.BlockSpec((WINDOW, LANES), next_map(2)),
                pl.BlockSpec((ctx_len, LANES), ctx_map(0)),
                pl.BlockSpec((ctx_len, LANES), ctx_map(2)),
            ],
            out_specs=pl.BlockSpec((tq, 2 * LANES), lambda b, i, kh, s: (b * nq + i, kh)),
        ),
        compiler_params=_cparams(3),
        name="win_attn",
    )(sink, qa, kva, kva, kva, kva, kva, kva, kva_c, kva_c)


def _ctx_attn_kernel(sink_ref, q_ref, k_ref, v_ref, o_ref, *, use_sink):
    p = pl.program_id(1)
    n = q_ref.shape[0]
    lane = lax.broadcasted_iota(I32, (1, LANES), 1)
    even = lane < HEAD_DIM
    lq = _pair_stack(q_ref[...], even)
    s = _dot_nt(lq, k_ref[...])
    sink_col = None
    if use_sink:
        rows = lax.broadcasted_iota(I32, (2 * n, 1), 0)
        sink_col = jnp.where(rows < n, sink_ref[2 * p], sink_ref[2 * p + 1])
    r = _softmax_pv([s], [v_ref[...]], sink_col)
    o_ref[...] = jnp.where(even, r[0:n], r[n:2 * n]).astype(BF16)


def _ctx_attn(sink, q_arr, q_lane0, kv_arr, k_lane0, v_lane0, kv_shared, batch, ctx_len, use_sink):
    def kv_map(lane0):
        if kv_shared:
            return lambda b, p, s: (b, lane0 + p // 2)
        return lambda b, p, s: (b, lane0 + p)

    kern = functools.partial(_ctx_attn_kernel, use_sink=use_sink)
    return pl.pallas_call(
        kern,
        out_shape=jax.ShapeDtypeStruct((batch * ctx_len, BRANCH_W), BF16),
        grid_spec=pltpu.PrefetchScalarGridSpec(
            num_scalar_prefetch=1,
            grid=(batch, BRANCH_W // LANES),
            in_specs=[
                pl.BlockSpec((ctx_len, LANES), lambda b, p, s: (b, q_lane0 + p)),
                pl.BlockSpec((ctx_len, LANES), kv_map(k_lane0)),
                pl.BlockSpec((ctx_len, LANES), kv_map(v_lane0)),
            ],
            out_specs=pl.BlockSpec((ctx_len, LANES), lambda b, p, s: (b, p)),
        ),
        compiler_params=_cparams(2),
        name="ctx_attn_sink" if use_sink else "ctx_attn",
    )(sink, q_arr, kv_arr, kv_arr)


def _na_kernel(q_ref, kp_ref, km_ref, kn_ref, vp_ref, vm_ref, vn_ref, kc_ref, vc_ref, b0_ref, b1_ref, o_ref):
    gq = NA_ROW_GROUP * GRID_W
    lane = lax.broadcasted_iota(I32, (1, LANES), 1)
    even = lane < HEAD_DIM
    kc = kc_ref[...]
    vc = vc_ref[...]
    spans = (
        (jnp.concatenate([kp_ref[gq:2 * gq], km_ref[...]], axis=0), jnp.concatenate([vp_ref[gq:2 * gq], vm_ref[...]], axis=0)),
        (jnp.concatenate([km_ref[...], kn_ref[0:gq]], axis=0), jnp.concatenate([vm_ref[...], vn_ref[0:gq]], axis=0)),
    )
    for g, b_ref in enumerate((b0_ref, b1_ref)):
        kspan, vspan = spans[g]
        lq = _pair_stack(q_ref[g * gq:(g + 1) * gq, :], even)
        bias = jnp.concatenate([b_ref[0, 0], b_ref[0, 1]], axis=0)
        s_b = _dot_nt(lq, kspan) + bias
        s_c = _dot_nt(lq, kc)
        r = _softmax_pv([s_b, s_c], [vspan, vc], None)
        o_ref[g * gq:(g + 1) * gq, :] = jnp.where(even, r[0:gq], r[gq:2 * gq]).astype(BF16)


_N_DR = 2 * NA_ROWS_MAX - 1
_N_DC = 2 * NA_COLS - 1


def _na_bias_kernel(rpb_ref, o_ref):
    h = pl.program_id(0)
    qc = lax.broadcasted_iota(I32, (GRID_W, LANES), 0)
    lane = lax.broadcasted_iota(I32, (GRID_W, LANES), 1)
    kc = lane & (GRID_W - 1)
    dc = jnp.clip(kc - qc, -(NA_COLS - 1), NA_COLS - 1) + NA_COLS - 1
    col_start = jnp.clip(qc - NA_COLS // 2, 0, GRID_W - NA_COLS)
    col_ok = (kc >= col_start) & (kc < col_start + NA_COLS)
    neg = jnp.full((GRID_W, LANES), NEG_BIG, F32)
    col_bias = []
    for dr in range(_N_DR):
        c = neg
        for d in range(_N_DC):
            c = jnp.where(dc == d, rpb_ref[(h * _N_DR + dr) * _N_DC + d], c)
        col_bias.append(jnp.where(col_ok, c, neg))
    row_ok = (
        lambda i, j: 4 <= j < 4 + NA_ROWS_MAX,
        lambda i, j: i <= j < i + NA_ROWS_MAX,
        lambda i, j: 0 <= j < NA_ROWS_MAX,
    )
    for v in range(3):
        for i in range(NA_ROW_GROUP):
            for m in range(3 * NA_ROW_GROUP // 2):
                halves = [col_bias[j - i + 3] if row_ok[v](i, j) else neg for j in (2 * m, 2 * m + 1)]
                o_ref[v, 0, i * GRID_W:(i + 1) * GRID_W, m * LANES:(m + 1) * LANES] = jnp.where(
                    lane < GRID_W, halves[0], halves[1])


def _na_bias_tables(rpb):
    shape = (3, NA_HEADS, NA_ROW_GROUP * GRID_W, 3 * NA_ROW_GROUP * GRID_W)
    return pl.pallas_call(
        _na_bias_kernel,
        out_shape=jax.ShapeDtypeStruct(shape, F32),
        grid_spec=pltpu.PrefetchScalarGridSpec(
            num_scalar_prefetch=1,
            grid=(NA_HEADS,),
            in_specs=[],
            out_specs=pl.BlockSpec((3, 1) + shape[2:], lambda h, r: (0, h, 0, 0)),
        ),
        compiler_params=_cparams(1),
        name="na_bias",
    )(rpb.reshape(-1))


def _na_attn(na, na_c, bias_tab, batch, seq_len, ctx_len):
    rows = seq_len // GRID_W
    assert rows % (2 * NA_ROW_GROUP) == 0 and rows >= 4 * NA_ROW_GROUP
    tq = 2 * NA_ROW_GROUP * GRID_W
    nq = seq_len // tq
    n_pairs = NA_HEADS // 2

    def tok_map(lane0, shift):
        return lambda b, i, p: (b * nq + jnp.clip(i + shift, 0, nq - 1), lane0 + p)

    def ctx_map(lane0):
        return lambda b, i, p: (b, lane0 + p)

    kq = BRANCH_W // LANES
    bias_block = (1, 2, NA_ROW_GROUP * GRID_W, 3 * NA_ROW_GROUP * GRID_W)
    return pl.pallas_call(
        _na_kernel,
        out_shape=jax.ShapeDtypeStruct((batch * seq_len, BRANCH_W), BF16),
        grid=(batch, nq, n_pairs),
        in_specs=[
            pl.BlockSpec((tq, LANES), tok_map(0, 0)),
            pl.BlockSpec((tq, LANES), tok_map(kq, -1)),
            pl.BlockSpec((tq, LANES), tok_map(kq, 0)),
            pl.BlockSpec((tq, LANES), tok_map(kq, 1)),
            pl.BlockSpec((tq, LANES), tok_map(2 * kq, -1)),
            pl.BlockSpec((tq, LANES), tok_map(2 * kq, 0)),
            pl.BlockSpec((tq, LANES), tok_map(2 * kq, 1)),
            pl.BlockSpec((ctx_len, LANES), ctx_map(kq)),
            pl.BlockSpec((ctx_len, LANES), ctx_map(2 * kq)),
            pl.BlockSpec(bias_block, lambda b, i, p: (jnp.where(i == 0, 0, 1), p, 0, 0)),
            pl.BlockSpec(bias_block, lambda b, i, p: (jnp.where(i == nq - 1, 2, 1), p, 0, 0)),
        ],
        out_specs=pl.BlockSpec((tq, LANES), lambda b, i, p: (b * nq + i, p)),
        compiler_params=_cparams(3),
        name="na_attn",
    )(na, na, na, na, na, na, na, na_c, na_c, bias_tab, bias_tab)


def _merge_kernel(oa_ref, conv_ref, cprev_ref, cnext_ref, oc_ref, gate_ref, x_ref, mod_ref, wb_ref, bg_ref, cw_ref,
                  wo_ref, gpost_ref, out_ref, y_ref, *, tiles_per_seq):
    tm = x_ref.shape[0]
    ti = pl.program_id(0) % tiles_per_seq
    ux = conv_ref[:, 0:512].astype(F32)
    gb = conv_ref[:, 512:1024].astype(F32)
    gc = conv_ref[:, 1024:1536].astype(F32)
    u = gc * ux
    cprev = cprev_ref[...].astype(F32)
    cnext = cnext_ref[...].astype(F32)
    halo = cprev.shape[0]
    u_before = cprev[halo - 1:halo, 1024:1536] * cprev[halo - 1:halo, 0:512]
    u_after = cnext[0:1, 1024:1536] * cnext[0:1, 0:512]
    u_before = jnp.where(ti == 0, 0.0, u_before)
    u_after = jnp.where(ti == tiles_per_seq - 1, 0.0, u_after)
    row = lax.broadcasted_iota(I32, (tm, 1), 0)
    u_dn = jnp.where(row == 0, u_before, pltpu.roll(u, 1, 0))
    u_up = jnp.where(row == tm - 1, u_after, pltpu.roll(u, tm - 1, 0))
    cw = cw_ref[...]
    ob = (gb * (cw[0:1] * u_dn + cw[1:2] * u + cw[2:3] * u_up)).astype(BF16)
    oa = oa_ref[...]
    oc = oc_ref[...]
    for lo, hi in _col_chunks(D_MODEL):
        acc = None
        for r, o in enumerate((oa, ob, oc)):
            gate = jax.nn.sigmoid(gate_ref[:, r * D_MODEL + lo:r * D_MODEL + hi].astype(F32)
                                  + bg_ref[:, r * D_MODEL + lo:r * D_MODEL + hi])
            term = gate * _dot(o, wb_ref[r, :, lo:hi])
            acc = term if acc is None else acc + term
        y_ref[:, lo:hi] = acc.astype(BF16)
    y2 = _dot(y_ref[...], wo_ref[...])
    out_ref[...] = x_ref[...] + mod_ref[0, 2:3, :] * _rms(y2, gpost_ref[...])


def _merge(oa, conv, oc, gate, x2d, mods, wb, b_gate, conv_w, wo, g_post, seq_len, tm):
    n_tok = x2d.shape[0]
    tiles_per_seq = seq_len // tm
    halo = 16
    hb = tm // halo
    n_halo = n_tok // halo
    if mods.shape[0] == 1:
        mod_map = lambda i: (0, 0, 0)
    else:
        mod_map = lambda i: (i // tiles_per_seq, 0, 0)
    kern = functools.partial(_merge_kernel, tiles_per_seq=tiles_per_seq)
    const2 = lambda i: (0, 0)
    return pl.pallas_call(
        kern,
        out_shape=jax.ShapeDtypeStruct((n_tok, D_MODEL), F32),
        grid=(n_tok // tm,),
        in_specs=[
            pl.BlockSpec((tm, BRANCH_W), lambda i: (i, 0)),
            pl.BlockSpec((tm, 3 * BRANCH_W), lambda i: (i, 0)),
            pl.BlockSpec((halo, 3 * BRANCH_W), lambda i: (jnp.maximum(i * hb - 1, 0), 0)),
            pl.BlockSpec((halo, 3 * BRANCH_W), lambda i: (jnp.minimum((i + 1) * hb, n_halo - 1), 0)),
            pl.BlockSpec((tm, BRANCH_W), lambda i: (i, 0)),
            pl.BlockSpec((tm, N_BRANCH * D_MODEL), lambda i: (i, 0)),
            pl.BlockSpec((tm, D_MODEL), lambda i: (i, 0)),
            pl.BlockSpec((1, 6, D_MODEL), mod_map),
            pl.BlockSpec((N_BRANCH, BRANCH_W, D_MODEL), lambda i: (0, 0, 0)),
            pl.BlockSpec((1, N_BRANCH * D_MODEL), const2),
            pl.BlockSpec((CONV_K, BRANCH_W), const2),
            pl.BlockSpec((D_MODEL, D_MODEL), const2),
            pl.BlockSpec((1, D_MODEL), const2),
        ],
        out_specs=pl.BlockSpec((tm, D_MODEL), lambda i: (i, 0)),
        scratch_shapes=[pltpu.VMEM((tm, D_MODEL), BF16)],
        compiler_params=_cparams(1),
        name="branch_merge",
    )(oa, conv, conv, conv, oc, gate, x2d, mods, wb, b_gate, conv_w, wo, g_post)


def _ffn_kernel(x_ref, mod_ref, gpre_ref, wgu_ref, wdn_ref, gpost_ref, out_ref, act_ref):
    x = x_ref[...]
    h = (_rms(x, gpre_ref[...]) * (1.0 + mod_ref[0, 4:5, :]) + mod_ref[0, 3:4, :]).astype(BF16)
    for lo, hi in _col_chunks(FFN_DENSE):
        g = _dot(h, wgu_ref[:, lo:hi])
        u = _dot(h, wgu_ref[:, FFN_DENSE + lo:FFN_DENSE + hi])
        act_ref[:, lo:hi] = (g * jax.nn.sigmoid(g) * u).astype(BF16)
    y = _dot(act_ref[...], wdn_ref[...])
    out_ref[...] = x + mod_ref[0, 5:6, :] * _rms(y, gpost_ref[...])


def _ffn_dense(x2d, mods, g_pre, wgu, wdn, g_post, seq_len, tm):
    n_tok = x2d.shape[0]
    tiles_per_seq = seq_len // tm
    if mods.shape[0] == 1:
        mod_map = lambda i: (0, 0, 0)
    else:
        mod_map = lambda i: (i // tiles_per_seq, 0, 0)
    const2 = lambda i: (0, 0)
    return pl.pallas_call(
        _ffn_kernel,
        out_shape=jax.ShapeDtypeStruct((n_tok, D_MODEL), F32),
        grid=(n_tok // tm,),
        in_specs=[
            pl.BlockSpec((tm, D_MODEL), lambda i: (i, 0)),
            pl.BlockSpec((1, 6, D_MODEL), mod_map),
            pl.BlockSpec((1, D_MODEL), const2),
            pl.BlockSpec((D_MODEL, 2 * FFN_DENSE), const2),
            pl.BlockSpec((FFN_DENSE, D_MODEL), const2),
            pl.BlockSpec((1, D_MODEL), const2),
        ],
        out_specs=pl.BlockSpec((tm, D_MODEL), lambda i: (i, 0)),
        scratch_shapes=[pltpu.VMEM((tm, FFN_DENSE), BF16)],
        compiler_params=_cparams(1),
        name="ffn_dense",
    )(x2d, mods, g_pre, wgu, wdn, g_post)


def _router_kernel(x_ref, mod_ref, gpre_ref, wr_ref, br_ref, h_ref, e_ref, gt_ref):
    x = x_ref[...]
    h = _rms(x, gpre_ref[...]) * (1.0 + mod_ref[0, 4:5, :]) + mod_ref[0, 3:4, :]
    h_ref[...] = h
    logits = lax.dot_general(wr_ref[...], h, (((1,), (1,)), ((), ())), preferred_element_type=F32,
                             precision=lax.Precision.HIGHEST) + br_ref[...]
    eid = lax.broadcasted_iota(I32, logits.shape, 0)
    m1 = logits.max(axis=0, keepdims=True)
    i1 = jnp.min(jnp.where(logits == m1, eid, N_EXPERTS), axis=0, keepdims=True)
    rest = jnp.where(eid == i1, -jnp.inf, logits)
    m2 = rest.max(axis=0, keepdims=True)
    i2 = jnp.min(jnp.where(rest == m2, eid, N_EXPERTS), axis=0, keepdims=True)
    e2 = jnp.exp(m2 - m1)
    den = 1.0 + e2
    e_ref[...] = jnp.concatenate([i1, i2], axis=0)
    gt_ref[...] = jnp.concatenate([1.0 / den, e2 / den], axis=0)


def _router(x2d, mods, g_pre, w_router_t, b_router, seq_len, tm):
    n_tok = x2d.shape[0]
    tiles_per_seq = seq_len // tm
    const2 = lambda i: (0, 0)
    return pl.pallas_call(
        _router_kernel,
        out_shape=[
            jax.ShapeDtypeStruct((n_tok, D_MODEL), F32),
            jax.ShapeDtypeStruct((TOP_K, n_tok), I32),
            jax.ShapeDtypeStruct((TOP_K, n_tok), F32),
        ],
        grid=(n_tok // tm,),
        in_specs=[
            pl.BlockSpec((tm, D_MODEL), lambda i: (i, 0)),
            pl.BlockSpec((1, 6, D_MODEL), lambda i: (i // tiles_per_seq, 0, 0)),
            pl.BlockSpec((1, D_MODEL), const2),
            pl.BlockSpec((N_EXPERTS, D_MODEL), const2),
            pl.BlockSpec((N_EXPERTS, 1), const2),
        ],
        out_specs=[
            pl.BlockSpec((tm, D_MODEL), lambda i: (i, 0)),
            pl.BlockSpec((TOP_K, tm), lambda i: (0, i)),
            pl.BlockSpec((TOP_K, tm), lambda i: (0, i)),
        ],
        compiler_params=_cparams(1),
        name="moe_router",
    )(x2d, mods, g_pre, w_router_t, b_router)


def _rank_kernel(e_ref, rank_ref, cnt_ref, carry_ref):
    tm = e_ref.shape[1]

    @pl.when(pl.program_id(0) == 0)
    def _():
        carry_ref[...] = jnp.zeros_like(carry_ref)

    e = e_ref[...]
    eid = lax.broadcasted_iota(I32, (N_EXPERTS, tm), 0)
    oh0 = eid == e[0:1, :]
    oh1 = eid == e[1:2, :]
    oh = jnp.concatenate([oh0, oh1], axis=0).astype(F32)
    tri = (lax.broadcasted_iota(I32, (tm, tm), 0) <= lax.broadcasted_iota(I32, (tm, tm), 1)).astype(BF16)
    incl = _dot(oh.astype(BF16), tri)
    excl = incl - oh
    tot = incl[:, tm - 1:tm]
    carry = carry_ref[:, 0:1]
    rank0 = carry + excl[0:N_EXPERTS]
    rank1 = carry + tot[0:N_EXPERTS] + excl[N_EXPERTS:2 * N_EXPERTS]
    r0 = jnp.sum(jnp.where(oh0, rank0, 0.0), axis=0, keepdims=True)
    r1 = jnp.sum(jnp.where(oh1, rank1, 0.0), axis=0, keepdims=True)
    rank_ref[...] = jnp.concatenate([r0, r1], axis=0).astype(I32)
    new_carry = carry + tot[0:N_EXPERTS] + tot[N_EXPERTS:2 * N_EXPERTS]
    carry_ref[...] = jnp.broadcast_to(new_carry, carry_ref.shape)
    cnt_ref[...] = jnp.broadcast_to(new_carry, cnt_ref.shape).astype(I32)


def _ranks(e_idx, tm):
    n_tok = e_idx.shape[1]
    return pl.pallas_call(
        _rank_kernel,
        out_shape=[
            jax.ShapeDtypeStruct((TOP_K, n_tok), I32),
            jax.ShapeDtypeStruct((N_EXPERTS, LANES), I32),
        ],
        grid=(n_tok // tm,),
        in_specs=[pl.BlockSpec((TOP_K, tm), lambda i: (0, i))],
        out_specs=[
            pl.BlockSpec((TOP_K, tm), lambda i: (0, i)),
            pl.BlockSpec((N_EXPERTS, LANES), lambda i: (0, 0)),
        ],
        scratch_shapes=[pltpu.VMEM((N_EXPERTS, LANES), F32)],
        compiler_params=_cparams(1),
        name="moe_rank",
    )(e_idx)


def _dispatch_kernel(dest_ref, pad_ref, h_ref, xs_ref, zero_ref, sem, zsem):
    tm = h_ref.shape[0]
    n_tok = dest_ref.shape[0] // TOP_K
    base = pl.program_id(0) * tm

    @pl.when(pl.program_id(0) == 0)
    def _():
        zero_ref[...] = jnp.zeros_like(zero_ref)

        zrows = zero_ref.shape[0]

        def zero_copy(r):
            return pltpu.make_async_copy(zero_ref.at[pl.ds(0, 1)], xs_ref.at[pl.ds(r, 1)], zsem)

        def zero_chunk(j):
            return pltpu.make_async_copy(zero_ref, xs_ref.at[pl.ds(pl.multiple_of(j * zrows, zrows), zrows)], zsem)

        tail = (pad_ref[2 * N_EXPERTS] // zrows, xs_ref.shape[0] // zrows)
        for e in range(N_EXPERTS):
            lo, hi = pad_ref[e], pad_ref[N_EXPERTS + e]
            lax.fori_loop(lo, hi, lambda r, c: (zero_copy(r).start(), c)[1], 0)
        lax.fori_loop(tail[0], tail[1], lambda j, c: (zero_chunk(j).start(), c)[1], 0)
        for e in range(N_EXPERTS):
            lo, hi = pad_ref[e], pad_ref[N_EXPERTS + e]
            lax.fori_loop(lo, hi, lambda r, c: (zero_copy(r).wait(), c)[1], 0)
        lax.fori_loop(tail[0], tail[1], lambda j, c: (zero_chunk(j).wait(), c)[1], 0)

    def start(t, carry):
        for k in range(TOP_K):
            d = dest_ref[k * n_tok + base + t]
            pltpu.make_async_copy(h_ref.at[pl.ds(t, 1)], xs_ref.at[pl.ds(d, 1)], sem).start()
        return carry

    lax.fori_loop(0, tm, start, 0)
    for k in range(TOP_K):
        pltpu.make_async_copy(h_ref, xs_ref.at[pl.ds(0, tm)], sem).wait()


def _dispatch(dest_flat, pad_rows, h, n_rows, tm):
    n_tok = h.shape[0]
    return pl.pallas_call(
        _dispatch_kernel,
        out_shape=jax.ShapeDtypeStruct((n_rows, D_MODEL), F32),
        grid_spec=pltpu.PrefetchScalarGridSpec(
            num_scalar_prefetch=2,
            grid=(n_tok // tm,),
            in_specs=[pl.BlockSpec((tm, D_MODEL), lambda i, d, p: (i, 0))],
            out_specs=pl.BlockSpec(memory_space=pl.ANY),
            scratch_shapes=[
                pltpu.VMEM((64, D_MODEL), F32),
                pltpu.SemaphoreType.DMA,
                pltpu.SemaphoreType.DMA,
            ],
        ),
        compiler_params=_cparams(1),
        name="moe_dispatch",
    )(dest_flat, pad_rows, h)


def _expert_kernel(be_ref, nb_ref, xs_ref, wgu_ref, wdn_ref, y_ref, act_ref):
    @pl.when(pl.program_id(0) < nb_ref[0])
    def _():
        xb = xs_ref[...].astype(BF16)
        for lo, hi in _col_chunks(FFN_EXPERT):
            g = _dot(xb, wgu_ref[0, :, lo:hi])
            u = _dot(xb, wgu_ref[0, :, FFN_EXPERT + lo:FFN_EXPERT + hi])
            act_ref[:, lo:hi] = (g * jax.nn.sigmoid(g) * u).astype(BF16)
        y_ref[...] = _dot(act_ref[...], wdn_ref[0])

    @pl.when(pl.program_id(0) >= nb_ref[0])
    def _():
        y_ref[...] = jnp.zeros_like(y_ref)


def _expert_ffn(block_e, n_used, xs, wgu, wdn, bm):
    n_rows = xs.shape[0]
    n_blocks = n_rows // bm

    def blk(j, be, nb):
        return jnp.minimum(j, nb[0] - 1)

    return pl.pallas_call(
        _expert_kernel,
        out_shape=jax.ShapeDtypeStruct((n_rows, D_MODEL), F32),
        grid_spec=pltpu.PrefetchScalarGridSpec(
            num_scalar_prefetch=2,
            grid=(n_blocks,),
            in_specs=[
                pl.BlockSpec((bm, D_MODEL), lambda j, be, nb: (blk(j, be, nb), 0)),
                pl.BlockSpec((1, D_MODEL, 2 * FFN_EXPERT), lambda j, be, nb: (be[blk(j, be, nb)], 0, 0),
                             pipeline_mode=pl.Buffered(1)),
                pl.BlockSpec((1, FFN_EXPERT, D_MODEL), lambda j, be, nb: (be[blk(j, be, nb)], 0, 0),
                             pipeline_mode=pl.Buffered(1)),
            ],
            out_specs=pl.BlockSpec((bm, D_MODEL), lambda j, be, nb: (j, 0)),
            scratch_shapes=[pltpu.VMEM((bm, FFN_EXPERT), BF16)],
        ),
        compiler_params=_cparams(1),
        name="moe_experts",
    )(block_e, n_used, xs, wgu, wdn)


def _combine_kernel(dest_ref, ys_ref, gt_ref, x_ref, mod_ref, gpost_ref, out_ref, y0_ref, y1_ref, sem):
    tm = x_ref.shape[0]
    n_tok = dest_ref.shape[0] // TOP_K
    base = pl.program_id(0) * tm
    bufs = (y0_ref, y1_ref)

    def start(t, carry):
        for k in range(TOP_K):
            d = dest_ref[k * n_tok + base + t]
            pltpu.make_async_copy(ys_ref.at[pl.ds(d, 1)], bufs[k].at[pl.ds(t, 1)], sem).start()
        return carry

    lax.fori_loop(0, tm, start, 0)
    for k in range(TOP_K):
        pltpu.make_async_copy(ys_ref.at[pl.ds(0, tm)], bufs[k], sem).wait()
    gt = gt_ref[...]
    y = gt[:, 0:1] * y0_ref[...] + gt[:, 1:2] * y1_ref[...]
    out_ref[...] = x_ref[...] + mod_ref[0, 5:6, :] * _rms(y, gpost_ref[...])


def _combine(dest_flat, ys, gates_t, x2d, mods, g_post, seq_len, tm):
    n_tok = x2d.shape[0]
    tiles_per_seq = seq_len // tm
    return pl.pallas_call(
        _combine_kernel,
        out_shape=jax.ShapeDtypeStruct((n_tok, D_MODEL), F32),
        grid_spec=pltpu.PrefetchScalarGridSpec(
            num_scalar_prefetch=1,
            grid=(n_tok // tm,),
            in_specs=[
                pl.BlockSpec(memory_space=pl.ANY),
                pl.BlockSpec((tm, TOP_K), lambda i, d: (i, 0)),
                pl.BlockSpec((tm, D_MODEL), lambda i, d: (i, 0)),
                pl.BlockSpec((1, 6, D_MODEL), lambda i, d: (i // tiles_per_seq, 0, 0)),
                pl.BlockSpec((1, D_MODEL), lambda i, d: (0, 0)),
            ],
            out_specs=pl.BlockSpec((tm, D_MODEL), lambda i, d: (i, 0)),
            scratch_shapes=[
                pltpu.VMEM((tm, D_MODEL), F32),
                pltpu.VMEM((tm, D_MODEL), F32),
                pltpu.SemaphoreType.DMA,
            ],
        ),
        compiler_params=_cparams(1),
        name="moe_combine",
    )(dest_flat, ys, gates_t, x2d, mods, g_post)


def _moe(x2d, mods, g_pre, g_post, w_router, b_router, wgu, wdn, seq_len, tm):
    n_tok = x2d.shape[0]
    bm = MOE_BLOCK_ROWS
    h, e_idx, gates = _router(x2d, mods, g_pre, w_router.T, b_router.reshape(N_EXPERTS, 1), seq_len, tm)
    rank, cnt = _ranks(e_idx, tm)
    counts = cnt[:, 0]
    padded = (counts + bm - 1) // bm * bm
    ends = jnp.cumsum(padded)
    pstart = ends - padded
    dest = rank
    for e in range(N_EXPERTS):
        dest = dest + jnp.where(e_idx == e, pstart[e], 0)
    dest_flat = dest.reshape(-1).astype(I32)
    n_blocks = n_tok * TOP_K // bm + N_EXPERTS
    blk_start = jnp.arange(n_blocks, dtype=I32) * bm
    block_e = jnp.minimum(jnp.sum(blk_start[:, None] >= ends[None, :], axis=1), N_EXPERTS - 1).astype(I32)
    n_used = (ends[-1:] // bm).astype(I32)
    pad_rows = jnp.concatenate([pstart + counts, ends, ends[-1:]]).astype(I32)
    xs = _dispatch(dest_flat, pad_rows, h, n_blocks * bm, tm)
    ys = _expert_ffn(block_e, n_used, xs, wgu, wdn, bm)
    return _combine(dest_flat, ys, gates.T, x2d, mods, g_post, seq_len, tm)


def _rope_tables(n_tokens):
    pos = jnp.arange(n_tokens)
    row = (pos // GRID_W).astype(F32)
    col = (pos % GRID_W).astype(F32)
    half = HEAD_DIM // 2
    inv = 1.0 / (ROPE_BASE ** (jnp.arange(0, half, 2, dtype=F32) / half))
    ar = row[:, None] * inv[None]
    ac = col[:, None] * inv[None]
    ang = jnp.concatenate([ar, ar, ac, ac], axis=-1)
    ang = jnp.concatenate([ang, ang], axis=-1)
    sign = jnp.where((jnp.arange(LANES) & 16) == 0, -1.0, 1.0).astype(F32)
    return jnp.cos(ang), jnp.sin(ang) * sign[None]


def _arrange_w_in(w):
    scale = HEAD_DIM ** -0.5
    offs = np.cumsum([0, BRANCH_W, 128, 128, BRANCH_W, BRANCH_W, BRANCH_W, BRANCH_W, BRANCH_W, BRANCH_W, N_BRANCH * D_MODEL])
    parts = [w[:, offs[i]:offs[i + 1]] for i in range(10)]
    qa, ka, va, ux, gb, gc, qn, kn, vn, gx = parts

    def dup(t):
        return jnp.concatenate([t[:, 0:64], t[:, 0:64], t[:, 64:128], t[:, 64:128]], axis=1)

    return jnp.concatenate([qa * scale, dup(ka), dup(va), ux, gb, gc, qn * scale, kn, vn, gx], axis=1).astype(BF16)


def kernel(x, c, ctx, c_ctx, w_ada, b_ada, g_pre_mix, g_post_mix, g_pre_ffn, g_post_ffn, w_in, b_gate, sink, conv_w,
           rpb, w_branch, w_out, w_gu_dense, w_dn_dense, w_router, b_router, w_gu_moe, w_dn_moe):
    batch, seq_len, d = x.shape
    ctx_len = ctx.shape[1]
    tm = min(TOKEN_TILE, seq_len)
    tmc = min(TOKEN_TILE, ctx_len)

    n_vec = batch + 1
    n_vec_pad = -(-n_vec // 8) * 8
    cvecs = jnp.concatenate([c, c_ctx[None], jnp.zeros((n_vec_pad - n_vec, d), F32)], axis=0)
    mods = _ada_mods(cvecs, w_ada, b_ada)

    cos_t, sin_t = _rope_tables(seq_len)
    cos_c = jnp.ones((ctx_len, LANES), F32)
    sin_c = jnp.zeros((ctx_len, LANES), F32)

    x2 = x.reshape(batch * seq_len, d)
    hc2 = ctx.reshape(batch * ctx_len, d)
    for l in range(DEPTH):
        update_ctx = l < DEPTH - 1
        mods_x = mods[l, :batch]
        mods_c = mods[l, batch:batch + 1]
        w_arr = _arrange_w_in(w_in[l])
        wb = w_branch[l].astype(BF16)
        wo = w_out[l].astype(BF16)
        bg = b_gate[l].reshape(1, N_BRANCH * D_MODEL)
        g1 = g_pre_mix[l].reshape(1, d)
        g2 = g_post_mix[l].reshape(1, d)
        g3 = g_pre_ffn[l].reshape(1, d)
        g4 = g_post_ffn[l].reshape(1, d)

        qa, kva, conv, na, gate = _inproj(x2, mods_x, g1, cos_t, sin_t, w_arr, seq_len, tm)
        qa_c, kva_c, conv_c, na_c, gate_c = _inproj(hc2, mods_c, g1, cos_c, sin_c, w_arr, ctx_len, tmc)

        o_a = _win_attn(sink[l], qa, kva, kva_c, batch, seq_len, ctx_len)
        o_c = _na_attn(na, na_c, _na_bias_tables(rpb[l]), batch, seq_len, ctx_len)
        x2 = _merge(o_a, conv, o_c, gate, x2, mods_x, wb, bg, conv_w[l], wo, g2, seq_len, tm)

        if l % 2 == 0:
            wgu = w_gu_dense[l // 2].astype(BF16)
            wdn = w_dn_dense[l // 2].astype(BF16)
            x2 = _ffn_dense(x2, mods_x, g3, wgu, wdn, g4, seq_len, tm)
        else:
            x2 = _moe(x2, mods_x, g3, g4, w_router[l // 2], b_router[l // 2], w_gu_moe[l // 2].astype(BF16),
                      w_dn_moe[l // 2].astype(BF16), seq_len, tm)

        if update_ctx:
            o_a_c = _ctx_attn(sink[l], qa_c, 0, kva_c, 0, 2, True, batch, ctx_len, True)
            o_c_c = _ctx_attn(sink[l], na_c, 0, na_c, 4, 8, False, batch, ctx_len, False)
            hc2 = _merge(o_a_c, conv_c, o_c_c, gate_c, hc2, mods_c, wb, bg, conv_w[l], wo, g2, ctx_len, tmc)
            if l % 2 == 0:
                hc2 = _ffn_dense(hc2, mods_c, g3, wgu, wdn, g4, ctx_len, tmc)
            else:
                hc2 = _moe(hc2, jnp.broadcast_to(mods_c, (batch,) + mods_c.shape[1:]), g3, g4, w_router[l // 2],
                           b_router[l // 2], w_gu_moe[l // 2].astype(BF16), w_dn_moe[l // 2].astype(BF16),
                           ctx_len, tmc)
    return x2.reshape(batch, seq_len, d)
```

```python
import functools

import numpy as np
import jax
import jax.numpy as jnp
from jax import lax
from jax.experimental import pallas as pl
from jax.experimental.pallas import tpu as pltpu

F32 = jnp.float32
BF16 = jnp.bfloat16
I32 = jnp.int32

D_MODEL = 1024
DEPTH = 2
GRID_W = 64
HEAD_DIM = 64
BRANCH_W = 512
N_BRANCH = 3
A_HEADS = 8
A_KV_HEADS = 2
WINDOW = 128
CONV_K = 3
NA_HEADS = 8
NA_ROWS_MAX = 8
NA_COLS = 16
ROPE_BASE = 10000.0
FFN_DENSE = 2816
N_EXPERTS = 8
TOP_K = 2
FFN_EXPERT = 3584
RMS_EPS = 1e-6

LANES = 128
MXU_COLS = 256
VMEM_LIMIT_BYTES = 56 * 1024 * 1024

NEG_BIG = -1e30
LOG2E = float(np.log2(np.e))
TOKEN_TILE = 512
WIN_Q_TILE = 512
NA_ROW_GROUP = 4
MOE_BLOCK_ROWS = 512
ISSUE_UNROLL = 8

_QA = (0, 512)
_KVA = (512, 1024)
_CONV = (1024, 2560)
_NA = (2560, 4096)
_GATE = (4096, 7168)
IN_COLS_ARRANGED = 7168


def _cparams(n_grid_dims, vmem=VMEM_LIMIT_BYTES):
    return pltpu.CompilerParams(dimension_semantics=("arbitrary",) * n_grid_dims, vmem_limit_bytes=vmem)


def _dot(a, b):
    return jnp.dot(a, b, preferred_element_type=F32)


def _dot_nt(a, b):
    return lax.dot_general(a, b, (((1,), (1,)), ((), ())), preferred_element_type=F32)


def _rms(x, g):
    return x * lax.rsqrt(jnp.mean(x * x, axis=-1, keepdims=True) + RMS_EPS) * g


def _col_chunks(n, step=512):
    return [(lo, min(lo + step, n)) for lo in range(0, n, step)]


def _ada_kernel(c_ref, w_ref, b_ref, o_ref):
    cv = c_ref[...]
    a = cv * jax.nn.sigmoid(cv)
    o_ref[0] = jnp.dot(a, w_ref[0], preferred_element_type=F32, precision=lax.Precision.HIGHEST) + b_ref[0]


def _ada_mods(cvecs, w_ada, b_ada):
    n_rows = cvecs.shape[0]
    tn = 1536
    out = pl.pallas_call(
        _ada_kernel,
        out_shape=jax.ShapeDtypeStruct((DEPTH, n_rows, 6 * D_MODEL), F32),
        grid=(DEPTH, 6 * D_MODEL // tn),
        in_specs=[
            pl.BlockSpec((n_rows, D_MODEL), lambda l, j: (0, 0)),
            pl.BlockSpec((1, D_MODEL, tn), lambda l, j: (l, 0, j)),
            pl.BlockSpec((1, 1, tn), lambda l, j: (l, 0, j)),
        ],
        out_specs=pl.BlockSpec((1, n_rows, tn), lambda l, j: (l, 0, j)),
        compiler_params=_cparams(2),
        name="ada_mod",
    )(cvecs, w_ada, b_ada.reshape(DEPTH, 1, 6 * D_MODEL))
    return out.reshape(DEPTH, n_rows, 6, D_MODEL)


def _inproj_kernel(x_ref, mod_ref, g_ref, cos_ref, sin_ref, w_ref, qa_ref, kva_ref, conv_ref, na_ref, gate_ref):
    x = x_ref[...]
    h = (_rms(x, g_ref[...]) * (1.0 + mod_ref[0, 1:2, :]) + mod_ref[0, 0:1, :]).astype(BF16)
    cos = cos_ref[...]
    sin = sin_ref[...]
    lane = lax.broadcasted_iota(I32, (1, LANES), 1)
    first_quarter = (lane & 16) == 0

    def mm(lo, hi):
        return _dot(h, w_ref[:, lo:hi])

    def rope(a):
        outs = []
        for j in range(a.shape[1] // LANES):
            c = a[:, j * LANES:(j + 1) * LANES]
            up = pltpu.roll(c, LANES - 16, 1)
            dn = pltpu.roll(c, 16, 1)
            outs.append(c * cos + jnp.where(first_quarter, up, dn) * sin)
        return jnp.concatenate(outs, axis=1)

    qa_ref[...] = rope(mm(*_QA)).astype(BF16)
    kv = mm(*_KVA)
    kva_ref[:, 0:256] = rope(kv[:, 0:256]).astype(BF16)
    kva_ref[:, 256:512] = kv[:, 256:512].astype(BF16)
    for ref, (base, end) in ((conv_ref, _CONV), (na_ref, _NA), (gate_ref, _GATE)):
        for lo, hi in _col_chunks(end - base):
            ref[:, lo:hi] = mm(base + lo, base + hi).astype(BF16)


def _inproj(x2d, mods, g_pre, cos_t, sin_t, w_arr, seq_len, tm):
    n_tok = x2d.shape[0]
    tiles_per_seq = seq_len // tm
    if mods.shape[0] == 1:
        mod_map = lambda i: (0, 0, 0)
    else:
        mod_map = lambda i: (i // tiles_per_seq, 0, 0)
    widths = (512, 512, 1536, 1536, 3072)
    return pl.pallas_call(
        _inproj_kernel,
        out_shape=[jax.ShapeDtypeStruct((n_tok, w), BF16) for w in widths],
        grid=(n_tok // tm,),
        in_specs=[
            pl.BlockSpec((tm, D_MODEL), lambda i: (i, 0)),
            pl.BlockSpec((1, 6, D_MODEL), mod_map),
            pl.BlockSpec((1, D_MODEL), lambda i: (0, 0)),
            pl.BlockSpec((tm, LANES), lambda i: (i % tiles_per_seq, 0)),
            pl.BlockSpec((tm, LANES), lambda i: (i % tiles_per_seq, 0)),
            pl.BlockSpec((D_MODEL, IN_COLS_ARRANGED), lambda i: (0, 0)),
        ],
        out_specs=[pl.BlockSpec((tm, w), lambda i: (i, 0)) for w in widths],
        compiler_params=_cparams(1),
        name="in_proj",
    )(x2d, mods, g_pre, cos_t, sin_t, w_arr)


def _pair_stack(q, even):
    zero = jnp.zeros_like(q)
    return jnp.concatenate([jnp.where(even, q, zero), jnp.where(even, zero, q)], axis=0)


def _softmax_pv(score_parts, value_parts, sink_col):
    m = score_parts[0].max(axis=1, keepdims=True)
    for s in score_parts[1:]:
        m = jnp.maximum(m, s.max(axis=1, keepdims=True))
    if sink_col is not None:
        m = jnp.maximum(m, sink_col)
    den = None
    acc = None
    for s, v in zip(score_parts, value_parts):
        e = jnp.exp2(s - m)
        d = e.sum(axis=1, keepdims=True)
        den = d if den is None else den + d
        r = _dot(e.astype(BF16), v)
        acc = r if acc is None else acc + r
    if sink_col is not None:
        den = den + jnp.exp2(sink_col - m)
    return acc / den


def _attend(s_ref, e_ref, values, sink_col):
    s = s_ref[...]
    m = s.max(axis=1, keepdims=True)
    if sink_col is not None:
        m = jnp.maximum(m, sink_col)
    e_ref[...] = jnp.exp2(s - m).astype(e_ref.dtype)
    r = _dot(e_ref[...], jnp.concatenate([values, jnp.ones_like(values)], axis=1))
    den = r[:, LANES:2 * LANES]
    if sink_col is not None:
        den = den + jnp.exp2(sink_col - m)
    return r[:, 0:LANES] / den


def _win_attn_kernel(sink_ref, q_ref, kvp_ref, kvm_ref, kvn_ref, kvc_ref, o_ref, s_ref, e_ref, *, seq_len, tq):
    blk0 = pl.program_id(1) * (tq // WINDOW)
    nw = 3 * WINDOW
    lane = lax.broadcasted_iota(I32, (1, LANES), 1)
    even = lane < HEAD_DIM
    n_stack = 4 * WINDOW
    rows = lax.broadcasted_iota(I32, (n_stack, 1), 0)
    piece = rows // WINDOW
    qi = rows - piece * WINDOW
    kj = lax.broadcasted_iota(I32, (1, nw), 1)
    rel = kj - WINDOW - qi
    in_window = (rel <= WINDOW) & (rel >= -WINDOW)
    for kh in range(A_KV_HEADS):
        kl = slice(kh * LANES, (kh + 1) * LANES)
        vl = slice((A_KV_HEADS + kh) * LANES, (A_KV_HEADS + kh + 1) * LANES)
        kspan = jnp.concatenate([kvp_ref[:, kl], kvm_ref[:, kl], kvn_ref[:, kl]], axis=0)
        vspan = jnp.concatenate([kvp_ref[:, vl], kvm_ref[:, vl], kvn_ref[:, vl]], axis=0)
        kc = kvc_ref[:, kl]
        vc = kvc_ref[:, vl]
        sink_col = jnp.zeros((n_stack, 1), F32)
        for j in range(4):
            sink_col = jnp.where(piece == j, sink_ref[kh * 4 + j], sink_col)
        sink_col = sink_col * LOG2E
        for n in range(tq // WINDOW):
            qb = q_ref[n * WINDOW:(n + 1) * WINDOW, 2 * kh * LANES:2 * (kh + 1) * LANES]
            lq = jnp.concatenate([_pair_stack(qb[:, 0:LANES], even), _pair_stack(qb[:, LANES:2 * LANES], even)],
                                 axis=0)
            kpos = (blk0 + n - 1) * WINDOW + kj
            valid = in_window & (kpos >= 0) & (kpos < seq_len)
            s_ref[:, 0:nw] = jnp.where(valid, _dot_nt(lq, kspan[n * WINDOW:(n + 3) * WINDOW]), NEG_BIG)
            s_ref[:, nw:] = _dot_nt(lq, kc)
            r = _attend(s_ref, e_ref, jnp.concatenate([vspan[n * WINDOW:(n + 3) * WINDOW], vc], axis=0), sink_col)
            rs = slice(n * WINDOW, (n + 1) * WINDOW)
            o_ref[rs, 2 * kh * LANES:(2 * kh + 1) * LANES] = jnp.where(
                even, r[0:WINDOW], r[WINDOW:2 * WINDOW]).astype(BF16)
            o_ref[rs, (2 * kh + 1) * LANES:(2 * kh + 2) * LANES] = jnp.where(
                even, r[2 * WINDOW:3 * WINDOW], r[3 * WINDOW:4 * WINDOW]).astype(BF16)


def _win_attn(sink, qa, kva, kva_c, batch, seq_len, ctx_len):
    tq = min(WIN_Q_TILE, seq_len)
    nblk = seq_len // WINDOW
    per = tq // WINDOW
    nq = seq_len // tq
    kvw = kva.shape[1]
    n_keys = 3 * WINDOW + ctx_len
    kern = functools.partial(_win_attn_kernel, seq_len=seq_len, tq=tq)
    return pl.pallas_call(
        kern,
        out_shape=jax.ShapeDtypeStruct((batch * seq_len, BRANCH_W), BF16),
        grid_spec=pltpu.PrefetchScalarGridSpec(
            num_scalar_prefetch=1,
            grid=(batch, nq),
            in_specs=[
                pl.BlockSpec((tq, BRANCH_W), lambda b, i, s: (b * nq + i, 0)),
                pl.BlockSpec((WINDOW, kvw), lambda b, i, s: (b * nblk + jnp.maximum(i * per - 1, 0), 0)),
                pl.BlockSpec((tq, kvw), lambda b, i, s: (b * nq + i, 0)),
                pl.BlockSpec((WINDOW, kvw), lambda b, i, s: (b * nblk + jnp.minimum((i + 1) * per, nblk - 1), 0)),
                pl.BlockSpec((ctx_len, kvw), lambda b, i, s: (b, 0)),
            ],
            out_specs=pl.BlockSpec((tq, BRANCH_W), lambda b, i, s: (b * nq + i, 0)),
            scratch_shapes=[pltpu.VMEM((4 * WINDOW, n_keys), F32), pltpu.VMEM((4 * WINDOW, n_keys), BF16)],
        ),
        compiler_params=_cparams(2),
        name="win_attn",
    )(sink, qa, kva, kva, kva, kva_c)


def _ctx_attn_kernel(sink_ref, q_ref, k_ref, v_ref, o_ref, *, use_sink):
    p = pl.program_id(1)
    n = q_ref.shape[0]
    lane = lax.broadcasted_iota(I32, (1, LANES), 1)
    even = lane < HEAD_DIM
    lq = _pair_stack(q_ref[...], even)
    s = _dot_nt(lq, k_ref[...])
    sink_col = None
    if use_sink:
        rows = lax.broadcasted_iota(I32, (2 * n, 1), 0)
        sink_col = jnp.where(rows < n, sink_ref[2 * p], sink_ref[2 * p + 1]) * LOG2E
    r = _softmax_pv([s], [v_ref[...]], sink_col)
    o_ref[...] = jnp.where(even, r[0:n], r[n:2 * n]).astype(BF16)


def _ctx_attn(sink, q_arr, q_lane0, kv_arr, k_lane0, v_lane0, kv_shared, batch, ctx_len, use_sink):
    def kv_map(lane0):
        if kv_shared:
            return lambda b, p, s: (b, lane0 + p // 2)
        return lambda b, p, s: (b, lane0 + p)

    kern = functools.partial(_ctx_attn_kernel, use_sink=use_sink)
    return pl.pallas_call(
        kern,
        out_shape=jax.ShapeDtypeStruct((batch * ctx_len, BRANCH_W), BF16),
        grid_spec=pltpu.PrefetchScalarGridSpec(
            num_scalar_prefetch=1,
            grid=(batch, BRANCH_W // LANES),
            in_specs=[
                pl.BlockSpec((ctx_len, LANES), lambda b, p, s: (b, q_lane0 + p)),
                pl.BlockSpec((ctx_len, LANES), kv_map(k_lane0)),
                pl.BlockSpec((ctx_len, LANES), kv_map(v_lane0)),
            ],
            out_specs=pl.BlockSpec((ctx_len, LANES), lambda b, p, s: (b, p)),
        ),
        compiler_params=_cparams(2),
        name="ctx_attn_sink" if use_sink else "ctx_attn",
    )(sink, q_arr, kv_arr, kv_arr)


def _na_kernel(q_ref, kp_ref, km_ref, kn_ref, vp_ref, vm_ref, vn_ref, kc_ref, vc_ref, b0_ref, b1_ref, o_ref,
               s_ref, e_ref):
    gq = NA_ROW_GROUP * GRID_W
    nb = 3 * gq
    lane = lax.broadcasted_iota(I32, (1, LANES), 1)
    even = lane < HEAD_DIM
    for p in range(NA_HEADS // 2):
        pl_ = slice(p * LANES, (p + 1) * LANES)
        kc = kc_ref[:, pl_]
        vc = vc_ref[:, pl_]
        spans = (
            (jnp.concatenate([kp_ref[gq:2 * gq, pl_], km_ref[:, pl_]], axis=0),
             jnp.concatenate([vp_ref[gq:2 * gq, pl_], vm_ref[:, pl_]], axis=0)),
            (jnp.concatenate([km_ref[:, pl_], kn_ref[0:gq, pl_]], axis=0),
             jnp.concatenate([vm_ref[:, pl_], vn_ref[0:gq, pl_]], axis=0)),
        )
        for g, b_ref in enumerate((b0_ref, b1_ref)):
            kspan, vspan = spans[g]
            lq = _pair_stack(q_ref[g * gq:(g + 1) * gq, pl_], even)
            bias = jnp.concatenate([b_ref[0, 2 * p], b_ref[0, 2 * p + 1]], axis=0)
            s_ref[:, 0:nb] = _dot_nt(lq, kspan) + bias
            s_ref[:, nb:] = _dot_nt(lq, kc)
            r = _attend(s_ref, e_ref, jnp.concatenate([vspan, vc], axis=0), None)
            o_ref[g * gq:(g + 1) * gq, pl_] = jnp.where(even, r[0:gq], r[gq:2 * gq]).astype(BF16)


_N_DR = 2 * NA_ROWS_MAX - 1
_N_DC = 2 * NA_COLS - 1


def _na_bias_kernel(rpb_ref, o_ref):
    h = pl.program_id(0)
    qc = lax.broadcasted_iota(I32, (GRID_W, LANES), 0)
    lane = lax.broadcasted_iota(I32, (GRID_W, LANES), 1)
    kc = lane & (GRID_W - 1)
    dc = jnp.clip(kc - qc, -(NA_COLS - 1), NA_COLS - 1) + NA_COLS - 1
    col_start = jnp.clip(qc - NA_COLS // 2, 0, GRID_W - NA_COLS)
    col_ok = (kc >= col_start) & (kc < col_start + NA_COLS)
    neg = jnp.full((GRID_W, LANES), NEG_BIG, F32)
    col_bias = []
    for dr in range(_N_DR):
        c = neg
        for d in range(_N_DC):
            c = jnp.where(dc == d, rpb_ref[(h * _N_DR + dr) * _N_DC + d], c)
        col_bias.append(jnp.where(col_ok, c * LOG2E, neg))
    row_ok = (
        lambda i, j: 4 <= j < 4 + NA_ROWS_MAX,
        lambda i, j: i <= j < i + NA_ROWS_MAX,
        lambda i, j: 0 <= j < NA_ROWS_MAX,
    )
    for v in range(3):
        for i in range(NA_ROW_GROUP):
            for m in range(3 * NA_ROW_GROUP // 2):
                halves = [col_bias[j - i + 3] if row_ok[v](i, j) else neg for j in (2 * m, 2 * m + 1)]
                o_ref[v, 0, i * GRID_W:(i + 1) * GRID_W, m * LANES:(m + 1) * LANES] = jnp.where(
                    lane < GRID_W, halves[0], halves[1])


def _na_bias_tables(rpb):
    shape = (3, NA_HEADS, NA_ROW_GROUP * GRID_W, 3 * NA_ROW_GROUP * GRID_W)
    return pl.pallas_call(
        _na_bias_kernel,
        out_shape=jax.ShapeDtypeStruct(shape, F32),
        grid_spec=pltpu.PrefetchScalarGridSpec(
            num_scalar_prefetch=1,
            grid=(NA_HEADS,),
            in_specs=[],
            out_specs=pl.BlockSpec((3, 1) + shape[2:], lambda h, r: (0, h, 0, 0)),
        ),
        compiler_params=_cparams(1),
        name="na_bias",
    )(rpb.reshape(-1))


def _na_attn(na, na_c, bias_tab, batch, seq_len, ctx_len):
    rows = seq_len // GRID_W
    assert rows % (2 * NA_ROW_GROUP) == 0 and rows >= 4 * NA_ROW_GROUP
    tq = 2 * NA_ROW_GROUP * GRID_W
    nq = seq_len // tq
    gq = NA_ROW_GROUP * GRID_W
    n_keys = 3 * gq + ctx_len

    def tok_map(part, shift):
        return lambda b, i: (b * nq + jnp.clip(i + shift, 0, nq - 1), part)

    bias_block = (1, NA_HEADS, gq, 3 * gq)
    return pl.pallas_call(
        _na_kernel,
        out_shape=jax.ShapeDtypeStruct((batch * seq_len, BRANCH_W), BF16),
        grid=(batch, nq),
        in_specs=[
            pl.BlockSpec((tq, BRANCH_W), tok_map(0, 0)),
            pl.BlockSpec((tq, BRANCH_W), tok_map(1, -1)),
            pl.BlockSpec((tq, BRANCH_W), tok_map(1, 0)),
            pl.BlockSpec((tq, BRANCH_W), tok_map(1, 1)),
            pl.BlockSpec((tq, BRANCH_W), tok_map(2, -1)),
            pl.BlockSpec((tq, BRANCH_W), tok_map(2, 0)),
            pl.BlockSpec((tq, BRANCH_W), tok_map(2, 1)),
            pl.BlockSpec((ctx_len, BRANCH_W), lambda b, i: (b, 1)),
            pl.BlockSpec((ctx_len, BRANCH_W), lambda b, i: (b, 2)),
            pl.BlockSpec(bias_block, lambda b, i: (jnp.where(i == 0, 0, 1), 0, 0, 0), pipeline_mode=pl.Buffered(1)),
            pl.BlockSpec(bias_block, lambda b, i: (jnp.where(i == nq - 1, 2, 1), 0, 0, 0),
                         pipeline_mode=pl.Buffered(1)),
        ],
        out_specs=pl.BlockSpec((tq, BRANCH_W), lambda b, i: (b * nq + i, 0)),
        scratch_shapes=[pltpu.VMEM((2 * gq, n_keys), F32), pltpu.VMEM((2 * gq, n_keys), BF16)],
        compiler_params=_cparams(2),
        name="na_attn",
    )(na, na, na, na, na, na, na, na_c, na_c, bias_tab, bias_tab)


def _merge_kernel(oa_ref, conv_ref, cprev_ref, cnext_ref, oc_ref, gate_ref, x_ref, mod_ref, wb_ref, bg_ref, cw_ref,
                  wo_ref, gpost_ref, out_ref, y_ref, *, tiles_per_seq):
    tm = x_ref.shape[0]
    ti = pl.program_id(0) % tiles_per_seq
    ux = conv_ref[:, 0:512].astype(F32)
    gb = conv_ref[:, 512:1024].astype(F32)
    gc = conv_ref[:, 1024:1536].astype(F32)
    u = gc * ux
    cprev = cprev_ref[...].astype(F32)
    cnext = cnext_ref[...].astype(F32)
    halo = cprev.shape[0]
    u_before = cprev[halo - 1:halo, 1024:1536] * cprev[halo - 1:halo, 0:512]
    u_after = cnext[0:1, 1024:1536] * cnext[0:1, 0:512]
    u_before = jnp.where(ti == 0, 0.0, u_before)
    u_after = jnp.where(ti == tiles_per_seq - 1, 0.0, u_after)
    row = lax.broadcasted_iota(I32, (tm, 1), 0)
    u_dn = jnp.where(row == 0, u_before, pltpu.roll(u, 1, 0))
    u_up = jnp.where(row == tm - 1, u_after, pltpu.roll(u, tm - 1, 0))
    cw = cw_ref[...]
    ob = (gb * (cw[0:1] * u_dn + cw[1:2] * u + cw[2:3] * u_up)).astype(BF16)
    oa = oa_ref[...]
    oc = oc_ref[...]
    for lo, hi in _col_chunks(D_MODEL):
        acc = None
        for r, o in enumerate((oa, ob, oc)):
            gate = jax.nn.sigmoid(gate_ref[:, r * D_MODEL + lo:r * D_MODEL + hi].astype(F32)
                                  + bg_ref[:, r * D_MODEL + lo:r * D_MODEL + hi])
            term = gate * _dot(o, wb_ref[r, :, lo:hi])
            acc = term if acc is None else acc + term
        y_ref[:, lo:hi] = acc.astype(BF16)
    y2 = _dot(y_ref[...], wo_ref[...])
    out_ref[...] = x_ref[...] + mod_ref[0, 2:3, :] * _rms(y2, gpost_ref[...])


def _merge(oa, conv, oc, gate, x2d, mods, wb, b_gate, conv_w, wo, g_post, seq_len, tm):
    n_tok = x2d.shape[0]
    tiles_per_seq = seq_len // tm
    halo = 16
    hb = tm // halo
    n_halo = n_tok // halo
    if mods.shape[0] == 1:
        mod_map = lambda i: (0, 0, 0)
    else:
        mod_map = lambda i: (i // tiles_per_seq, 0, 0)
    kern = functools.partial(_merge_kernel, tiles_per_seq=tiles_per_seq)
    const2 = lambda i: (0, 0)
    return pl.pallas_call(
        kern,
        out_shape=jax.ShapeDtypeStruct((n_tok, D_MODEL), F32),
        grid=(n_tok // tm,),
        in_specs=[
            pl.BlockSpec((tm, BRANCH_W), lambda i: (i, 0)),
            pl.BlockSpec((tm, 3 * BRANCH_W), lambda i: (i, 0)),
            pl.BlockSpec((halo, 3 * BRANCH_W), lambda i: (jnp.maximum(i * hb - 1, 0), 0)),
            pl.BlockSpec((halo, 3 * BRANCH_W), lambda i: (jnp.minimum((i + 1) * hb, n_halo - 1), 0)),
            pl.BlockSpec((tm, BRANCH_W), lambda i: (i, 0)),
            pl.BlockSpec((tm, N_BRANCH * D_MODEL), lambda i: (i, 0)),
            pl.BlockSpec((tm, D_MODEL), lambda i: (i, 0)),
            pl.BlockSpec((1, 6, D_MODEL), mod_map),
            pl.BlockSpec((N_BRANCH, BRANCH_W, D_MODEL), lambda i: (0, 0, 0)),
            pl.BlockSpec((1, N_BRANCH * D_MODEL), const2),
            pl.BlockSpec((CONV_K, BRANCH_W), const2),
            pl.BlockSpec((D_MODEL, D_MODEL), const2),
            pl.BlockSpec((1, D_MODEL), const2),
        ],
        out_specs=pl.BlockSpec((tm, D_MODEL), lambda i: (i, 0)),
        scratch_shapes=[pltpu.VMEM((tm, D_MODEL), BF16)],
        compiler_params=_cparams(1),
        name="branch_merge",
    )(oa, conv, conv, conv, oc, gate, x2d, mods, wb, b_gate, conv_w, wo, g_post)


def _ffn_kernel(x_ref, mod_ref, gpre_ref, wgu_ref, wdn_ref, gpost_ref, out_ref, act_ref):
    x = x_ref[...]
    h = (_rms(x, gpre_ref[...]) * (1.0 + mod_ref[0, 4:5, :]) + mod_ref[0, 3:4, :]).astype(BF16)
    for lo, hi in _col_chunks(FFN_DENSE):
        g = _dot(h, wgu_ref[:, lo:hi])
        u = _dot(h, wgu_ref[:, FFN_DENSE + lo:FFN_DENSE + hi])
        act_ref[:, lo:hi] = (g * jax.nn.sigmoid(g) * u).astype(BF16)
    y = _dot(act_ref[...], wdn_ref[...])
    out_ref[...] = x + mod_ref[0, 5:6, :] * _rms(y, gpost_ref[...])


def _ffn_dense(x2d, mods, g_pre, wgu, wdn, g_post, seq_len, tm):
    n_tok = x2d.shape[0]
    tiles_per_seq = seq_len // tm
    if mods.shape[0] == 1:
        mod_map = lambda i: (0, 0, 0)
    else:
        mod_map = lambda i: (i // tiles_per_seq, 0, 0)
    const2 = lambda i: (0, 0)
    return pl.pallas_call(
        _ffn_kernel,
        out_shape=jax.ShapeDtypeStruct((n_tok, D_MODEL), F32),
        grid=(n_tok // tm,),
        in_specs=[
            pl.BlockSpec((tm, D_MODEL), lambda i: (i, 0)),
            pl.BlockSpec((1, 6, D_MODEL), mod_map),
            pl.BlockSpec((1, D_MODEL), const2),
            pl.BlockSpec((D_MODEL, 2 * FFN_DENSE), const2),
            pl.BlockSpec((FFN_DENSE, D_MODEL), const2),
            pl.BlockSpec((1, D_MODEL), const2),
        ],
        out_specs=pl.BlockSpec((tm, D_MODEL), lambda i: (i, 0)),
        scratch_shapes=[pltpu.VMEM((tm, FFN_DENSE), BF16)],
        compiler_params=_cparams(1),
        name="ffn_dense",
    )(x2d, mods, g_pre, wgu, wdn, g_post)


def _router_kernel(x_ref, mod_ref, gpre_ref, wr_ref, br_ref, h_ref, e_ref, gt_ref):
    x = x_ref[...]
    h = _rms(x, gpre_ref[...]) * (1.0 + mod_ref[0, 4:5, :]) + mod_ref[0, 3:4, :]
    h_ref[...] = h
    logits = lax.dot_general(wr_ref[...], h, (((1,), (1,)), ((), ())), preferred_element_type=F32,
                             precision=lax.Precision.HIGHEST) + br_ref[...]
    eid = lax.broadcasted_iota(I32, logits.shape, 0)
    m1 = logits.max(axis=0, keepdims=True)
    i1 = jnp.min(jnp.where(logits == m1, eid, N_EXPERTS), axis=0, keepdims=True)
    rest = jnp.where(eid == i1, -jnp.inf, logits)
    m2 = rest.max(axis=0, keepdims=True)
    i2 = jnp.min(jnp.where(rest == m2, eid, N_EXPERTS), axis=0, keepdims=True)
    e2 = jnp.exp(m2 - m1)
    den = 1.0 + e2
    e_ref[...] = jnp.concatenate([i1, i2], axis=0)
    gt_ref[...] = jnp.concatenate([1.0 / den, e2 / den], axis=0)


def _router(x2d, mods, g_pre, w_router_t, b_router, seq_len, tm):
    n_tok = x2d.shape[0]
    tiles_per_seq = seq_len // tm
    const2 = lambda i: (0, 0)
    return pl.pallas_call(
        _router_kernel,
        out_shape=[
            jax.ShapeDtypeStruct((n_tok, D_MODEL), F32),
            jax.ShapeDtypeStruct((TOP_K, n_tok), I32),
            jax.ShapeDtypeStruct((TOP_K, n_tok), F32),
        ],
        grid=(n_tok // tm,),
        in_specs=[
            pl.BlockSpec((tm, D_MODEL), lambda i: (i, 0)),
            pl.BlockSpec((1, 6, D_MODEL), lambda i: (i // tiles_per_seq, 0, 0)),
            pl.BlockSpec((1, D_MODEL), const2),
            pl.BlockSpec((N_EXPERTS, D_MODEL), const2),
            pl.BlockSpec((N_EXPERTS, 1), const2),
        ],
        out_specs=[
            pl.BlockSpec((tm, D_MODEL), lambda i: (i, 0)),
            pl.BlockSpec((TOP_K, tm), lambda i: (0, i)),
            pl.BlockSpec((TOP_K, tm), lambda i: (0, i)),
        ],
        compiler_params=_cparams(1),
        name="moe_router",
    )(x2d, mods, g_pre, w_router_t, b_router)


def _rank_kernel(e_ref, rank_ref, cnt_ref, carry_ref):
    tm = e_ref.shape[1]

    @pl.when(pl.program_id(0) == 0)
    def _():
        carry_ref[...] = jnp.zeros_like(carry_ref)

    e = e_ref[...]
    eid = lax.broadcasted_iota(I32, (N_EXPERTS, tm), 0)
    oh0 = eid == e[0:1, :]
    oh1 = eid == e[1:2, :]
    oh = jnp.concatenate([oh0, oh1], axis=0).astype(F32)
    tri = (lax.broadcasted_iota(I32, (tm, tm), 0) <= lax.broadcasted_iota(I32, (tm, tm), 1)).astype(BF16)
    incl = _dot(oh.astype(BF16), tri)
    excl = incl - oh
    tot = incl[:, tm - 1:tm]
    carry = carry_ref[:, 0:1]
    rank0 = carry + excl[0:N_EXPERTS]
    rank1 = carry + tot[0:N_EXPERTS] + excl[N_EXPERTS:2 * N_EXPERTS]
    r0 = jnp.sum(jnp.where(oh0, rank0, 0.0), axis=0, keepdims=True)
    r1 = jnp.sum(jnp.where(oh1, rank1, 0.0), axis=0, keepdims=True)
    rank_ref[...] = jnp.concatenate([r0, r1], axis=0).astype(I32)
    new_carry = carry + tot[0:N_EXPERTS] + tot[N_EXPERTS:2 * N_EXPERTS]
    carry_ref[...] = jnp.broadcast_to(new_carry, carry_ref.shape)
    cnt_ref[...] = jnp.broadcast_to(new_carry, cnt_ref.shape).astype(I32)


def _ranks(e_idx, tm):
    n_tok = e_idx.shape[1]
    return pl.pallas_call(
        _rank_kernel,
        out_shape=[
            jax.ShapeDtypeStruct((TOP_K, n_tok), I32),
            jax.ShapeDtypeStruct((N_EXPERTS, LANES), I32),
        ],
        grid=(n_tok // tm,),
        in_specs=[pl.BlockSpec((TOP_K, tm), lambda i: (0, i))],
        out_specs=[
            pl.BlockSpec((TOP_K, tm), lambda i: (0, i)),
            pl.BlockSpec((N_EXPERTS, LANES), lambda i: (0, 0)),
        ],
        scratch_shapes=[pltpu.VMEM((N_EXPERTS, LANES), F32)],
        compiler_params=_cparams(1),
        name="moe_rank",
    )(e_idx)


def _dispatch_kernel(dest_ref, pad_ref, h_ref, xs_ref, zero_ref, sem, zsem):
    tm = h_ref.shape[0]
    n_tok = dest_ref.shape[0] // TOP_K
    base = pl.program_id(0) * tm

    @pl.when(pl.program_id(0) == 0)
    def _():
        zero_ref[...] = jnp.zeros_like(zero_ref)

        zrows = zero_ref.shape[0]

        def zero_copy(r):
            return pltpu.make_async_copy(zero_ref.at[pl.ds(0, 1)], xs_ref.at[pl.ds(r, 1)], zsem)

        def zero_chunk(j):
            return pltpu.make_async_copy(zero_ref, xs_ref.at[pl.ds(pl.multiple_of(j * zrows, zrows), zrows)], zsem)

        tail = (pad_ref[2 * N_EXPERTS] // zrows, xs_ref.shape[0] // zrows)
        for e in range(N_EXPERTS):
            lo, hi = pad_ref[e], pad_ref[N_EXPERTS + e]
            lax.fori_loop(lo, hi, lambda r, c: (zero_copy(r).start(), c)[1], 0)
        lax.fori_loop(tail[0], tail[1], lambda j, c: (zero_chunk(j).start(), c)[1], 0)
        for e in range(N_EXPERTS):
            lo, hi = pad_ref[e], pad_ref[N_EXPERTS + e]
            lax.fori_loop(lo, hi, lambda r, c: (zero_copy(r).wait(), c)[1], 0)
        lax.fori_loop(tail[0], tail[1], lambda j, c: (zero_chunk(j).wait(), c)[1], 0)

    def start(t, carry):
        for k in range(TOP_K):
            d = dest_ref[k * n_tok + base + t]
            pltpu.make_async_copy(h_ref.at[pl.ds(t, 1)], xs_ref.at[pl.ds(d, 1)], sem).start(priority=k)
        return carry

    lax.fori_loop(0, tm, start, 0, unroll=ISSUE_UNROLL)
    for k in range(TOP_K):
        pltpu.make_async_copy(h_ref, xs_ref.at[pl.ds(0, tm)], sem).wait()


def _dispatch(dest_flat, pad_rows, h, n_rows, tm):
    n_tok = h.shape[0]
    return pl.pallas_call(
        _dispatch_kernel,
        out_shape=jax.ShapeDtypeStruct((n_rows, D_MODEL), F32),
        grid_spec=pltpu.PrefetchScalarGridSpec(
            num_scalar_prefetch=2,
            grid=(n_tok // tm,),
            in_specs=[pl.BlockSpec((tm, D_MODEL), lambda i, d, p: (i, 0))],
            out_specs=pl.BlockSpec(memory_space=pl.ANY),
            scratch_shapes=[
                pltpu.VMEM((64, D_MODEL), F32),
                pltpu.SemaphoreType.DMA,
                pltpu.SemaphoreType.DMA,
            ],
        ),
        compiler_params=_cparams(1),
        name="moe_dispatch",
    )(dest_flat, pad_rows, h)


def _expert_kernel(be_ref, nb_ref, xs_ref, wgu_ref, wdn_ref, y_ref, act_ref):
    @pl.when(pl.program_id(0) < nb_ref[0])
    def _():
        xb = xs_ref[...].astype(BF16)
        for lo, hi in _col_chunks(FFN_EXPERT):
            g = _dot(xb, wgu_ref[0, :, lo:hi])
            u = _dot(xb, wgu_ref[0, :, FFN_EXPERT + lo:FFN_EXPERT + hi])
            act_ref[:, lo:hi] = (g * jax.nn.sigmoid(g) * u).astype(BF16)
        y_ref[...] = _dot(act_ref[...], wdn_ref[0])

    @pl.when(pl.program_id(0) >= nb_ref[0])
    def _():
        y_ref[...] = jnp.zeros_like(y_ref)


def _expert_ffn(block_e, n_used, xs, wgu, wdn, bm):
    n_rows = xs.shape[0]
    n_blocks = n_rows // bm

    def blk(j, be, nb):
        return jnp.minimum(j, nb[0] - 1)

    return pl.pallas_call(
        _expert_kernel,
        out_shape=jax.ShapeDtypeStruct((n_rows, D_MODEL), F32),
        grid_spec=pltpu.PrefetchScalarGridSpec(
            num_scalar_prefetch=2,
            grid=(n_blocks,),
            in_specs=[
                pl.BlockSpec((bm, D_MODEL), lambda j, be, nb: (blk(j, be, nb), 0)),
                pl.BlockSpec((1, D_MODEL, 2 * FFN_EXPERT), lambda j, be, nb: (be[blk(j, be, nb)], 0, 0),
                             pipeline_mode=pl.Buffered(1)),
                pl.BlockSpec((1, FFN_EXPERT, D_MODEL), lambda j, be, nb: (be[blk(j, be, nb)], 0, 0),
                             pipeline_mode=pl.Buffered(1)),
            ],
            out_specs=pl.BlockSpec((bm, D_MODEL), lambda j, be, nb: (j, 0)),
            scratch_shapes=[pltpu.VMEM((bm, FFN_EXPERT), BF16)],
        ),
        compiler_params=_cparams(1),
        name="moe_experts",
    )(block_e, n_used, xs, wgu, wdn)


def _combine_kernel(dest_ref, ys_ref, gt_ref, x_ref, mod_ref, gpost_ref, out_ref, y0_ref, y1_ref, sem):
    tm = x_ref.shape[0]
    n_tok = dest_ref.shape[0] // TOP_K
    base = pl.program_id(0) * tm
    bufs = (y0_ref, y1_ref)

    def start(t, carry):
        for k in range(TOP_K):
            d = dest_ref[k * n_tok + base + t]
            pltpu.make_async_copy(ys_ref.at[pl.ds(d, 1)], bufs[k].at[pl.ds(t, 1)], sem).start(priority=k)
        return carry

    lax.fori_loop(0, tm, start, 0, unroll=ISSUE_UNROLL)
    for k in range(TOP_K):
        pltpu.make_async_copy(ys_ref.at[pl.ds(0, tm)], bufs[k], sem).wait()
    gt = gt_ref[...]
    y = gt[:, 0:1] * y0_ref[...] + gt[:, 1:2] * y1_ref[...]
    out_ref[...] = x_ref[...] + mod_ref[0, 5:6, :] * _rms(y, gpost_ref[...])


def _combine(dest_flat, ys, gates_t, x2d, mods, g_post, seq_len, tm):
    n_tok = x2d.shape[0]
    tiles_per_seq = seq_len // tm
    return pl.pallas_call(
        _combine_kernel,
        out_shape=jax.ShapeDtypeStruct((n_tok, D_MODEL), F32),
        grid_spec=pltpu.PrefetchScalarGridSpec(
            num_scalar_prefetch=1,
            grid=(n_tok // tm,),
            in_specs=[
                pl.BlockSpec(memory_space=pl.ANY),
                pl.BlockSpec((tm, TOP_K), lambda i, d: (i, 0)),
                pl.BlockSpec((tm, D_MODEL), lambda i, d: (i, 0)),
                pl.BlockSpec((1, 6, D_MODEL), lambda i, d: (i // tiles_per_seq, 0, 0)),
                pl.BlockSpec((1, D_MODEL), lambda i, d: (0, 0)),
            ],
            out_specs=pl.BlockSpec((tm, D_MODEL), lambda i, d: (i, 0)),
            scratch_shapes=[
                pltpu.VMEM((tm, D_MODEL), F32),
                pltpu.VMEM((tm, D_MODEL), F32),
                pltpu.SemaphoreType.DMA,
            ],
        ),
        compiler_params=_cparams(1),
        name="moe_combine",
    )(dest_flat, ys, gates_t, x2d, mods, g_post)


def _moe(x2d, mods, g_pre, g_post, w_router, b_router, wgu, wdn, seq_len, tm):
    n_tok = x2d.shape[0]
    bm = MOE_BLOCK_ROWS
    h, e_idx, gates = _router(x2d, mods, g_pre, w_router.T, b_router.reshape(N_EXPERTS, 1), seq_len, tm)
    rank, cnt = _ranks(e_idx, tm)
    counts = cnt[:, 0]
    padded = (counts + bm - 1) // bm * bm
    ends = jnp.cumsum(padded)
    pstart = ends - padded
    dest = rank
    for e in range(N_EXPERTS):
        dest = dest + jnp.where(e_idx == e, pstart[e], 0)
    dest_flat = dest.reshape(-1).astype(I32)
    n_blocks = n_tok * TOP_K // bm + N_EXPERTS
    blk_start = jnp.arange(n_blocks, dtype=I32) * bm
    block_e = jnp.minimum(jnp.sum(blk_start[:, None] >= ends[None, :], axis=1), N_EXPERTS - 1).astype(I32)
    n_used = (ends[-1:] // bm).astype(I32)
    pad_rows = jnp.concatenate([pstart + counts, ends, ends[-1:]]).astype(I32)
    xs = _dispatch(dest_flat, pad_rows, h, n_blocks * bm, tm)
    ys = _expert_ffn(block_e, n_used, xs, wgu, wdn, bm)
    return _combine(dest_flat, ys, gates.T, x2d, mods, g_post, seq_len, tm)


def _rope_tables(n_tokens):
    pos = jnp.arange(n_tokens)
    row = (pos // GRID_W).astype(F32)
    col = (pos % GRID_W).astype(F32)
    half = HEAD_DIM // 2
    inv = 1.0 / (ROPE_BASE ** (jnp.arange(0, half, 2, dtype=F32) / half))
    ar = row[:, None] * inv[None]
    ac = col[:, None] * inv[None]
    ang = jnp.concatenate([ar, ar, ac, ac], axis=-1)
    ang = jnp.concatenate([ang, ang], axis=-1)
    sign = jnp.where((jnp.arange(LANES) & 16) == 0, -1.0, 1.0).astype(F32)
    return jnp.cos(ang), jnp.sin(ang) * sign[None]


def _arrange_w_in(w):
    scale = HEAD_DIM ** -0.5 * LOG2E
    offs = np.cumsum([0, BRANCH_W, 128, 128, BRANCH_W, BRANCH_W, BRANCH_W, BRANCH_W, BRANCH_W, BRANCH_W, N_BRANCH * D_MODEL])
    parts = [w[:, offs[i]:offs[i + 1]] for i in range(10)]
    qa, ka, va, ux, gb, gc, qn, kn, vn, gx = parts

    def dup(t):
        return jnp.concatenate([t[:, 0:64], t[:, 0:64], t[:, 64:128], t[:, 64:128]], axis=1)

    return jnp.concatenate([qa * scale, dup(ka), dup(va), ux, gb, gc, qn * scale, kn, vn, gx], axis=1).astype(BF16)


def kernel(x, c, ctx, c_ctx, w_ada, b_ada, g_pre_mix, g_post_mix, g_pre_ffn, g_post_ffn, w_in, b_gate, sink, conv_w,
           rpb, w_branch, w_out, w_gu_dense, w_dn_dense, w_router, b_router, w_gu_moe, w_dn_moe):
    batch, seq_len, d = x.shape
    ctx_len = ctx.shape[1]
    tm = min(TOKEN_TILE, seq_len)
    tmc = min(TOKEN_TILE, ctx_len)

    n_vec = batch + 1
    n_vec_pad = -(-n_vec // 8) * 8
    cvecs = jnp.concatenate([c, c_ctx[None], jnp.zeros((n_vec_pad - n_vec, d), F32)], axis=0)
    mods = _ada_mods(cvecs, w_ada, b_ada)

    cos_t, sin_t = _rope_tables(seq_len)
    cos_c = jnp.ones((ctx_len, LANES), F32)
    sin_c = jnp.zeros((ctx_len, LANES), F32)

    x2 = x.reshape(batch * seq_len, d)
    hc2 = ctx.reshape(batch * ctx_len, d)
    for l in range(DEPTH):
        update_ctx = l < DEPTH - 1
        mods_x = mods[l, :batch]
        mods_c = mods[l, batch:batch + 1]
        w_arr = _arrange_w_in(w_in[l])
        wb = w_branch[l].astype(BF16)
        wo = w_out[l].astype(BF16)
        bg = b_gate[l].reshape(1, N_BRANCH * D_MODEL)
        g1 = g_pre_mix[l].reshape(1, d)
        g2 = g_post_mix[l].reshape(1, d)
        g3 = g_pre_ffn[l].reshape(1, d)
        g4 = g_post_ffn[l].reshape(1, d)

        qa, kva, conv, na, gate = _inproj(x2, mods_x, g1, cos_t, sin_t, w_arr, seq_len, tm)
        qa_c, kva_c, conv_c, na_c, gate_c = _inproj(hc2, mods_c, g1, cos_c, sin_c, w_arr, ctx_len, tmc)

        o_a = _win_attn(sink[l], qa, kva, kva_c, batch, seq_len, ctx_len)
        o_c = _na_attn(na, na_c, _na_bias_tables(rpb[l]), batch, seq_len, ctx_len)
        x2 = _merge(o_a, conv, o_c, gate, x2, mods_x, wb, bg, conv_w[l], wo, g2, seq_len, tm)

        if l % 2 == 0:
            wgu = w_gu_dense[l // 2].astype(BF16)
            wdn = w_dn_dense[l // 2].astype(BF16)
            x2 = _ffn_dense(x2, mods_x, g3, wgu, wdn, g4, seq_len, tm)
        else:
            x2 = _moe(x2, mods_x, g3, g4, w_router[l // 2], b_router[l // 2], w_gu_moe[l // 2].astype(BF16),
                      w_dn_moe[l // 2].astype(BF16), seq_len, tm)

        if update_ctx:
            o_a_c = _ctx_attn(sink[l], qa_c, 0, kva_c, 0, 2, True, batch, ctx_len, True)
            o_c_c = _ctx_attn(sink[l], na_c, 0, na_c, 4, 8, False, batch, ctx_len, False)
            hc2 = _merge(o_a_c, conv_c, o_c_c, gate_c, hc2, mods_c, wb, bg, conv_w[l], wo, g2, ctx_len, tmc)
            if l % 2 == 0:
                hc2 = _ffn_dense(hc2, mods_c, g3, wgu, wdn, g4, ctx_len, tmc)
            else:
                hc2 = _moe(hc2, jnp.broadcast_to(mods_c, (batch,) + mods_c.shape[1:]), g3, g4, w_router[l // 2],
                           b_router[l // 2], w_gu_moe[l // 2].astype(BF16), w_dn_moe[l // 2].astype(BF16),
                           ctx_len, tmc)
    return x2.reshape(batch, seq_len, d)
```

```python
import functools

import numpy as np
import jax
import jax.numpy as jnp
from jax import lax
from jax.experimental import pallas as pl
from jax.experimental.pallas import tpu as pltpu

F32 = jnp.float32
BF16 = jnp.bfloat16
I32 = jnp.int32

D_MODEL = 1024
DEPTH = 2
GRID_W = 64
HEAD_DIM = 64
BRANCH_W = 512
N_BRANCH = 3
A_HEADS = 8
A_KV_HEADS = 2
WINDOW = 128
CONV_K = 3
NA_HEADS = 8
NA_ROWS_MAX = 8
NA_COLS = 16
ROPE_BASE = 10000.0
FFN_DENSE = 2816
N_EXPERTS = 8
TOP_K = 2
FFN_EXPERT = 3584
RMS_EPS = 1e-6

LANES = 128
MXU_COLS = 256
VMEM_LIMIT_BYTES = 56 * 1024 * 1024

NEG_BIG = -1e30
LOG2E = float(np.log2(np.e))
TOKEN_TILE = 512
WIN_Q_TILE = 512
NA_ROW_GROUP = 4
MOE_BLOCK_ROWS = 512
ISSUE_UNROLL = 8

_QA = (0, 512)
_KVA = (512, 1024)
_CONV = (1024, 2560)
_NA = (2560, 4096)
_GATE = (4096, 7168)
IN_COLS_ARRANGED = 7168


def _cparams(n_grid_dims, vmem=VMEM_LIMIT_BYTES):
    return pltpu.CompilerParams(dimension_semantics=("arbitrary",) * n_grid_dims, vmem_limit_bytes=vmem)


def _dot(a, b):
    return jnp.dot(a, b, preferred_element_type=F32)


def _dot_nt(a, b):
    return lax.dot_general(a, b, (((1,), (1,)), ((), ())), preferred_element_type=F32)


def _rms(x, g):
    return x * lax.rsqrt(jnp.mean(x * x, axis=-1, keepdims=True) + RMS_EPS) * g


def _col_chunks(n, step=512):
    return [(lo, min(lo + step, n)) for lo in range(0, n, step)]


def _ada_kernel(c_ref, w_ref, b_ref, o_ref):
    cv = c_ref[...]
    a = cv * jax.nn.sigmoid(cv)
    o_ref[0] = jnp.dot(a, w_ref[0], preferred_element_type=F32, precision=lax.Precision.HIGHEST) + b_ref[0]


def _ada_mods(cvecs, w_ada, b_ada):
    n_rows = cvecs.shape[0]
    tn = 1536
    out = pl.pallas_call(
        _ada_kernel,
        out_shape=jax.ShapeDtypeStruct((DEPTH, n_rows, 6 * D_MODEL), F32),
        grid=(DEPTH, 6 * D_MODEL // tn),
        in_specs=[
            pl.BlockSpec((n_rows, D_MODEL), lambda l, j: (0, 0)),
            pl.BlockSpec((1, D_MODEL, tn), lambda l, j: (l, 0, j)),
            pl.BlockSpec((1, 1, tn), lambda l, j: (l, 0, j)),
        ],
        out_specs=pl.BlockSpec((1, n_rows, tn), lambda l, j: (l, 0, j)),
        compiler_params=_cparams(2),
        name="ada_mod",
    )(cvecs, w_ada, b_ada.reshape(DEPTH, 1, 6 * D_MODEL))
    return out.reshape(DEPTH, n_rows, 6, D_MODEL)


def _inproj_kernel(x_ref, mod_ref, g_ref, cos_ref, sin_ref, w_ref, qa_ref, kva_ref, conv_ref, na_ref, gate_ref):
    x = x_ref[...]
    h = (_rms(x, g_ref[...]) * (1.0 + mod_ref[0, 1:2, :]) + mod_ref[0, 0:1, :]).astype(BF16)
    cos = cos_ref[...]
    sin = sin_ref[...]
    lane = lax.broadcasted_iota(I32, (1, LANES), 1)
    first_quarter = (lane & 16) == 0

    def mm(lo, hi):
        return _dot(h, w_ref[:, lo:hi])

    def rope(a):
        outs = []
        for j in range(a.shape[1] // LANES):
            c = a[:, j * LANES:(j + 1) * LANES]
            up = pltpu.roll(c, LANES - 16, 1)
            dn = pltpu.roll(c, 16, 1)
            outs.append(c * cos + jnp.where(first_quarter, up, dn) * sin)
        return jnp.concatenate(outs, axis=1)

    qa_ref[...] = rope(mm(*_QA)).astype(BF16)
    kv = mm(*_KVA)
    kva_ref[:, 0:256] = rope(kv[:, 0:256]).astype(BF16)
    kva_ref[:, 256:512] = kv[:, 256:512].astype(BF16)
    for ref, (base, end) in ((conv_ref, _CONV), (na_ref, _NA), (gate_ref, _GATE)):
        for lo, hi in _col_chunks(end - base):
            ref[:, lo:hi] = mm(base + lo, base + hi).astype(BF16)


def _inproj(x2d, mods, g_pre, cos_t, sin_t, w_arr, seq_len, tm):
    n_tok = x2d.shape[0]
    tiles_per_seq = seq_len // tm
    if mods.shape[0] == 1:
        mod_map = lambda i: (0, 0, 0)
    else:
        mod_map = lambda i: (i // tiles_per_seq, 0, 0)
    widths = (512, 512, 1536, 1536, 3072)
    return pl.pallas_call(
        _inproj_kernel,
        out_shape=[jax.ShapeDtypeStruct((n_tok, w), BF16) for w in widths],
        grid=(n_tok // tm,),
        in_specs=[
            pl.BlockSpec((tm, D_MODEL), lambda i: (i, 0)),
            pl.BlockSpec((1, 6, D_MODEL), mod_map),
            pl.BlockSpec((1, D_MODEL), lambda i: (0, 0)),
            pl.BlockSpec((tm, LANES), lambda i: (i % tiles_per_seq, 0)),
            pl.BlockSpec((tm, LANES), lambda i: (i % tiles_per_seq, 0)),
            pl.BlockSpec((D_MODEL, IN_COLS_ARRANGED), lambda i: (0, 0)),
        ],
        out_specs=[pl.BlockSpec((tm, w), lambda i: (i, 0)) for w in widths],
        compiler_params=_cparams(1),
        name="in_proj",
    )(x2d, mods, g_pre, cos_t, sin_t, w_arr)


def _pair_stack(q, even):
    zero = jnp.zeros_like(q)
    return jnp.concatenate([jnp.where(even, q, zero), jnp.where(even, zero, q)], axis=0)


def _softmax_pv(score_parts, value_parts, sink_col):
    m = score_parts[0].max(axis=1, keepdims=True)
    for s in score_parts[1:]:
        m = jnp.maximum(m, s.max(axis=1, keepdims=True))
    if sink_col is not None:
        m = jnp.maximum(m, sink_col)
    den = None
    acc = None
    for s, v in zip(score_parts, value_parts):
        e = jnp.exp2(s - m)
        d = e.sum(axis=1, keepdims=True)
        den = d if den is None else den + d
        r = _dot(e.astype(BF16), v)
        acc = r if acc is None else acc + r
    if sink_col is not None:
        den = den + jnp.exp2(sink_col - m)
    return acc / den


def _attend(s_ref, e_ref, values, sink_col):
    s = s_ref[...]
    m = s.max(axis=1, keepdims=True)
    if sink_col is not None:
        m = jnp.maximum(m, sink_col)
    e_ref[...] = jnp.exp2(s - m).astype(e_ref.dtype)
    r = _dot(e_ref[...], jnp.concatenate([values, jnp.ones_like(values)], axis=1))
    den = r[:, LANES:2 * LANES]
    if sink_col is not None:
        den = den + jnp.exp2(sink_col - m)
    return r[:, 0:LANES] / den


def _win_attn_kernel(sink_ref, q_ref, kvp_ref, kvm_ref, kvn_ref, kvc_ref, o_ref, s_ref, e_ref, *, seq_len, tq):
    blk0 = pl.program_id(1) * (tq // WINDOW)
    nw = 3 * WINDOW
    lane = lax.broadcasted_iota(I32, (1, LANES), 1)
    even = lane < HEAD_DIM
    n_stack = 4 * WINDOW
    rows = lax.broadcasted_iota(I32, (n_stack, 1), 0)
    piece = rows // WINDOW
    qi = rows - piece * WINDOW
    kj = lax.broadcasted_iota(I32, (1, nw), 1)
    rel = kj - WINDOW - qi
    in_window = (rel <= WINDOW) & (rel >= -WINDOW)
    for kh in range(A_KV_HEADS):
        kl = slice(kh * LANES, (kh + 1) * LANES)
        vl = slice((A_KV_HEADS + kh) * LANES, (A_KV_HEADS + kh + 1) * LANES)
        kspan = jnp.concatenate([kvp_ref[:, kl], kvm_ref[:, kl], kvn_ref[:, kl]], axis=0)
        vspan = jnp.concatenate([kvp_ref[:, vl], kvm_ref[:, vl], kvn_ref[:, vl]], axis=0)
        kc = kvc_ref[:, kl]
        vc = kvc_ref[:, vl]
        sink_col = jnp.zeros((n_stack, 1), F32)
        for j in range(4):
            sink_col = jnp.where(piece == j, sink_ref[kh * 4 + j], sink_col)
        sink_col = sink_col * LOG2E
        for n in range(tq // WINDOW):
            qb = q_ref[n * WINDOW:(n + 1) * WINDOW, 2 * kh * LANES:2 * (kh + 1) * LANES]
            lq = jnp.concatenate([_pair_stack(qb[:, 0:LANES], even), _pair_stack(qb[:, LANES:2 * LANES], even)],
                                 axis=0)
            kpos = (blk0 + n - 1) * WINDOW + kj
            valid = in_window & (kpos >= 0) & (kpos < seq_len)
            s_ref[:, 0:nw] = jnp.where(valid, _dot_nt(lq, kspan[n * WINDOW:(n + 3) * WINDOW]), NEG_BIG)
            s_ref[:, nw:] = _dot_nt(lq, kc)
            r = _attend(s_ref, e_ref, jnp.concatenate([vspan[n * WINDOW:(n + 3) * WINDOW], vc], axis=0), sink_col)
            rs = slice(n * WINDOW, (n + 1) * WINDOW)
            o_ref[rs, 2 * kh * LANES:(2 * kh + 1) * LANES] = jnp.where(
                even, r[0:WINDOW], r[WINDOW:2 * WINDOW]).astype(BF16)
            o_ref[rs, (2 * kh + 1) * LANES:(2 * kh + 2) * LANES] = jnp.where(
                even, r[2 * WINDOW:3 * WINDOW], r[3 * WINDOW:4 * WINDOW]).astype(BF16)


def _win_attn(sink, qa, kva, kva_c, batch, seq_len, ctx_len):
    tq = min(WIN_Q_TILE, seq_len)
    nblk = seq_len // WINDOW
    per = tq // WINDOW
    nq = seq_len // tq
    kvw = kva.shape[1]
    n_keys = 3 * WINDOW + ctx_len
    kern = functools.partial(_win_attn_kernel, seq_len=seq_len, tq=tq)
    return pl.pallas_call(
        kern,
        out_shape=jax.ShapeDtypeStruct((batch * seq_len, BRANCH_W), BF16),
        grid_spec=pltpu.PrefetchScalarGridSpec(
            num_scalar_prefetch=1,
            grid=(batch, nq),
            in_specs=[
                pl.BlockSpec((tq, BRANCH_W), lambda b, i, s: (b * nq + i, 0)),
                pl.BlockSpec((WINDOW, kvw), lambda b, i, s: (b * nblk + jnp.maximum(i * per - 1, 0), 0)),
                pl.BlockSpec((tq, kvw), lambda b, i, s: (b * nq + i, 0)),
                pl.BlockSpec((WINDOW, kvw), lambda b, i, s: (b * nblk + jnp.minimum((i + 1) * per, nblk - 1), 0)),
                pl.BlockSpec((ctx_len, kvw), lambda b, i, s: (b, 0)),
            ],
            out_specs=pl.BlockSpec((tq, BRANCH_W), lambda b, i, s: (b * nq + i, 0)),
            scratch_shapes=[pltpu.VMEM((4 * WINDOW, n_keys), F32), pltpu.VMEM((4 * WINDOW, n_keys), BF16)],
        ),
        compiler_params=_cparams(2),
        name="win_attn",
    )(sink, qa, kva, kva, kva, kva_c)


def _ctx_attn_kernel(sink_ref, q_ref, k_ref, v_ref, o_ref, *, use_sink):
    p = pl.program_id(1)
    n = q_ref.shape[0]
    lane = lax.broadcasted_iota(I32, (1, LANES), 1)
    even = lane < HEAD_DIM
    lq = _pair_stack(q_ref[...], even)
    s = _dot_nt(lq, k_ref[...])
    sink_col = None
    if use_sink:
        rows = lax.broadcasted_iota(I32, (2 * n, 1), 0)
        sink_col = jnp.where(rows < n, sink_ref[2 * p], sink_ref[2 * p + 1]) * LOG2E
    r = _softmax_pv([s], [v_ref[...]], sink_col)
    o_ref[...] = jnp.where(even, r[0:n], r[n:2 * n]).astype(BF16)


def _ctx_attn(sink, q_arr, q_lane0, kv_arr, k_lane0, v_lane0, kv_shared, batch, ctx_len, use_sink):
    def kv_map(lane0):
        if kv_shared:
            return lambda b, p, s: (b, lane0 + p // 2)
        return lambda b, p, s: (b, lane0 + p)

    kern = functools.partial(_ctx_attn_kernel, use_sink=use_sink)
    return pl.pallas_call(
        kern,
        out_shape=jax.ShapeDtypeStruct((batch * ctx_len, BRANCH_W), BF16),
        grid_spec=pltpu.PrefetchScalarGridSpec(
            num_scalar_prefetch=1,
            grid=(batch, BRANCH_W // LANES),
            in_specs=[
                pl.BlockSpec((ctx_len, LANES), lambda b, p, s: (b, q_lane0 + p)),
                pl.BlockSpec((ctx_len, LANES), kv_map(k_lane0)),
                pl.BlockSpec((ctx_len, LANES), kv_map(v_lane0)),
            ],
            out_specs=pl.BlockSpec((ctx_len, LANES), lambda b, p, s: (b, p)),
        ),
        compiler_params=_cparams(2),
        name="ctx_attn_sink" if use_sink else "ctx_attn",
    )(sink, q_arr, kv_arr, kv_arr)


def _na_kernel(q_ref, kp_ref, km_ref, kn_ref, vp_ref, vm_ref, vn_ref, kc_ref, vc_ref, b0_ref, b1_ref, o_ref,
               s_ref, e_ref):
    gq = NA_ROW_GROUP * GRID_W
    nb = 3 * gq
    lane = lax.broadcasted_iota(I32, (1, LANES), 1)
    even = lane < HEAD_DIM
    for p in range(NA_HEADS // 2):
        pl_ = slice(p * LANES, (p + 1) * LANES)
        kc = kc_ref[:, pl_]
        vc = vc_ref[:, pl_]
        spans = (
            (jnp.concatenate([kp_ref[gq:2 * gq, pl_], km_ref[:, pl_]], axis=0),
             jnp.concatenate([vp_ref[gq:2 * gq, pl_], vm_ref[:, pl_]], axis=0)),
            (jnp.concatenate([km_ref[:, pl_], kn_ref[0:gq, pl_]], axis=0),
             jnp.concatenate([vm_ref[:, pl_], vn_ref[0:gq, pl_]], axis=0)),
        )
        for g, b_ref in enumerate((b0_ref, b1_ref)):
            kspan, vspan = spans[g]
            lq = _pair_stack(q_ref[g * gq:(g + 1) * gq, pl_], even)
            bias = jnp.concatenate([b_ref[0, 2 * p], b_ref[0, 2 * p + 1]], axis=0)
            s_ref[:, 0:nb] = _dot_nt(lq, kspan) + bias
            s_ref[:, nb:] = _dot_nt(lq, kc)
            r = _attend(s_ref, e_ref, jnp.concatenate([vspan, vc], axis=0), None)
            o_ref[g * gq:(g + 1) * gq, pl_] = jnp.where(even, r[0:gq], r[gq:2 * gq]).astype(BF16)


_N_DR = 2 * NA_ROWS_MAX - 1
_N_DC = 2 * NA_COLS - 1


def _na_bias_kernel(rpb_ref, o_ref):
    h = pl.program_id(0)
    qc = lax.broadcasted_iota(I32, (GRID_W, LANES), 0)
    lane = lax.broadcasted_iota(I32, (GRID_W, LANES), 1)
    kc = lane & (GRID_W - 1)
    dc = jnp.clip(kc - qc, -(NA_COLS - 1), NA_COLS - 1) + NA_COLS - 1
    col_start = jnp.clip(qc - NA_COLS // 2, 0, GRID_W - NA_COLS)
    col_ok = (kc >= col_start) & (kc < col_start + NA_COLS)
    neg = jnp.full((GRID_W, LANES), NEG_BIG, F32)
    col_bias = []
    for dr in range(_N_DR):
        c = neg
        for d in range(_N_DC):
            c = jnp.where(dc == d, rpb_ref[(h * _N_DR + dr) * _N_DC + d], c)
        col_bias.append(jnp.where(col_ok, c * LOG2E, neg))
    row_ok = (
        lambda i, j: 4 <= j < 4 + NA_ROWS_MAX,
        lambda i, j: i <= j < i + NA_ROWS_MAX,
        lambda i, j: 0 <= j < NA_ROWS_MAX,
    )
    for v in range(3):
        for i in range(NA_ROW_GROUP):
            for m in range(3 * NA_ROW_GROUP // 2):
                halves = [col_bias[j - i + 3] if row_ok[v](i, j) else neg for j in (2 * m, 2 * m + 1)]
                o_ref[v, 0, i * GRID_W:(i + 1) * GRID_W, m * LANES:(m + 1) * LANES] = jnp.where(
                    lane < GRID_W, halves[0], halves[1])


def _na_bias_tables(rpb):
    shape = (3, NA_HEADS, NA_ROW_GROUP * GRID_W, 3 * NA_ROW_GROUP * GRID_W)
    return pl.pallas_call(
        _na_bias_kernel,
        out_shape=jax.ShapeDtypeStruct(shape, F32),
        grid_spec=pltpu.PrefetchScalarGridSpec(
            num_scalar_prefetch=1,
            grid=(NA_HEADS,),
            in_specs=[],
            out_specs=pl.BlockSpec((3, 1) + shape[2:], lambda h, r: (0, h, 0, 0)),
        ),
        compiler_params=_cparams(1),
        name="na_bias",
    )(rpb.reshape(-1))


def _na_attn(na, na_c, bias_tab, batch, seq_len, ctx_len):
    rows = seq_len // GRID_W
    assert rows % (2 * NA_ROW_GROUP) == 0 and rows >= 4 * NA_ROW_GROUP
    tq = 2 * NA_ROW_GROUP * GRID_W
    nq = seq_len // tq
    gq = NA_ROW_GROUP * GRID_W
    n_keys = 3 * gq + ctx_len

    def tok_map(part, shift):
        return lambda b, i: (b * nq + jnp.clip(i + shift, 0, nq - 1), part)

    bias_block = (1, NA_HEADS, gq, 3 * gq)
    return pl.pallas_call(
        _na_kernel,
        out_shape=jax.ShapeDtypeStruct((batch * seq_len, BRANCH_W), BF16),
        grid=(batch, nq),
        in_specs=[
            pl.BlockSpec((tq, BRANCH_W), tok_map(0, 0)),
            pl.BlockSpec((tq, BRANCH_W), tok_map(1, -1)),
            pl.BlockSpec((tq, BRANCH_W), tok_map(1, 0)),
            pl.BlockSpec((tq, BRANCH_W), tok_map(1, 1)),
            pl.BlockSpec((tq, BRANCH_W), tok_map(2, -1)),
            pl.BlockSpec((tq, BRANCH_W), tok_map(2, 0)),
            pl.BlockSpec((tq, BRANCH_W), tok_map(2, 1)),
            pl.BlockSpec((ctx_len, BRANCH_W), lambda b, i: (b, 1)),
            pl.BlockSpec((ctx_len, BRANCH_W), lambda b, i: (b, 2)),
            pl.BlockSpec(bias_block, lambda b, i: (jnp.where(i == 0, 0, 1), 0, 0, 0), pipeline_mode=pl.Buffered(1)),
            pl.BlockSpec(bias_block, lambda b, i: (jnp.where(i == nq - 1, 2, 1), 0, 0, 0),
                         pipeline_mode=pl.Buffered(1)),
        ],
        out_specs=pl.BlockSpec((tq, BRANCH_W), lambda b, i: (b * nq + i, 0)),
        scratch_shapes=[pltpu.VMEM((2 * gq, n_keys), F32), pltpu.VMEM((2 * gq, n_keys), BF16)],
        compiler_params=_cparams(2),
        name="na_attn",
    )(na, na, na, na, na, na, na, na_c, na_c, bias_tab, bias_tab)


def _merge_kernel(oa_ref, conv_ref, cprev_ref, cnext_ref, oc_ref, gate_ref, x_ref, mod_ref, wb_ref, bg_ref, cw_ref,
                  wo_ref, gpost_ref, out_ref, y_ref, *, tiles_per_seq):
    tm = x_ref.shape[0]
    ti = pl.program_id(0) % tiles_per_seq
    ux = conv_ref[:, 0:512].astype(F32)
    gb = conv_ref[:, 512:1024].astype(F32)
    gc = conv_ref[:, 1024:1536].astype(F32)
    u = gc * ux
    cprev = cprev_ref[...].astype(F32)
    cnext = cnext_ref[...].astype(F32)
    halo = cprev.shape[0]
    u_before = cprev[halo - 1:halo, 1024:1536] * cprev[halo - 1:halo, 0:512]
    u_after = cnext[0:1, 1024:1536] * cnext[0:1, 0:512]
    u_before = jnp.where(ti == 0, 0.0, u_before)
    u_after = jnp.where(ti == tiles_per_seq - 1, 0.0, u_after)
    row = lax.broadcasted_iota(I32, (tm, 1), 0)
    u_dn = jnp.where(row == 0, u_before, pltpu.roll(u, 1, 0))
    u_up = jnp.where(row == tm - 1, u_after, pltpu.roll(u, tm - 1, 0))
    cw = cw_ref[...]
    ob = (gb * (cw[0:1] * u_dn + cw[1:2] * u + cw[2:3] * u_up)).astype(BF16)
    oa = oa_ref[...]
    oc = oc_ref[...]
    for lo, hi in _col_chunks(D_MODEL):
        acc = None
        for r, o in enumerate((oa, ob, oc)):
            gate = jax.nn.sigmoid(gate_ref[:, r * D_MODEL + lo:r * D_MODEL + hi].astype(F32)
                                  + bg_ref[:, r * D_MODEL + lo:r * D_MODEL + hi])
            term = gate * _dot(o, wb_ref[r, :, lo:hi])
            acc = term if acc is None else acc + term
        y_ref[:, lo:hi] = acc.astype(BF16)
    y2 = _dot(y_ref[...], wo_ref[...])
    out_ref[...] = x_ref[...] + mod_ref[0, 2:3, :] * _rms(y2, gpost_ref[...])


def _merge(oa, conv, oc, gate, x2d, mods, wb, b_gate, conv_w, wo, g_post, seq_len, tm):
    n_tok = x2d.shape[0]
    tiles_per_seq = seq_len // tm
    halo = 16
    hb = tm // halo
    n_halo = n_tok // halo
    if mods.shape[0] == 1:
        mod_map = lambda i: (0, 0, 0)
    else:
        mod_map = lambda i: (i // tiles_per_seq, 0, 0)
    kern = functools.partial(_merge_kernel, tiles_per_seq=tiles_per_seq)
    const2 = lambda i: (0, 0)
    return pl.pallas_call(
        kern,
        out_shape=jax.ShapeDtypeStruct((n_tok, D_MODEL), F32),
        grid=(n_tok // tm,),
        in_specs=[
            pl.BlockSpec((tm, BRANCH_W), lambda i: (i, 0)),
            pl.BlockSpec((tm, 3 * BRANCH_W), lambda i: (i, 0)),
            pl.BlockSpec((halo, 3 * BRANCH_W), lambda i: (jnp.maximum(i * hb - 1, 0), 0)),
            pl.BlockSpec((halo, 3 * BRANCH_W), lambda i: (jnp.minimum((i + 1) * hb, n_halo - 1), 0)),
            pl.BlockSpec((tm, BRANCH_W), lambda i: (i, 0)),
            pl.BlockSpec((tm, N_BRANCH * D_MODEL), lambda i: (i, 0)),
            pl.BlockSpec((tm, D_MODEL), lambda i: (i, 0)),
            pl.BlockSpec((1, 6, D_MODEL), mod_map),
            pl.BlockSpec((N_BRANCH, BRANCH_W, D_MODEL), lambda i: (0, 0, 0)),
            pl.BlockSpec((1, N_BRANCH * D_MODEL), const2),
            pl.BlockSpec((CONV_K, BRANCH_W), const2),
            pl.BlockSpec((D_MODEL, D_MODEL), const2),
            pl.BlockSpec((1, D_MODEL), const2),
        ],
        out_specs=pl.BlockSpec((tm, D_MODEL), lambda i: (i, 0)),
        scratch_shapes=[pltpu.VMEM((tm, D_MODEL), BF16)],
        compiler_params=_cparams(1),
        name="branch_merge",
    )(oa, conv, conv, conv, oc, gate, x2d, mods, wb, b_gate, conv_w, wo, g_post)


def _ffn_kernel(x_ref, mod_ref, gpre_ref, wgu_ref, wdn_ref, gpost_ref, out_ref, act_ref):
    x = x_ref[...]
    h = (_rms(x, gpre_ref[...]) * (1.0 + mod_ref[0, 4:5, :]) + mod_ref[0, 3:4, :]).astype(BF16)
    for lo, hi in _col_chunks(FFN_DENSE):
        g = _dot(h, wgu_ref[:, lo:hi])
        u = _dot(h, wgu_ref[:, FFN_DENSE + lo:FFN_DENSE + hi])
        act_ref[:, lo:hi] = (g * jax.nn.sigmoid(g) * u).astype(BF16)
    y = _dot(act_ref[...], wdn_ref[...])
    out_ref[...] = x + mod_ref[0, 5:6, :] * _rms(y, gpost_ref[...])


def _ffn_dense(x2d, mods, g_pre, wgu, wdn, g_post, seq_len, tm):
    n_tok = x2d.shape[0]
    tiles_per_seq = seq_len // tm
    if mods.shape[0] == 1:
        mod_map = lambda i: (0, 0, 0)
    else:
        mod_map = lambda i: (i // tiles_per_seq, 0, 0)
    const2 = lambda i: (0, 0)
    return pl.pallas_call(
        _ffn_kernel,
        out_shape=jax.ShapeDtypeStruct((n_tok, D_MODEL), F32),
        grid=(n_tok // tm,),
        in_specs=[
            pl.BlockSpec((tm, D_MODEL), lambda i: (i, 0)),
            pl.BlockSpec((1, 6, D_MODEL), mod_map),
            pl.BlockSpec((1, D_MODEL), const2),
            pl.BlockSpec((D_MODEL, 2 * FFN_DENSE), const2),
            pl.BlockSpec((FFN_DENSE, D_MODEL), const2),
            pl.BlockSpec((1, D_MODEL), const2),
        ],
        out_specs=pl.BlockSpec((tm, D_MODEL), lambda i: (i, 0)),
        scratch_shapes=[pltpu.VMEM((tm, FFN_DENSE), BF16)],
        compiler_params=_cparams(1),
        name="ffn_dense",
    )(x2d, mods, g_pre, wgu, wdn, g_post)


def _to_token_tiles(a):
    slabs = jnp.stack([a[:, s * LANES:(s + 1) * LANES] for s in range(D_MODEL // LANES)], axis=0)
    return pltpu.einshape("stl->tsl", slabs)


def _from_token_tiles(t):
    slabs = pltpu.einshape("tsl->stl", t)
    return jnp.concatenate([slabs[s] for s in range(D_MODEL // LANES)], axis=1)


TOKEN_TILE_SHAPE = (D_MODEL // LANES, LANES)


def _router_kernel(x_ref, mod_ref, gpre_ref, wr_ref, br_ref, h_ref, e_ref, gt_ref):
    x = x_ref[...]
    h = _rms(x, gpre_ref[...]) * (1.0 + mod_ref[0, 4:5, :]) + mod_ref[0, 3:4, :]
    h_ref[...] = _to_token_tiles(h)
    logits = lax.dot_general(wr_ref[...], h, (((1,), (1,)), ((), ())), preferred_element_type=F32,
                             precision=lax.Precision.HIGHEST) + br_ref[...]
    eid = lax.broadcasted_iota(I32, logits.shape, 0)
    m1 = logits.max(axis=0, keepdims=True)
    i1 = jnp.min(jnp.where(logits == m1, eid, N_EXPERTS), axis=0, keepdims=True)
    rest = jnp.where(eid == i1, -jnp.inf, logits)
    m2 = rest.max(axis=0, keepdims=True)
    i2 = jnp.min(jnp.where(rest == m2, eid, N_EXPERTS), axis=0, keepdims=True)
    e2 = jnp.exp(m2 - m1)
    den = 1.0 + e2
    e_ref[...] = jnp.concatenate([i1, i2], axis=0)
    gt_ref[...] = jnp.concatenate([1.0 / den, e2 / den], axis=0)


def _router(x2d, mods, g_pre, w_router_t, b_router, seq_len, tm):
    n_tok = x2d.shape[0]
    tiles_per_seq = seq_len // tm
    const2 = lambda i: (0, 0)
    return pl.pallas_call(
        _router_kernel,
        out_shape=[
            jax.ShapeDtypeStruct((n_tok,) + TOKEN_TILE_SHAPE, F32),
            jax.ShapeDtypeStruct((TOP_K, n_tok), I32),
            jax.ShapeDtypeStruct((TOP_K, n_tok), F32),
        ],
        grid=(n_tok // tm,),
        in_specs=[
            pl.BlockSpec((tm, D_MODEL), lambda i: (i, 0)),
            pl.BlockSpec((1, 6, D_MODEL), lambda i: (i // tiles_per_seq, 0, 0)),
            pl.BlockSpec((1, D_MODEL), const2),
            pl.BlockSpec((N_EXPERTS, D_MODEL), const2),
            pl.BlockSpec((N_EXPERTS, 1), const2),
        ],
        out_specs=[
            pl.BlockSpec((tm,) + TOKEN_TILE_SHAPE, lambda i: (i, 0, 0)),
            pl.BlockSpec((TOP_K, tm), lambda i: (0, i)),
            pl.BlockSpec((TOP_K, tm), lambda i: (0, i)),
        ],
        compiler_params=_cparams(1),
        name="moe_router",
    )(x2d, mods, g_pre, w_router_t, b_router)


def _rank_kernel(e_ref, rank_ref, cnt_ref, carry_ref):
    tm = e_ref.shape[1]

    @pl.when(pl.program_id(0) == 0)
    def _():
        carry_ref[...] = jnp.zeros_like(carry_ref)

    e = e_ref[...]
    eid = lax.broadcasted_iota(I32, (N_EXPERTS, tm), 0)
    oh0 = eid == e[0:1, :]
    oh1 = eid == e[1:2, :]
    oh = jnp.concatenate([oh0, oh1], axis=0).astype(F32)
    tri = (lax.broadcasted_iota(I32, (tm, tm), 0) <= lax.broadcasted_iota(I32, (tm, tm), 1)).astype(BF16)
    incl = _dot(oh.astype(BF16), tri)
    excl = incl - oh
    tot = incl[:, tm - 1:tm]
    carry = carry_ref[:, 0:1]
    rank0 = carry + excl[0:N_EXPERTS]
    rank1 = carry + tot[0:N_EXPERTS] + excl[N_EXPERTS:2 * N_EXPERTS]
    r0 = jnp.sum(jnp.where(oh0, rank0, 0.0), axis=0, keepdims=True)
    r1 = jnp.sum(jnp.where(oh1, rank1, 0.0), axis=0, keepdims=True)
    rank_ref[...] = jnp.concatenate([r0, r1], axis=0).astype(I32)
    new_carry = carry + tot[0:N_EXPERTS] + tot[N_EXPERTS:2 * N_EXPERTS]
    carry_ref[...] = jnp.broadcast_to(new_carry, carry_ref.shape)
    cnt_ref[...] = jnp.broadcast_to(new_carry, cnt_ref.shape).astype(I32)


def _ranks(e_idx, tm):
    n_tok = e_idx.shape[1]
    return pl.pallas_call(
        _rank_kernel,
        out_shape=[
            jax.ShapeDtypeStruct((TOP_K, n_tok), I32),
            jax.ShapeDtypeStruct((N_EXPERTS, LANES), I32),
        ],
        grid=(n_tok // tm,),
        in_specs=[pl.BlockSpec((TOP_K, tm), lambda i: (0, i))],
        out_specs=[
            pl.BlockSpec((TOP_K, tm), lambda i: (0, i)),
            pl.BlockSpec((N_EXPERTS, LANES), lambda i: (0, 0)),
        ],
        scratch_shapes=[pltpu.VMEM((N_EXPERTS, LANES), F32)],
        compiler_params=_cparams(1),
        name="moe_rank",
    )(e_idx)


def _dispatch_kernel(dest_ref, pad_ref, h_ref, xs_ref, zero_ref, sem, zsem):
    tm = h_ref.shape[0]
    n_tok = dest_ref.shape[0] // TOP_K
    base = pl.program_id(0) * tm

    @pl.when(pl.program_id(0) == 0)
    def _():
        zero_ref[...] = jnp.zeros_like(zero_ref)

        zrows = zero_ref.shape[0]

        def zero_copy(r):
            return pltpu.make_async_copy(zero_ref.at[0], xs_ref.at[r], zsem)

        def zero_chunk(j):
            return pltpu.make_async_copy(zero_ref, xs_ref.at[pl.ds(pl.multiple_of(j * zrows, zrows), zrows)], zsem)

        tail = (pad_ref[2 * N_EXPERTS] // zrows, xs_ref.shape[0] // zrows)
        for e in range(N_EXPERTS):
            lo, hi = pad_ref[e], pad_ref[N_EXPERTS + e]
            lax.fori_loop(lo, hi, lambda r, c: (zero_copy(r).start(), c)[1], 0)
        lax.fori_loop(tail[0], tail[1], lambda j, c: (zero_chunk(j).start(), c)[1], 0)
        for e in range(N_EXPERTS):
            lo, hi = pad_ref[e], pad_ref[N_EXPERTS + e]
            lax.fori_loop(lo, hi, lambda r, c: (zero_copy(r).wait(), c)[1], 0)
        lax.fori_loop(tail[0], tail[1], lambda j, c: (zero_chunk(j).wait(), c)[1], 0)

    def start(t, carry):
        for k in range(TOP_K):
            d = dest_ref[k * n_tok + base + t]
            pltpu.make_async_copy(h_ref.at[t], xs_ref.at[d], sem).start(priority=k)
        return carry

    lax.fori_loop(0, tm, start, 0, unroll=ISSUE_UNROLL)
    for k in range(TOP_K):
        pltpu.make_async_copy(h_ref, xs_ref.at[pl.ds(0, tm)], sem).wait()


def _dispatch(dest_flat, pad_rows, h, n_rows, tm):
    n_tok = h.shape[0]
    return pl.pallas_call(
        _dispatch_kernel,
        out_shape=jax.ShapeDtypeStruct((n_rows,) + TOKEN_TILE_SHAPE, F32),
        grid_spec=pltpu.PrefetchScalarGridSpec(
            num_scalar_prefetch=2,
            grid=(n_tok // tm,),
            in_specs=[pl.BlockSpec((tm,) + TOKEN_TILE_SHAPE, lambda i, d, p: (i, 0, 0))],
            out_specs=pl.BlockSpec(memory_space=pl.ANY),
            scratch_shapes=[
                pltpu.VMEM((64,) + TOKEN_TILE_SHAPE, F32),
                pltpu.SemaphoreType.DMA,
                pltpu.SemaphoreType.DMA,
            ],
        ),
        compiler_params=_cparams(1),
        name="moe_dispatch",
    )(dest_flat, pad_rows, h)


def _expert_kernel(be_ref, nb_ref, xs_ref, wgu_ref, wdn_ref, y_ref, act_ref):
    @pl.when(pl.program_id(0) < nb_ref[0])
    def _():
        xb = _from_token_tiles(xs_ref[...]).astype(BF16)
        for lo, hi in _col_chunks(FFN_EXPERT):
            g = _dot(xb, wgu_ref[0, :, lo:hi])
            u = _dot(xb, wgu_ref[0, :, FFN_EXPERT + lo:FFN_EXPERT + hi])
            act_ref[:, lo:hi] = (g * jax.nn.sigmoid(g) * u).astype(BF16)
        y_ref[...] = _to_token_tiles(_dot(act_ref[...], wdn_ref[0]))

    @pl.when(pl.program_id(0) >= nb_ref[0])
    def _():
        y_ref[...] = jnp.zeros_like(y_ref)


def _expert_ffn(block_e, n_used, xs, wgu, wdn, bm):
    n_rows = xs.shape[0]
    n_blocks = n_rows // bm

    def blk(j, be, nb):
        return jnp.minimum(j, nb[0] - 1)

    return pl.pallas_call(
        _expert_kernel,
        out_shape=jax.ShapeDtypeStruct((n_rows,) + TOKEN_TILE_SHAPE, F32),
        grid_spec=pltpu.PrefetchScalarGridSpec(
            num_scalar_prefetch=2,
            grid=(n_blocks,),
            in_specs=[
                pl.BlockSpec((bm,) + TOKEN_TILE_SHAPE, lambda j, be, nb: (blk(j, be, nb), 0, 0)),
                pl.BlockSpec((1, D_MODEL, 2 * FFN_EXPERT), lambda j, be, nb: (be[blk(j, be, nb)], 0, 0),
                             pipeline_mode=pl.Buffered(1)),
                pl.BlockSpec((1, FFN_EXPERT, D_MODEL), lambda j, be, nb: (be[blk(j, be, nb)], 0, 0),
                             pipeline_mode=pl.Buffered(1)),
            ],
            out_specs=pl.BlockSpec((bm,) + TOKEN_TILE_SHAPE, lambda j, be, nb: (j, 0, 0)),
            scratch_shapes=[pltpu.VMEM((bm, FFN_EXPERT), BF16)],
        ),
        compiler_params=_cparams(1),
        name="moe_experts",
    )(block_e, n_used, xs, wgu, wdn)


def _combine_kernel(dest_ref, ys_ref, gt_ref, x_ref, mod_ref, gpost_ref, out_ref, y0_ref, y1_ref, sem):
    tm = x_ref.shape[0]
    n_tok = dest_ref.shape[0] // TOP_K
    base = pl.program_id(0) * tm
    bufs = (y0_ref, y1_ref)

    def start(t, carry):
        for k in range(TOP_K):
            d = dest_ref[k * n_tok + base + t]
            pltpu.make_async_copy(ys_ref.at[d], bufs[k].at[t], sem).start(priority=k)
        return carry

    lax.fori_loop(0, tm, start, 0, unroll=ISSUE_UNROLL)
    for k in range(TOP_K):
        pltpu.make_async_copy(ys_ref.at[pl.ds(0, tm)], bufs[k], sem).wait()
    gt = gt_ref[...]
    y = gt[:, 0:1] * _from_token_tiles(y0_ref[...]) + gt[:, 1:2] * _from_token_tiles(y1_ref[...])
    out_ref[...] = x_ref[...] + mod_ref[0, 5:6, :] * _rms(y, gpost_ref[...])


def _combine(dest_flat, ys, gates_t, x2d, mods, g_post, seq_len, tm):
    n_tok = x2d.shape[0]
    tiles_per_seq = seq_len // tm
    return pl.pallas_call(
        _combine_kernel,
        out_shape=jax.ShapeDtypeStruct((n_tok, D_MODEL), F32),
        grid_spec=pltpu.PrefetchScalarGridSpec(
            num_scalar_prefetch=1,
            grid=(n_tok // tm,),
            in_specs=[
                pl.BlockSpec(memory_space=pl.ANY),
                pl.BlockSpec((tm, TOP_K), lambda i, d: (i, 0)),
                pl.BlockSpec((tm, D_MODEL), lambda i, d: (i, 0)),
                pl.BlockSpec((1, 6, D_MODEL), lambda i, d: (i // tiles_per_seq, 0, 0)),
                pl.BlockSpec((1, D_MODEL), lambda i, d: (0, 0)),
            ],
            out_specs=pl.BlockSpec((tm, D_MODEL), lambda i, d: (i, 0)),
            scratch_shapes=[
                pltpu.VMEM((tm,) + TOKEN_TILE_SHAPE, F32),
                pltpu.VMEM((tm,) + TOKEN_TILE_SHAPE, F32),
                pltpu.SemaphoreType.DMA,
            ],
        ),
        compiler_params=_cparams(1),
        name="moe_combine",
    )(dest_flat, ys, gates_t, x2d, mods, g_post)


def _moe(x2d, mods, g_pre, g_post, w_router, b_router, wgu, wdn, seq_len, tm):
    n_tok = x2d.shape[0]
    bm = MOE_BLOCK_ROWS
    h, e_idx, gates = _router(x2d, mods, g_pre, w_router.T, b_router.reshape(N_EXPERTS, 1), seq_len, tm)
    rank, cnt = _ranks(e_idx, tm)
    counts = cnt[:, 0]
    padded = (counts + bm - 1) // bm * bm
    ends = jnp.cumsum(padded)
    pstart = ends - padded
    dest = rank
    for e in range(N_EXPERTS):
        dest = dest + jnp.where(e_idx == e, pstart[e], 0)
    dest_flat = dest.reshape(-1).astype(I32)
    n_blocks = n_tok * TOP_K // bm + N_EXPERTS
    blk_start = jnp.arange(n_blocks, dtype=I32) * bm
    block_e = jnp.minimum(jnp.sum(blk_start[:, None] >= ends[None, :], axis=1), N_EXPERTS - 1).astype(I32)
    n_used = (ends[-1:] // bm).astype(I32)
    pad_rows = jnp.concatenate([pstart + counts, ends, ends[-1:]]).astype(I32)
    xs = _dispatch(dest_flat, pad_rows, h, n_blocks * bm, tm)
    ys = _expert_ffn(block_e, n_used, xs, wgu, wdn, bm)
    return _combine(dest_flat, ys, gates.T, x2d, mods, g_post, seq_len, tm)


def _rope_tables(n_tokens):
    pos = jnp.arange(n_tokens)
    row = (pos // GRID_W).astype(F32)
    col = (pos % GRID_W).astype(F32)
    half = HEAD_DIM // 2
    inv = 1.0 / (ROPE_BASE ** (jnp.arange(0, half, 2, dtype=F32) / half))
    ar = row[:, None] * inv[None]
    ac = col[:, None] * inv[None]
    ang = jnp.concatenate([ar, ar, ac, ac], axis=-1)
    ang = jnp.concatenate([ang, ang], axis=-1)
    sign = jnp.where((jnp.arange(LANES) & 16) == 0, -1.0, 1.0).astype(F32)
    return jnp.cos(ang), jnp.sin(ang) * sign[None]


def _arrange_w_in(w):
    scale = HEAD_DIM ** -0.5 * LOG2E
    offs = np.cumsum([0, BRANCH_W, 128, 128, BRANCH_W, BRANCH_W, BRANCH_W, BRANCH_W, BRANCH_W, BRANCH_W, N_BRANCH * D_MODEL])
    parts = [w[:, offs[i]:offs[i + 1]] for i in range(10)]
    qa, ka, va, ux, gb, gc, qn, kn, vn, gx = parts

    def dup(t):
        return jnp.concatenate([t[:, 0:64], t[:, 0:64], t[:, 64:128], t[:, 64:128]], axis=1)

    return jnp.concatenate([qa * scale, dup(ka), dup(va), ux, gb, gc, qn * scale, kn, vn, gx], axis=1).astype(BF16)


def kernel(x, c, ctx, c_ctx, w_ada, b_ada, g_pre_mix, g_post_mix, g_pre_ffn, g_post_ffn, w_in, b_gate, sink, conv_w,
           rpb, w_branch, w_out, w_gu_dense, w_dn_dense, w_router, b_router, w_gu_moe, w_dn_moe):
    batch, seq_len, d = x.shape
    ctx_len = ctx.shape[1]
    tm = min(TOKEN_TILE, seq_len)
    tmc = min(TOKEN_TILE, ctx_len)

    n_vec = batch + 1
    n_vec_pad = -(-n_vec // 8) * 8
    cvecs = jnp.concatenate([c, c_ctx[None], jnp.zeros((n_vec_pad - n_vec, d), F32)], axis=0)
    mods = _ada_mods(cvecs, w_ada, b_ada)

    cos_t, sin_t = _rope_tables(seq_len)
    cos_c = jnp.ones((ctx_len, LANES), F32)
    sin_c = jnp.zeros((ctx_len, LANES), F32)

    x2 = x.reshape(batch * seq_len, d)
    hc2 = ctx.reshape(batch * ctx_len, d)
    for l in range(DEPTH):
        update_ctx = l < DEPTH - 1
        mods_x = mods[l, :batch]
        mods_c = mods[l, batch:batch + 1]
        w_arr = _arrange_w_in(w_in[l])
        wb = w_branch[l].astype(BF16)
        wo = w_out[l].astype(BF16)
        bg = b_gate[l].reshape(1, N_BRANCH * D_MODEL)
        g1 = g_pre_mix[l].reshape(1, d)
        g2 = g_post_mix[l].reshape(1, d)
        g3 = g_pre_ffn[l].reshape(1, d)
        g4 = g_post_ffn[l].reshape(1, d)

        qa, kva, conv, na, gate = _inproj(x2, mods_x, g1, cos_t, sin_t, w_arr, seq_len, tm)
        qa_c, kva_c, conv_c, na_c, gate_c = _inproj(hc2, mods_c, g1, cos_c, sin_c, w_arr, ctx_len, tmc)

        o_a = _win_attn(sink[l], qa, kva, kva_c, batch, seq_len, ctx_len)
        o_c = _na_attn(na, na_c, _na_bias_tables(rpb[l]), batch, seq_len, ctx_len)
        x2 = _merge(o_a, conv, o_c, gate, x2, mods_x, wb, bg, conv_w[l], wo, g2, seq_len, tm)

        if l % 2 == 0:
            wgu = w_gu_dense[l // 2].astype(BF16)
            wdn = w_dn_dense[l // 2].astype(BF16)
            x2 = _ffn_dense(x2, mods_x, g3, wgu, wdn, g4, seq_len, tm)
        else:
            x2 = _moe(x2, mods_x, g3, g4, w_router[l // 2], b_router[l // 2], w_gu_moe[l // 2].astype(BF16),
                      w_dn_moe[l // 2].astype(BF16), seq_len, tm)

        if update_ctx:
            o_a_c = _ctx_attn(sink[l], qa_c, 0, kva_c, 0, 2, True, batch, ctx_len, True)
            o_c_c = _ctx_attn(sink[l], na_c, 0, na_c, 4, 8, False, batch, ctx_len, False)
            hc2 = _merge(o_a_c, conv_c, o_c_c, gate_c, hc2, mods_c, wb, bg, conv_w[l], wo, g2, ctx_len, tmc)
            if l % 2 == 0:
                hc2 = _ffn_dense(hc2, mods_c, g3, wgu, wdn, g4, ctx_len, tmc)
            else:
                hc2 = _moe(hc2, jnp.broadcast_to(mods_c, (batch,) + mods_c.shape[1:]), g3, g4, w_router[l // 2],
                           b_router[l // 2], w_gu_moe[l // 2].astype(BF16), w_dn_moe[l // 2].astype(BF16),
                           ctx_len, tmc)
    return x2.reshape(batch, seq_len, d)
```

```python
import functools

import numpy as np
import jax
import jax.numpy as jnp
from jax import lax
from jax.experimental import pallas as pl
from jax.experimental.pallas import tpu as pltpu

F32 = jnp.float32
BF16 = jnp.bfloat16
I32 = jnp.int32

D_MODEL = 1024
DEPTH = 2
GRID_W = 64
HEAD_DIM = 64
BRANCH_W = 512
N_BRANCH = 3
A_HEADS = 8
A_KV_HEADS = 2
WINDOW = 128
CONV_K = 3
NA_HEADS = 8
NA_ROWS_MAX = 8
NA_COLS = 16
ROPE_BASE = 10000.0
FFN_DENSE = 2816
N_EXPERTS = 8
TOP_K = 2
FFN_EXPERT = 3584
RMS_EPS = 1e-6

LANES = 128
MXU_COLS = 256
VMEM_LIMIT_BYTES = 56 * 1024 * 1024

NEG_BIG = -1e30
LOG2E = float(np.log2(np.e))
TOKEN_TILE = 512
WIN_Q_TILE = 512
NA_ROW_GROUP = 4
MOE_BLOCK_ROWS = 512
ISSUE_UNROLL = 8

_QA = (0, 512)
_KVA = (512, 1024)
_CONV = (1024, 2560)
_NA = (2560, 4096)
_GATE = (4096, 7168)
IN_COLS_ARRANGED = 7168


def _cparams(n_grid_dims, vmem=VMEM_LIMIT_BYTES):
    return pltpu.CompilerParams(dimension_semantics=("arbitrary",) * n_grid_dims, vmem_limit_bytes=vmem)


def _dot(a, b):
    return jnp.dot(a, b, preferred_element_type=F32)


def _dot_nt(a, b):
    return lax.dot_general(a, b, (((1,), (1,)), ((), ())), preferred_element_type=F32)


def _rms(x, g):
    return x * lax.rsqrt(jnp.mean(x * x, axis=-1, keepdims=True) + RMS_EPS) * g


def _col_chunks(n, step=512):
    return [(lo, min(lo + step, n)) for lo in range(0, n, step)]


def _ada_kernel(c_ref, w_ref, b_ref, o_ref):
    cv = c_ref[...]
    a = cv * jax.nn.sigmoid(cv)
    o_ref[0] = jnp.dot(a, w_ref[0], preferred_element_type=F32, precision=lax.Precision.HIGHEST) + b_ref[0]


def _ada_mods(cvecs, w_ada, b_ada):
    n_rows = cvecs.shape[0]
    tn = 1536
    out = pl.pallas_call(
        _ada_kernel,
        out_shape=jax.ShapeDtypeStruct((DEPTH, n_rows, 6 * D_MODEL), F32),
        grid=(DEPTH, 6 * D_MODEL // tn),
        in_specs=[
            pl.BlockSpec((n_rows, D_MODEL), lambda l, j: (0, 0)),
            pl.BlockSpec((1, D_MODEL, tn), lambda l, j: (l, 0, j)),
            pl.BlockSpec((1, 1, tn), lambda l, j: (l, 0, j)),
        ],
        out_specs=pl.BlockSpec((1, n_rows, tn), lambda l, j: (l, 0, j)),
        compiler_params=_cparams(2),
        name="ada_mod",
    )(cvecs, w_ada, b_ada.reshape(DEPTH, 1, 6 * D_MODEL))
    return out.reshape(DEPTH, n_rows, 6, D_MODEL)


def _inproj_kernel(x_ref, mod_ref, g_ref, cos_ref, sin_ref, w_ref, qa_ref, kva_ref, conv_ref, na_ref, gate_ref):
    x = x_ref[...]
    h = (_rms(x, g_ref[...]) * (1.0 + mod_ref[0, 1:2, :]) + mod_ref[0, 0:1, :]).astype(BF16)
    cos = cos_ref[...]
    sin = sin_ref[...]
    lane = lax.broadcasted_iota(I32, (1, LANES), 1)
    first_quarter = (lane & 16) == 0

    def mm(lo, hi):
        return _dot(h, w_ref[:, lo:hi])

    def rope(a):
        outs = []
        for j in range(a.shape[1] // LANES):
            c = a[:, j * LANES:(j + 1) * LANES]
            up = pltpu.roll(c, LANES - 16, 1)
            dn = pltpu.roll(c, 16, 1)
            outs.append(c * cos + jnp.where(first_quarter, up, dn) * sin)
        return jnp.concatenate(outs, axis=1)

    qa_ref[...] = rope(mm(*_QA)).astype(BF16)
    kv = mm(*_KVA)
    kva_ref[:, 0:256] = rope(kv[:, 0:256]).astype(BF16)
    kva_ref[:, 256:512] = kv[:, 256:512].astype(BF16)
    for ref, (base, end) in ((conv_ref, _CONV), (na_ref, _NA), (gate_ref, _GATE)):
        for lo, hi in _col_chunks(end - base):
            ref[:, lo:hi] = mm(base + lo, base + hi).astype(BF16)


def _inproj(x2d, mods, g_pre, cos_t, sin_t, w_arr, seq_len, tm):
    n_tok = x2d.shape[0]
    tiles_per_seq = seq_len // tm
    if mods.shape[0] == 1:
        mod_map = lambda i: (0, 0, 0)
    else:
        mod_map = lambda i: (i // tiles_per_seq, 0, 0)
    widths = (512, 512, 1536, 1536, 3072)
    return pl.pallas_call(
        _inproj_kernel,
        out_shape=[jax.ShapeDtypeStruct((n_tok, w), BF16) for w in widths],
        grid=(n_tok // tm,),
        in_specs=[
            pl.BlockSpec((tm, D_MODEL), lambda i: (i, 0)),
            pl.BlockSpec((1, 6, D_MODEL), mod_map),
            pl.BlockSpec((1, D_MODEL), lambda i: (0, 0)),
            pl.BlockSpec((tm, LANES), lambda i: (i % tiles_per_seq, 0)),
            pl.BlockSpec((tm, LANES), lambda i: (i % tiles_per_seq, 0)),
            pl.BlockSpec((D_MODEL, IN_COLS_ARRANGED), lambda i: (0, 0)),
        ],
        out_specs=[pl.BlockSpec((tm, w), lambda i: (i, 0)) for w in widths],
        compiler_params=_cparams(1),
        name="in_proj",
    )(x2d, mods, g_pre, cos_t, sin_t, w_arr)


def _pair_stack(q, even):
    zero = jnp.zeros_like(q)
    return jnp.concatenate([jnp.where(even, q, zero), jnp.where(even, zero, q)], axis=0)


def _softmax_pv(score_parts, value_parts, sink_col):
    m = score_parts[0].max(axis=1, keepdims=True)
    for s in score_parts[1:]:
        m = jnp.maximum(m, s.max(axis=1, keepdims=True))
    if sink_col is not None:
        m = jnp.maximum(m, sink_col)
    den = None
    acc = None
    for s, v in zip(score_parts, value_parts):
        e = jnp.exp2(s - m)
        d = e.sum(axis=1, keepdims=True)
        den = d if den is None else den + d
        r = _dot(e.astype(BF16), v)
        acc = r if acc is None else acc + r
    if sink_col is not None:
        den = den + jnp.exp2(sink_col - m)
    return acc / den


def _attend(s_ref, e_ref, values, sink_col):
    s = s_ref[...]
    m = s.max(axis=1, keepdims=True)
    if sink_col is not None:
        m = jnp.maximum(m, sink_col)
    e_ref[...] = jnp.exp2(s - m).astype(e_ref.dtype)
    r = _dot(e_ref[...], jnp.concatenate([values, jnp.ones_like(values)], axis=1))
    den = r[:, LANES:2 * LANES]
    if sink_col is not None:
        den = den + jnp.exp2(sink_col - m)
    return r[:, 0:LANES] / den


def _win_attn_kernel(sink_ref, q_ref, kvp_ref, kvm_ref, kvn_ref, kvc_ref, o_ref, s_ref, e_ref, *, seq_len, tq):
    blk0 = pl.program_id(1) * (tq // WINDOW)
    nw = 3 * WINDOW
    lane = lax.broadcasted_iota(I32, (1, LANES), 1)
    even = lane < HEAD_DIM
    n_stack = 4 * WINDOW
    rows = lax.broadcasted_iota(I32, (n_stack, 1), 0)
    piece = rows // WINDOW
    qi = rows - piece * WINDOW
    kj = lax.broadcasted_iota(I32, (1, nw), 1)
    rel = kj - WINDOW - qi
    in_window = (rel <= WINDOW) & (rel >= -WINDOW)
    for kh in range(A_KV_HEADS):
        kl = slice(kh * LANES, (kh + 1) * LANES)
        vl = slice((A_KV_HEADS + kh) * LANES, (A_KV_HEADS + kh + 1) * LANES)
        kspan = jnp.concatenate([kvp_ref[:, kl], kvm_ref[:, kl], kvn_ref[:, kl]], axis=0)
        vspan = jnp.concatenate([kvp_ref[:, vl], kvm_ref[:, vl], kvn_ref[:, vl]], axis=0)
        kc = kvc_ref[:, kl]
        vc = kvc_ref[:, vl]
        sink_col = jnp.zeros((n_stack, 1), F32)
        for j in range(4):
            sink_col = jnp.where(piece == j, sink_ref[kh * 4 + j], sink_col)
        sink_col = sink_col * LOG2E
        for n in range(tq // WINDOW):
            qb = q_ref[n * WINDOW:(n + 1) * WINDOW, 2 * kh * LANES:2 * (kh + 1) * LANES]
            lq = jnp.concatenate([_pair_stack(qb[:, 0:LANES], even), _pair_stack(qb[:, LANES:2 * LANES], even)],
                                 axis=0)
            kpos = (blk0 + n - 1) * WINDOW + kj
            valid = in_window & (kpos >= 0) & (kpos < seq_len)
            s_ref[:, 0:nw] = jnp.where(valid, _dot_nt(lq, kspan[n * WINDOW:(n + 3) * WINDOW]), NEG_BIG)
            s_ref[:, nw:] = _dot_nt(lq, kc)
            r = _attend(s_ref, e_ref, jnp.concatenate([vspan[n * WINDOW:(n + 3) * WINDOW], vc], axis=0), sink_col)
            rs = slice(n * WINDOW, (n + 1) * WINDOW)
            o_ref[rs, 2 * kh * LANES:(2 * kh + 1) * LANES] = jnp.where(
                even, r[0:WINDOW], r[WINDOW:2 * WINDOW]).astype(BF16)
            o_ref[rs, (2 * kh + 1) * LANES:(2 * kh + 2) * LANES] = jnp.where(
                even, r[2 * WINDOW:3 * WINDOW], r[3 * WINDOW:4 * WINDOW]).astype(BF16)


def _win_attn(sink, qa, kva, kva_c, batch, seq_len, ctx_len):
    tq = min(WIN_Q_TILE, seq_len)
    nblk = seq_len // WINDOW
    per = tq // WINDOW
    nq = seq_len // tq
    kvw = kva.shape[1]
    n_keys = 3 * WINDOW + ctx_len
    kern = functools.partial(_win_attn_kernel, seq_len=seq_len, tq=tq)
    return pl.pallas_call(
        kern,
        out_shape=jax.ShapeDtypeStruct((batch * seq_len, BRANCH_W), BF16),
        grid_spec=pltpu.PrefetchScalarGridSpec(
            num_scalar_prefetch=1,
            grid=(batch, nq),
            in_specs=[
                pl.BlockSpec((tq, BRANCH_W), lambda b, i, s: (b * nq + i, 0)),
                pl.BlockSpec((WINDOW, kvw), lambda b, i, s: (b * nblk + jnp.maximum(i * per - 1, 0), 0)),
                pl.BlockSpec((tq, kvw), lambda b, i, s: (b * nq + i, 0)),
                pl.BlockSpec((WINDOW, kvw), lambda b, i, s: (b * nblk + jnp.minimum((i + 1) * per, nblk - 1), 0)),
                pl.BlockSpec((ctx_len, kvw), lambda b, i, s: (b, 0)),
            ],
            out_specs=pl.BlockSpec((tq, BRANCH_W), lambda b, i, s: (b * nq + i, 0)),
            scratch_shapes=[pltpu.VMEM((4 * WINDOW, n_keys), F32), pltpu.VMEM((4 * WINDOW, n_keys), BF16)],
        ),
        compiler_params=_cparams(2),
        name="win_attn",
    )(sink, qa, kva, kva, kva, kva_c)


def _ctx_attn_kernel(sink_ref, q_ref, k_ref, v_ref, o_ref, *, use_sink):
    p = pl.program_id(1)
    n = q_ref.shape[0]
    lane = lax.broadcasted_iota(I32, (1, LANES), 1)
    even = lane < HEAD_DIM
    lq = _pair_stack(q_ref[...], even)
    s = _dot_nt(lq, k_ref[...])
    sink_col = None
    if use_sink:
        rows = lax.broadcasted_iota(I32, (2 * n, 1), 0)
        sink_col = jnp.where(rows < n, sink_ref[2 * p], sink_ref[2 * p + 1]) * LOG2E
    r = _softmax_pv([s], [v_ref[...]], sink_col)
    o_ref[...] = jnp.where(even, r[0:n], r[n:2 * n]).astype(BF16)


def _ctx_attn(sink, q_arr, q_lane0, kv_arr, k_lane0, v_lane0, kv_shared, batch, ctx_len, use_sink):
    def kv_map(lane0):
        if kv_shared:
            return lambda b, p, s: (b, lane0 + p // 2)
        return lambda b, p, s: (b, lane0 + p)

    kern = functools.partial(_ctx_attn_kernel, use_sink=use_sink)
    return pl.pallas_call(
        kern,
        out_shape=jax.ShapeDtypeStruct((batch * ctx_len, BRANCH_W), BF16),
        grid_spec=pltpu.PrefetchScalarGridSpec(
            num_scalar_prefetch=1,
            grid=(batch, BRANCH_W // LANES),
            in_specs=[
                pl.BlockSpec((ctx_len, LANES), lambda b, p, s: (b, q_lane0 + p)),
                pl.BlockSpec((ctx_len, LANES), kv_map(k_lane0)),
                pl.BlockSpec((ctx_len, LANES), kv_map(v_lane0)),
            ],
            out_specs=pl.BlockSpec((ctx_len, LANES), lambda b, p, s: (b, p)),
        ),
        compiler_params=_cparams(2),
        name="ctx_attn_sink" if use_sink else "ctx_attn",
    )(sink, q_arr, kv_arr, kv_arr)


def _na_kernel(q_ref, kp_ref, km_ref, kn_ref, vp_ref, vm_ref, vn_ref, kc_ref, vc_ref, b0_ref, b1_ref, o_ref,
               s_ref, e_ref):
    gq = NA_ROW_GROUP * GRID_W
    nb = 3 * gq
    lane = lax.broadcasted_iota(I32, (1, LANES), 1)
    even = lane < HEAD_DIM
    for p in range(NA_HEADS // 2):
        pl_ = slice(p * LANES, (p + 1) * LANES)
        kc = kc_ref[:, pl_]
        vc = vc_ref[:, pl_]
        spans = (
            (jnp.concatenate([kp_ref[gq:2 * gq, pl_], km_ref[:, pl_]], axis=0),
             jnp.concatenate([vp_ref[gq:2 * gq, pl_], vm_ref[:, pl_]], axis=0)),
            (jnp.concatenate([km_ref[:, pl_], kn_ref[0:gq, pl_]], axis=0),
             jnp.concatenate([vm_ref[:, pl_], vn_ref[0:gq, pl_]], axis=0)),
        )
        for g, b_ref in enumerate((b0_ref, b1_ref)):
            kspan, vspan = spans[g]
            lq = _pair_stack(q_ref[g * gq:(g + 1) * gq, pl_], even)
            bias = jnp.concatenate([b_ref[0, 2 * p], b_ref[0, 2 * p + 1]], axis=0)
            s_ref[:, 0:nb] = _dot_nt(lq, kspan) + bias
            s_ref[:, nb:] = _dot_nt(lq, kc)
            r = _attend(s_ref, e_ref, jnp.concatenate([vspan, vc], axis=0), None)
            o_ref[g * gq:(g + 1) * gq, pl_] = jnp.where(even, r[0:gq], r[gq:2 * gq]).astype(BF16)


_N_DR = 2 * NA_ROWS_MAX - 1
_N_DC = 2 * NA_COLS - 1


def _na_bias_kernel(rpb_ref, o_ref):
    h = pl.program_id(0)
    qc = lax.broadcasted_iota(I32, (GRID_W, LANES), 0)
    lane = lax.broadcasted_iota(I32, (GRID_W, LANES), 1)
    kc = lane & (GRID_W - 1)
    dc = jnp.clip(kc - qc, -(NA_COLS - 1), NA_COLS - 1) + NA_COLS - 1
    col_start = jnp.clip(qc - NA_COLS // 2, 0, GRID_W - NA_COLS)
    col_ok = (kc >= col_start) & (kc < col_start + NA_COLS)
    neg = jnp.full((GRID_W, LANES), NEG_BIG, F32)
    col_bias = []
    for dr in range(_N_DR):
        c = neg
        for d in range(_N_DC):
            c = jnp.where(dc == d, rpb_ref[(h * _N_DR + dr) * _N_DC + d], c)
        col_bias.append(jnp.where(col_ok, c * LOG2E, neg))
    row_ok = (
        lambda i, j: 4 <= j < 4 + NA_ROWS_MAX,
        lambda i, j: i <= j < i + NA_ROWS_MAX,
        lambda i, j: 0 <= j < NA_ROWS_MAX,
    )
    for v in range(3):
        for i in range(NA_ROW_GROUP):
            for m in range(3 * NA_ROW_GROUP // 2):
                halves = [col_bias[j - i + 3] if row_ok[v](i, j) else neg for j in (2 * m, 2 * m + 1)]
                o_ref[v, 0, i * GRID_W:(i + 1) * GRID_W, m * LANES:(m + 1) * LANES] = jnp.where(
                    lane < GRID_W, halves[0], halves[1])


def _na_bias_tables(rpb):
    shape = (3, NA_HEADS, NA_ROW_GROUP * GRID_W, 3 * NA_ROW_GROUP * GRID_W)
    return pl.pallas_call(
        _na_bias_kernel,
        out_shape=jax.ShapeDtypeStruct(shape, F32),
        grid_spec=pltpu.PrefetchScalarGridSpec(
            num_scalar_prefetch=1,
            grid=(NA_HEADS,),
            in_specs=[],
            out_specs=pl.BlockSpec((3, 1) + shape[2:], lambda h, r: (0, h, 0, 0)),
        ),
        compiler_params=_cparams(1),
        name="na_bias",
    )(rpb.reshape(-1))


def _na_attn(na, na_c, bias_tab, batch, seq_len, ctx_len):
    rows = seq_len // GRID_W
    assert rows % (2 * NA_ROW_GROUP) == 0 and rows >= 4 * NA_ROW_GROUP
    tq = 2 * NA_ROW_GROUP * GRID_W
    nq = seq_len // tq
    gq = NA_ROW_GROUP * GRID_W
    n_keys = 3 * gq + ctx_len

    def tok_map(part, shift):
        return lambda b, i: (b * nq + jnp.clip(i + shift, 0, nq - 1), part)

    bias_block = (1, NA_HEADS, gq, 3 * gq)
    return pl.pallas_call(
        _na_kernel,
        out_shape=jax.ShapeDtypeStruct((batch * seq_len, BRANCH_W), BF16),
        grid=(batch, nq),
        in_specs=[
            pl.BlockSpec((tq, BRANCH_W), tok_map(0, 0)),
            pl.BlockSpec((tq, BRANCH_W), tok_map(1, -1)),
            pl.BlockSpec((tq, BRANCH_W), tok_map(1, 0)),
            pl.BlockSpec((tq, BRANCH_W), tok_map(1, 1)),
            pl.BlockSpec((tq, BRANCH_W), tok_map(2, -1)),
            pl.BlockSpec((tq, BRANCH_W), tok_map(2, 0)),
            pl.BlockSpec((tq, BRANCH_W), tok_map(2, 1)),
            pl.BlockSpec((ctx_len, BRANCH_W), lambda b, i: (b, 1)),
            pl.BlockSpec((ctx_len, BRANCH_W), lambda b, i: (b, 2)),
            pl.BlockSpec(bias_block, lambda b, i: (jnp.where(i == 0, 0, 1), 0, 0, 0), pipeline_mode=pl.Buffered(1)),
            pl.BlockSpec(bias_block, lambda b, i: (jnp.where(i == nq - 1, 2, 1), 0, 0, 0),
                         pipeline_mode=pl.Buffered(1)),
        ],
        out_specs=pl.BlockSpec((tq, BRANCH_W), lambda b, i: (b * nq + i, 0)),
        scratch_shapes=[pltpu.VMEM((2 * gq, n_keys), F32), pltpu.VMEM((2 * gq, n_keys), BF16)],
        compiler_params=_cparams(2),
        name="na_attn",
    )(na, na, na, na, na, na, na, na_c, na_c, bias_tab, bias_tab)


def _merge_kernel(oa_ref, conv_ref, cprev_ref, cnext_ref, oc_ref, gate_ref, x_ref, mod_ref, wb_ref, bg_ref, cw_ref,
                  wo_ref, gpost_ref, out_ref, y_ref, *, tiles_per_seq):
    tm = x_ref.shape[0]
    ti = pl.program_id(0) % tiles_per_seq
    ux = conv_ref[:, 0:512].astype(F32)
    gb = conv_ref[:, 512:1024].astype(F32)
    gc = conv_ref[:, 1024:1536].astype(F32)
    u = gc * ux
    cprev = cprev_ref[...].astype(F32)
    cnext = cnext_ref[...].astype(F32)
    halo = cprev.shape[0]
    u_before = cprev[halo - 1:halo, 1024:1536] * cprev[halo - 1:halo, 0:512]
    u_after = cnext[0:1, 1024:1536] * cnext[0:1, 0:512]
    u_before = jnp.where(ti == 0, 0.0, u_before)
    u_after = jnp.where(ti == tiles_per_seq - 1, 0.0, u_after)
    row = lax.broadcasted_iota(I32, (tm, 1), 0)
    u_dn = jnp.where(row == 0, u_before, pltpu.roll(u, 1, 0))
    u_up = jnp.where(row == tm - 1, u_after, pltpu.roll(u, tm - 1, 0))
    cw = cw_ref[...]
    ob = (gb * (cw[0:1] * u_dn + cw[1:2] * u + cw[2:3] * u_up)).astype(BF16)
    oa = oa_ref[...]
    oc = oc_ref[...]
    for lo, hi in _col_chunks(D_MODEL):
        acc = None
        for r, o in enumerate((oa, ob, oc)):
            gate = jax.nn.sigmoid(gate_ref[:, r * D_MODEL + lo:r * D_MODEL + hi].astype(F32)
                                  + bg_ref[:, r * D_MODEL + lo:r * D_MODEL + hi])
            term = gate * _dot(o, wb_ref[r, :, lo:hi])
            acc = term if acc is None else acc + term
        y_ref[:, lo:hi] = acc.astype(BF16)
    y2 = _dot(y_ref[...], wo_ref[...])
    out_ref[...] = x_ref[...] + mod_ref[0, 2:3, :] * _rms(y2, gpost_ref[...])


def _merge(oa, conv, oc, gate, x2d, mods, wb, b_gate, conv_w, wo, g_post, seq_len, tm):
    n_tok = x2d.shape[0]
    tiles_per_seq = seq_len // tm
    halo = 16
    hb = tm // halo
    n_halo = n_tok // halo
    if mods.shape[0] == 1:
        mod_map = lambda i: (0, 0, 0)
    else:
        mod_map = lambda i: (i // tiles_per_seq, 0, 0)
    kern = functools.partial(_merge_kernel, tiles_per_seq=tiles_per_seq)
    const2 = lambda i: (0, 0)
    return pl.pallas_call(
        kern,
        out_shape=jax.ShapeDtypeStruct((n_tok, D_MODEL), F32),
        grid=(n_tok // tm,),
        in_specs=[
            pl.BlockSpec((tm, BRANCH_W), lambda i: (i, 0)),
            pl.BlockSpec((tm, 3 * BRANCH_W), lambda i: (i, 0)),
            pl.BlockSpec((halo, 3 * BRANCH_W), lambda i: (jnp.maximum(i * hb - 1, 0), 0)),
            pl.BlockSpec((halo, 3 * BRANCH_W), lambda i: (jnp.minimum((i + 1) * hb, n_halo - 1), 0)),
            pl.BlockSpec((tm, BRANCH_W), lambda i: (i, 0)),
            pl.BlockSpec((tm, N_BRANCH * D_MODEL), lambda i: (i, 0)),
            pl.BlockSpec((tm, D_MODEL), lambda i: (i, 0)),
            pl.BlockSpec((1, 6, D_MODEL), mod_map),
            pl.BlockSpec((N_BRANCH, BRANCH_W, D_MODEL), lambda i: (0, 0, 0)),
            pl.BlockSpec((1, N_BRANCH * D_MODEL), const2),
            pl.BlockSpec((CONV_K, BRANCH_W), const2),
            pl.BlockSpec((D_MODEL, D_MODEL), const2),
            pl.BlockSpec((1, D_MODEL), const2),
        ],
        out_specs=pl.BlockSpec((tm, D_MODEL), lambda i: (i, 0)),
        scratch_shapes=[pltpu.VMEM((tm, D_MODEL), BF16)],
        compiler_params=_cparams(1),
        name="branch_merge",
    )(oa, conv, conv, conv, oc, gate, x2d, mods, wb, b_gate, conv_w, wo, g_post)


def _ffn_kernel(x_ref, mod_ref, gpre_ref, wgu_ref, wdn_ref, gpost_ref, out_ref, act_ref):
    x = x_ref[...]
    h = (_rms(x, gpre_ref[...]) * (1.0 + mod_ref[0, 4:5, :]) + mod_ref[0, 3:4, :]).astype(BF16)
    for lo, hi in _col_chunks(FFN_DENSE):
        g = _dot(h, wgu_ref[:, lo:hi])
        u = _dot(h, wgu_ref[:, FFN_DENSE + lo:FFN_DENSE + hi])
        act_ref[:, lo:hi] = (g * jax.nn.sigmoid(g) * u).astype(BF16)
    y = _dot(act_ref[...], wdn_ref[...])
    out_ref[...] = x + mod_ref[0, 5:6, :] * _rms(y, gpost_ref[...])


def _ffn_dense(x2d, mods, g_pre, wgu, wdn, g_post, seq_len, tm):
    n_tok = x2d.shape[0]
    tiles_per_seq = seq_len // tm
    if mods.shape[0] == 1:
        mod_map = lambda i: (0, 0, 0)
    else:
        mod_map = lambda i: (i // tiles_per_seq, 0, 0)
    const2 = lambda i: (0, 0)
    return pl.pallas_call(
        _ffn_kernel,
        out_shape=jax.ShapeDtypeStruct((n_tok, D_MODEL), F32),
        grid=(n_tok // tm,),
        in_specs=[
            pl.BlockSpec((tm, D_MODEL), lambda i: (i, 0)),
            pl.BlockSpec((1, 6, D_MODEL), mod_map),
            pl.BlockSpec((1, D_MODEL), const2),
            pl.BlockSpec((D_MODEL, 2 * FFN_DENSE), const2),
            pl.BlockSpec((FFN_DENSE, D_MODEL), const2),
            pl.BlockSpec((1, D_MODEL), const2),
        ],
        out_specs=pl.BlockSpec((tm, D_MODEL), lambda i: (i, 0)),
        scratch_shapes=[pltpu.VMEM((tm, FFN_DENSE), BF16)],
        compiler_params=_cparams(1),
        name="ffn_dense",
    )(x2d, mods, g_pre, wgu, wdn, g_post)


def _to_token_tiles(a):
    slabs = jnp.stack([a[:, s * LANES:(s + 1) * LANES] for s in range(D_MODEL // LANES)], axis=0)
    return pltpu.einshape("stl->tsl", slabs)


def _from_token_tiles(t):
    slabs = pltpu.einshape("tsl->stl", t)
    return jnp.concatenate([slabs[s] for s in range(D_MODEL // LANES)], axis=1)


TOKEN_TILE_SHAPE = (D_MODEL // LANES, LANES)


def _router_kernel(x_ref, mod_ref, gpre_ref, wr_ref, br_ref, h_ref, e_ref, gt_ref):
    x = x_ref[...]
    h = _rms(x, gpre_ref[...]) * (1.0 + mod_ref[0, 4:5, :]) + mod_ref[0, 3:4, :]
    h_ref[...] = _to_token_tiles(h)
    logits = lax.dot_general(wr_ref[...], h, (((1,), (1,)), ((), ())), preferred_element_type=F32,
                             precision=lax.Precision.HIGHEST) + br_ref[...]
    eid = lax.broadcasted_iota(I32, logits.shape, 0)
    m1 = logits.max(axis=0, keepdims=True)
    i1 = jnp.min(jnp.where(logits == m1, eid, N_EXPERTS), axis=0, keepdims=True)
    rest = jnp.where(eid == i1, -jnp.inf, logits)
    m2 = rest.max(axis=0, keepdims=True)
    i2 = jnp.min(jnp.where(rest == m2, eid, N_EXPERTS), axis=0, keepdims=True)
    e2 = jnp.exp(m2 - m1)
    den = 1.0 + e2
    e_ref[...] = jnp.concatenate([i1, i2], axis=0)
    gt_ref[...] = jnp.concatenate([1.0 / den, e2 / den], axis=0)


def _router(x2d, mods, g_pre, w_router_t, b_router, seq_len, tm):
    n_tok = x2d.shape[0]
    tiles_per_seq = seq_len // tm
    const2 = lambda i: (0, 0)
    return pl.pallas_call(
        _router_kernel,
        out_shape=[
            jax.ShapeDtypeStruct((n_tok,) + TOKEN_TILE_SHAPE, F32),
            jax.ShapeDtypeStruct((TOP_K, n_tok), I32),
            jax.ShapeDtypeStruct((TOP_K, n_tok), F32),
        ],
        grid=(n_tok // tm,),
        in_specs=[
            pl.BlockSpec((tm, D_MODEL), lambda i: (i, 0)),
            pl.BlockSpec((1, 6, D_MODEL), lambda i: (i // tiles_per_seq, 0, 0)),
            pl.BlockSpec((1, D_MODEL), const2),
            pl.BlockSpec((N_EXPERTS, D_MODEL), const2),
            pl.BlockSpec((N_EXPERTS, 1), const2),
        ],
        out_specs=[
            pl.BlockSpec((tm,) + TOKEN_TILE_SHAPE, lambda i: (i, 0, 0)),
            pl.BlockSpec((TOP_K, tm), lambda i: (0, i)),
            pl.BlockSpec((TOP_K, tm), lambda i: (0, i)),
        ],
        compiler_params=_cparams(1),
        name="moe_router",
    )(x2d, mods, g_pre, w_router_t, b_router)


def _rank_kernel(e_ref, rank_ref, cnt_ref, carry_ref):
    tm = e_ref.shape[1]

    @pl.when(pl.program_id(0) == 0)
    def _():
        carry_ref[...] = jnp.zeros_like(carry_ref)

    e = e_ref[...]
    eid = lax.broadcasted_iota(I32, (N_EXPERTS, tm), 0)
    oh0 = eid == e[0:1, :]
    oh1 = eid == e[1:2, :]
    oh = jnp.concatenate([oh0, oh1], axis=0).astype(F32)
    tri = (lax.broadcasted_iota(I32, (tm, tm), 0) <= lax.broadcasted_iota(I32, (tm, tm), 1)).astype(BF16)
    incl = _dot(oh.astype(BF16), tri)
    excl = incl - oh
    tot = incl[:, tm - 1:tm]
    carry = carry_ref[:, 0:1]
    rank0 = carry + excl[0:N_EXPERTS]
    rank1 = carry + tot[0:N_EXPERTS] + excl[N_EXPERTS:2 * N_EXPERTS]
    r0 = jnp.sum(jnp.where(oh0, rank0, 0.0), axis=0, keepdims=True)
    r1 = jnp.sum(jnp.where(oh1, rank1, 0.0), axis=0, keepdims=True)
    rank_ref[...] = jnp.concatenate([r0, r1], axis=0).astype(I32)
    new_carry = carry + tot[0:N_EXPERTS] + tot[N_EXPERTS:2 * N_EXPERTS]
    carry_ref[...] = jnp.broadcast_to(new_carry, carry_ref.shape)
    cnt_ref[...] = jnp.broadcast_to(new_carry, cnt_ref.shape).astype(I32)


def _ranks(e_idx, tm):
    n_tok = e_idx.shape[1]
    return pl.pallas_call(
        _rank_kernel,
        out_shape=[
            jax.ShapeDtypeStruct((TOP_K, n_tok), I32),
            jax.ShapeDtypeStruct((N_EXPERTS, LANES), I32),
        ],
        grid=(n_tok // tm,),
        in_specs=[pl.BlockSpec((TOP_K, tm), lambda i: (0, i))],
        out_specs=[
            pl.BlockSpec((TOP_K, tm), lambda i: (0, i)),
            pl.BlockSpec((N_EXPERTS, LANES), lambda i: (0, 0)),
        ],
        scratch_shapes=[pltpu.VMEM((N_EXPERTS, LANES), F32)],
        compiler_params=_cparams(1),
        name="moe_rank",
    )(e_idx)


def _dispatch_kernel(dest_ref, pad_ref, h_ref, xs_ref, zero_ref, sem, zsem):
    tm = h_ref.shape[0]
    n_tok = dest_ref.shape[0] // TOP_K
    base = pl.program_id(0) * tm

    @pl.when(pl.program_id(0) == 0)
    def _():
        zero_ref[...] = jnp.zeros_like(zero_ref)

        zrows = zero_ref.shape[0]

        def zero_copy(r):
            return pltpu.make_async_copy(zero_ref.at[0], xs_ref.at[r], zsem)

        def zero_chunk(j):
            return pltpu.make_async_copy(zero_ref, xs_ref.at[pl.ds(pl.multiple_of(j * zrows, zrows), zrows)], zsem)

        tail = (pad_ref[2 * N_EXPERTS] // zrows, xs_ref.shape[0] // zrows)
        for e in range(N_EXPERTS):
            lo, hi = pad_ref[e], pad_ref[N_EXPERTS + e]
            lax.fori_loop(lo, hi, lambda r, c: (zero_copy(r).start(), c)[1], 0)
        lax.fori_loop(tail[0], tail[1], lambda j, c: (zero_chunk(j).start(), c)[1], 0)
        for e in range(N_EXPERTS):
            lo, hi = pad_ref[e], pad_ref[N_EXPERTS + e]
            lax.fori_loop(lo, hi, lambda r, c: (zero_copy(r).wait(), c)[1], 0)
        lax.fori_loop(tail[0], tail[1], lambda j, c: (zero_chunk(j).wait(), c)[1], 0)

    def start(t, carry):
        for k in range(TOP_K):
            d = dest_ref[k * n_tok + base + t]
            pltpu.make_async_copy(h_ref.at[t], xs_ref.at[d], sem).start(priority=k)
        return carry

    lax.fori_loop(0, tm, start, 0, unroll=ISSUE_UNROLL)
    for k in range(TOP_K):
        pltpu.make_async_copy(h_ref, xs_ref.at[pl.ds(0, tm)], sem).wait()


def _dispatch(dest_flat, pad_rows, h, n_rows, tm):
    n_tok = h.shape[0]
    return pl.pallas_call(
        _dispatch_kernel,
        out_shape=jax.ShapeDtypeStruct((n_rows,) + TOKEN_TILE_SHAPE, F32),
        grid_spec=pltpu.PrefetchScalarGridSpec(
            num_scalar_prefetch=2,
            grid=(n_tok // tm,),
            in_specs=[pl.BlockSpec((tm,) + TOKEN_TILE_SHAPE, lambda i, d, p: (i, 0, 0))],
            out_specs=pl.BlockSpec(memory_space=pl.ANY),
            scratch_shapes=[
                pltpu.VMEM((64,) + TOKEN_TILE_SHAPE, F32),
                pltpu.SemaphoreType.DMA,
                pltpu.SemaphoreType.DMA,
            ],
        ),
        compiler_params=_cparams(1),
        name="moe_dispatch",
    )(dest_flat, pad_rows, h)


def _expert_kernel(be_ref, nb_ref, xs_ref, wgu_ref, wdn_ref, y_ref, act_ref):
    @pl.when(pl.program_id(0) < nb_ref[0])
    def _():
        xb = _from_token_tiles(xs_ref[...]).astype(BF16)
        for lo, hi in _col_chunks(FFN_EXPERT):
            g = _dot(xb, wgu_ref[0, :, lo:hi])
            u = _dot(xb, wgu_ref[0, :, FFN_EXPERT + lo:FFN_EXPERT + hi])
            act_ref[:, lo:hi] = (g * jax.nn.sigmoid(g) * u).astype(BF16)
        y_ref[...] = _to_token_tiles(_dot(act_ref[...], wdn_ref[0]))

    @pl.when(pl.program_id(0) >= nb_ref[0])
    def _():
        y_ref[...] = jnp.zeros_like(y_ref)


def _expert_ffn(block_e, n_used, xs, wgu, wdn, bm):
    n_rows = xs.shape[0]
    n_blocks = n_rows // bm

    def blk(j, be, nb):
        return jnp.minimum(j, nb[0] - 1)

    return pl.pallas_call(
        _expert_kernel,
        out_shape=jax.ShapeDtypeStruct((n_rows,) + TOKEN_TILE_SHAPE, F32),
        grid_spec=pltpu.PrefetchScalarGridSpec(
            num_scalar_prefetch=2,
            grid=(n_blocks,),
            in_specs=[
                pl.BlockSpec((bm,) + TOKEN_TILE_SHAPE, lambda j, be, nb: (blk(j, be, nb), 0, 0)),
                pl.BlockSpec((1, D_MODEL, 2 * FFN_EXPERT), lambda j, be, nb: (be[blk(j, be, nb)], 0, 0)),
                pl.BlockSpec((1, FFN_EXPERT, D_MODEL), lambda j, be, nb: (be[blk(j, be, nb)], 0, 0),
                             pipeline_mode=pl.Buffered(1)),
            ],
            out_specs=pl.BlockSpec((bm,) + TOKEN_TILE_SHAPE, lambda j, be, nb: (j, 0, 0)),
            scratch_shapes=[pltpu.VMEM((bm, FFN_EXPERT), BF16)],
        ),
        compiler_params=_cparams(1),
        name="moe_experts",
    )(block_e, n_used, xs, wgu, wdn)


def _combine_kernel(dest_ref, ys_ref, gt_ref, x_ref, mod_ref, gpost_ref, out_ref, buf_ref, sem):
    tm = x_ref.shape[0]
    n_tok = dest_ref.shape[0] // TOP_K
    i = pl.program_id(0)
    slot = lax.rem(i, 2)

    def issue(step, into):
        base = step * tm

        def start(t, carry):
            for k in range(TOP_K):
                d = dest_ref[k * n_tok + base + t]
                pltpu.make_async_copy(ys_ref.at[d], buf_ref.at[into, k, t], sem.at[into]).start(priority=k)
            return carry

        lax.fori_loop(0, tm, start, 0, unroll=ISSUE_UNROLL)

    @pl.when(i == 0)
    def _():
        issue(0, 0)

    @pl.when(i + 1 < pl.num_programs(0))
    def _():
        issue(i + 1, 1 - slot)

    for k in range(TOP_K):
        pltpu.make_async_copy(ys_ref.at[pl.ds(0, tm)], buf_ref.at[slot, k], sem.at[slot]).wait()
    gt = gt_ref[...]
    y = gt[:, 0:1] * _from_token_tiles(buf_ref[slot, 0]) + gt[:, 1:2] * _from_token_tiles(buf_ref[slot, 1])
    out_ref[...] = x_ref[...] + mod_ref[0, 5:6, :] * _rms(y, gpost_ref[...])


def _combine(dest_flat, ys, gates_t, x2d, mods, g_post, seq_len, tm):
    n_tok = x2d.shape[0]
    tiles_per_seq = seq_len // tm
    return pl.pallas_call(
        _combine_kernel,
        out_shape=jax.ShapeDtypeStruct((n_tok, D_MODEL), F32),
        grid_spec=pltpu.PrefetchScalarGridSpec(
            num_scalar_prefetch=1,
            grid=(n_tok // tm,),
            in_specs=[
                pl.BlockSpec(memory_space=pl.ANY),
                pl.BlockSpec((tm, TOP_K), lambda i, d: (i, 0)),
                pl.BlockSpec((tm, D_MODEL), lambda i, d: (i, 0)),
                pl.BlockSpec((1, 6, D_MODEL), lambda i, d: (i // tiles_per_seq, 0, 0)),
                pl.BlockSpec((1, D_MODEL), lambda i, d: (0, 0)),
            ],
            out_specs=pl.BlockSpec((tm, D_MODEL), lambda i, d: (i, 0)),
            scratch_shapes=[
                pltpu.VMEM((2, TOP_K, tm) + TOKEN_TILE_SHAPE, F32),
                pltpu.SemaphoreType.DMA((2,)),
            ],
        ),
        compiler_params=_cparams(1),
        name="moe_combine",
    )(dest_flat, ys, gates_t, x2d, mods, g_post)


def _moe(x2d, mods, g_pre, g_post, w_router, b_router, wgu, wdn, seq_len, tm):
    n_tok = x2d.shape[0]
    bm = MOE_BLOCK_ROWS
    h, e_idx, gates = _router(x2d, mods, g_pre, w_router.T, b_router.reshape(N_EXPERTS, 1), seq_len, tm)
    rank, cnt = _ranks(e_idx, tm)
    counts = cnt[:, 0]
    padded = (counts + bm - 1) // bm * bm
    ends = jnp.cumsum(padded)
    pstart = ends - padded
    dest = rank
    for e in range(N_EXPERTS):
        dest = dest + jnp.where(e_idx == e, pstart[e], 0)
    dest_flat = dest.reshape(-1).astype(I32)
    n_blocks = n_tok * TOP_K // bm + N_EXPERTS
    blk_start = jnp.arange(n_blocks, dtype=I32) * bm
    block_e = jnp.minimum(jnp.sum(blk_start[:, None] >= ends[None, :], axis=1), N_EXPERTS - 1).astype(I32)
    n_used = (ends[-1:] // bm).astype(I32)
    pad_rows = jnp.concatenate([pstart + counts, ends, ends[-1:]]).astype(I32)
    xs = _dispatch(dest_flat, pad_rows, h, n_blocks * bm, tm)
    ys = _expert_ffn(block_e, n_used, xs, wgu, wdn, bm)
    return _combine(dest_flat, ys, gates.T, x2d, mods, g_post, seq_len, tm)


def _rope_tables(n_tokens):
    pos = jnp.arange(n_tokens)
    row = (pos // GRID_W).astype(F32)
    col = (pos % GRID_W).astype(F32)
    half = HEAD_DIM // 2
    inv = 1.0 / (ROPE_BASE ** (jnp.arange(0, half, 2, dtype=F32) / half))
    ar = row[:, None] * inv[None]
    ac = col[:, None] * inv[None]
    ang = jnp.concatenate([ar, ar, ac, ac], axis=-1)
    ang = jnp.concatenate([ang, ang], axis=-1)
    sign = jnp.where((jnp.arange(LANES) & 16) == 0, -1.0, 1.0).astype(F32)
    return jnp.cos(ang), jnp.sin(ang) * sign[None]


def _arrange_w_in(w):
    scale = HEAD_DIM ** -0.5 * LOG2E
    offs = np.cumsum([0, BRANCH_W, 128, 128, BRANCH_W, BRANCH_W, BRANCH_W, BRANCH_W, BRANCH_W, BRANCH_W, N_BRANCH * D_MODEL])
    parts = [w[:, offs[i]:offs[i + 1]] for i in range(10)]
    qa, ka, va, ux, gb, gc, qn, kn, vn, gx = parts

    def dup(t):
        return jnp.concatenate([t[:, 0:64], t[:, 0:64], t[:, 64:128], t[:, 64:128]], axis=1)

    return jnp.concatenate([qa * scale, dup(ka), dup(va), ux, gb, gc, qn * scale, kn, vn, gx], axis=1).astype(BF16)


def kernel(x, c, ctx, c_ctx, w_ada, b_ada, g_pre_mix, g_post_mix, g_pre_ffn, g_post_ffn, w_in, b_gate, sink, conv_w,
           rpb, w_branch, w_out, w_gu_dense, w_dn_dense, w_router, b_router, w_gu_moe, w_dn_moe):
    batch, seq_len, d = x.shape
    ctx_len = ctx.shape[1]
    tm = min(TOKEN_TILE, seq_len)
    tmc = min(TOKEN_TILE, ctx_len)

    n_vec = batch + 1
    n_vec_pad = -(-n_vec // 8) * 8
    cvecs = jnp.concatenate([c, c_ctx[None], jnp.zeros((n_vec_pad - n_vec, d), F32)], axis=0)
    mods = _ada_mods(cvecs, w_ada, b_ada)

    cos_t, sin_t = _rope_tables(seq_len)
    cos_c = jnp.ones((ctx_len, LANES), F32)
    sin_c = jnp.zeros((ctx_len, LANES), F32)

    x2 = x.reshape(batch * seq_len, d)
    hc2 = ctx.reshape(batch * ctx_len, d)
    for l in range(DEPTH):
        update_ctx = l < DEPTH - 1
        mods_x = mods[l, :batch]
        mods_c = mods[l, batch:batch + 1]
        w_arr = _arrange_w_in(w_in[l])
        wb = w_branch[l].astype(BF16)
        wo = w_out[l].astype(BF16)
        bg = b_gate[l].reshape(1, N_BRANCH * D_MODEL)
        g1 = g_pre_mix[l].reshape(1, d)
        g2 = g_post_mix[l].reshape(1, d)
        g3 = g_pre_ffn[l].reshape(1, d)
        g4 = g_post_ffn[l].reshape(1, d)

        qa, kva, conv, na, gate = _inproj(x2, mods_x, g1, cos_t, sin_t, w_arr, seq_len, tm)
        qa_c, kva_c, conv_c, na_c, gate_c = _inproj(hc2, mods_c, g1, cos_c, sin_c, w_arr, ctx_len, tmc)

        o_a = _win_attn(sink[l], qa, kva, kva_c, batch, seq_len, ctx_len)
        o_c = _na_attn(na, na_c, _na_bias_tables(rpb[l]), batch, seq_len, ctx_len)
        x2 = _merge(o_a, conv, o_c, gate, x2, mods_x, wb, bg, conv_w[l], wo, g2, seq_len, tm)

        if l % 2 == 0:
            wgu = w_gu_dense[l // 2].astype(BF16)
            wdn = w_dn_dense[l // 2].astype(BF16)
            x2 = _ffn_dense(x2, mods_x, g3, wgu, wdn, g4, seq_len, tm)
        else:
            x2 = _moe(x2, mods_x, g3, g4, w_router[l // 2], b_router[l // 2], w_gu_moe[l // 2].astype(BF16),
                      w_dn_moe[l // 2].astype(BF16), seq_len, tm)

        if update_ctx:
            o_a_c = _ctx_attn(sink[l], qa_c, 0, kva_c, 0, 2, True, batch, ctx_len, True)
            o_c_c = _ctx_attn(sink[l], na_c, 0, na_c, 4, 8, False, batch, ctx_len, False)
            hc2 = _merge(o_a_c, conv_c, o_c_c, gate_c, hc2, mods_c, wb, bg, conv_w[l], wo, g2, ctx_len, tmc)
            if l % 2 == 0:
                hc2 = _ffn_dense(hc2, mods_c, g3, wgu, wdn, g4, ctx_len, tmc)
            else:
                hc2 = _moe(hc2, jnp.broadcast_to(mods_c, (batch,) + mods_c.shape[1:]), g3, g4, w_router[l // 2],
                           b_router[l // 2], w_gu_moe[l // 2].astype(BF16), w_dn_moe[l // 2].astype(BF16),
                           ctx_len, tmc)
    return x2.reshape(batch, seq_len, d)
```

```python
import functools

import numpy as np
import jax
import jax.numpy as jnp
from jax import lax
from jax.experimental import pallas as pl
from jax.experimental.pallas import tpu as pltpu

F32 = jnp.float32
BF16 = jnp.bfloat16
I32 = jnp.int32

D_MODEL = 1024
DEPTH = 2
GRID_W = 64
HEAD_DIM = 64
BRANCH_W = 512
N_BRANCH = 3
A_HEADS = 8
A_KV_HEADS = 2
WINDOW = 128
CONV_K = 3
NA_HEADS = 8
NA_ROWS_MAX = 8
NA_COLS = 16
ROPE_BASE = 10000.0
FFN_DENSE = 2816
N_EXPERTS = 8
TOP_K = 2
FFN_EXPERT = 3584
RMS_EPS = 1e-6

LANES = 128
MXU_COLS = 256
VMEM_LIMIT_BYTES = 56 * 1024 * 1024

NEG_BIG = -1e30
LOG2E = float(np.log2(np.e))
TOKEN_TILE = 512
WIN_Q_TILE = 512
NA_ROW_GROUP = 4
MOE_BLOCK_ROWS = 512
ISSUE_UNROLL = 8
DISPATCH_TILE = 2048

_QA = (0, 512)
_KVA = (512, 1024)
_CONV = (1024, 2560)
_NA = (2560, 4096)
_GATE = (4096, 7168)
IN_COLS_ARRANGED = 7168


def _cparams(n_grid_dims, vmem=VMEM_LIMIT_BYTES):
    return pltpu.CompilerParams(dimension_semantics=("arbitrary",) * n_grid_dims, vmem_limit_bytes=vmem)


def _dot(a, b):
    return jnp.dot(a, b, preferred_element_type=F32)


def _dot_nt(a, b):
    return lax.dot_general(a, b, (((1,), (1,)), ((), ())), preferred_element_type=F32)


def _rms(x, g):
    return x * lax.rsqrt(jnp.mean(x * x, axis=-1, keepdims=True) + RMS_EPS) * g


def _col_chunks(n, step=512):
    return [(lo, min(lo + step, n)) for lo in range(0, n, step)]


def _ada_kernel(c_ref, w_ref, b_ref, o_ref):
    cv = c_ref[...]
    a = cv * jax.nn.sigmoid(cv)
    o_ref[0] = jnp.dot(a, w_ref[0], preferred_element_type=F32, precision=lax.Precision.HIGHEST) + b_ref[0]


def _ada_mods(cvecs, w_ada, b_ada):
    n_rows = cvecs.shape[0]
    tn = 1536
    out = pl.pallas_call(
        _ada_kernel,
        out_shape=jax.ShapeDtypeStruct((DEPTH, n_rows, 6 * D_MODEL), F32),
        grid=(DEPTH, 6 * D_MODEL // tn),
        in_specs=[
            pl.BlockSpec((n_rows, D_MODEL), lambda l, j: (0, 0)),
            pl.BlockSpec((1, D_MODEL, tn), lambda l, j: (l, 0, j)),
            pl.BlockSpec((1, 1, tn), lambda l, j: (l, 0, j)),
        ],
        out_specs=pl.BlockSpec((1, n_rows, tn), lambda l, j: (l, 0, j)),
        compiler_params=_cparams(2),
        name="ada_mod",
    )(cvecs, w_ada, b_ada.reshape(DEPTH, 1, 6 * D_MODEL))
    return out.reshape(DEPTH, n_rows, 6, D_MODEL)


def _inproj_kernel(x_ref, mod_ref, g_ref, cos_ref, sin_ref, w_ref, qa_ref, kva_ref, conv_ref, na_ref, gate_ref):
    x = x_ref[...]
    h = (_rms(x, g_ref[...]) * (1.0 + mod_ref[0, 1:2, :]) + mod_ref[0, 0:1, :]).astype(BF16)
    cos = cos_ref[...]
    sin = sin_ref[...]
    lane = lax.broadcasted_iota(I32, (1, LANES), 1)
    first_quarter = (lane & 16) == 0

    def mm(lo, hi):
        return _dot(h, w_ref[:, lo:hi])

    def rope(a):
        outs = []
        for j in range(a.shape[1] // LANES):
            c = a[:, j * LANES:(j + 1) * LANES]
            up = pltpu.roll(c, LANES - 16, 1)
            dn = pltpu.roll(c, 16, 1)
            outs.append(c * cos + jnp.where(first_quarter, up, dn) * sin)
        return jnp.concatenate(outs, axis=1)

    qa_ref[...] = rope(mm(*_QA)).astype(BF16)
    kv = mm(*_KVA)
    kva_ref[:, 0:256] = rope(kv[:, 0:256]).astype(BF16)
    kva_ref[:, 256:512] = kv[:, 256:512].astype(BF16)
    for ref, (base, end) in ((conv_ref, _CONV), (na_ref, _NA), (gate_ref, _GATE)):
        for lo, hi in _col_chunks(end - base):
            ref[:, lo:hi] = mm(base + lo, base + hi).astype(BF16)


def _inproj(x2d, mods, g_pre, cos_t, sin_t, w_arr, seq_len, tm):
    n_tok = x2d.shape[0]
    tiles_per_seq = seq_len // tm
    if mods.shape[0] == 1:
        mod_map = lambda i: (0, 0, 0)
    else:
        mod_map = lambda i: (i // tiles_per_seq, 0, 0)
    widths = (512, 512, 1536, 1536, 3072)
    return pl.pallas_call(
        _inproj_kernel,
        out_shape=[jax.ShapeDtypeStruct((n_tok, w), BF16) for w in widths],
        grid=(n_tok // tm,),
        in_specs=[
            pl.BlockSpec((tm, D_MODEL), lambda i: (i, 0)),
            pl.BlockSpec((1, 6, D_MODEL), mod_map),
            pl.BlockSpec((1, D_MODEL), lambda i: (0, 0)),
            pl.BlockSpec((tm, LANES), lambda i: (i % tiles_per_seq, 0)),
            pl.BlockSpec((tm, LANES), lambda i: (i % tiles_per_seq, 0)),
            pl.BlockSpec((D_MODEL, IN_COLS_ARRANGED), lambda i: (0, 0)),
        ],
        out_specs=[pl.BlockSpec((tm, w), lambda i: (i, 0)) for w in widths],
        compiler_params=_cparams(1),
        name="in_proj",
    )(x2d, mods, g_pre, cos_t, sin_t, w_arr)


def _pair_stack(q, even):
    zero = jnp.zeros_like(q)
    return jnp.concatenate([jnp.where(even, q, zero), jnp.where(even, zero, q)], axis=0)


def _softmax_pv(score_parts, value_parts, sink_col):
    m = score_parts[0].max(axis=1, keepdims=True)
    for s in score_parts[1:]:
        m = jnp.maximum(m, s.max(axis=1, keepdims=True))
    if sink_col is not None:
        m = jnp.maximum(m, sink_col)
    den = None
    acc = None
    for s, v in zip(score_parts, value_parts):
        e = jnp.exp2(s - m)
        d = e.sum(axis=1, keepdims=True)
        den = d if den is None else den + d
        r = _dot(e.astype(BF16), v)
        acc = r if acc is None else acc + r
    if sink_col is not None:
        den = den + jnp.exp2(sink_col - m)
    return acc / den


def _attend(s_ref, e_ref, values, sink_col):
    s = s_ref[...]
    m = s.max(axis=1, keepdims=True)
    if sink_col is not None:
        m = jnp.maximum(m, sink_col)
    e_ref[...] = jnp.exp2(s - m).astype(e_ref.dtype)
    r = _dot(e_ref[...], jnp.concatenate([values, jnp.ones_like(values)], axis=1))
    den = r[:, LANES:2 * LANES]
    if sink_col is not None:
        den = den + jnp.exp2(sink_col - m)
    return r[:, 0:LANES] / den


def _win_attn_kernel(sink_ref, q_ref, kvp_ref, kvm_ref, kvn_ref, kvc_ref, o_ref, s_ref, e_ref, *, seq_len, tq):
    blk0 = pl.program_id(1) * (tq // WINDOW)
    nw = 3 * WINDOW
    lane = lax.broadcasted_iota(I32, (1, LANES), 1)
    even = lane < HEAD_DIM
    n_stack = 4 * WINDOW
    rows = lax.broadcasted_iota(I32, (n_stack, 1), 0)
    piece = rows // WINDOW
    qi = rows - piece * WINDOW
    kj = lax.broadcasted_iota(I32, (1, nw), 1)
    rel = kj - WINDOW - qi
    in_window = (rel <= WINDOW) & (rel >= -WINDOW)
    for kh in range(A_KV_HEADS):
        kl = slice(kh * LANES, (kh + 1) * LANES)
        vl = slice((A_KV_HEADS + kh) * LANES, (A_KV_HEADS + kh + 1) * LANES)
        kspan = jnp.concatenate([kvp_ref[:, kl], kvm_ref[:, kl], kvn_ref[:, kl]], axis=0)
        vspan = jnp.concatenate([kvp_ref[:, vl], kvm_ref[:, vl], kvn_ref[:, vl]], axis=0)
        kc = kvc_ref[:, kl]
        vc = kvc_ref[:, vl]
        sink_col = jnp.zeros((n_stack, 1), F32)
        for j in range(4):
            sink_col = jnp.where(piece == j, sink_ref[kh * 4 + j], sink_col)
        sink_col = sink_col * LOG2E
        for n in range(tq // WINDOW):
            qb = q_ref[n * WINDOW:(n + 1) * WINDOW, 2 * kh * LANES:2 * (kh + 1) * LANES]
            lq = jnp.concatenate([_pair_stack(qb[:, 0:LANES], even), _pair_stack(qb[:, LANES:2 * LANES], even)],
                                 axis=0)
            kpos = (blk0 + n - 1) * WINDOW + kj
            valid = in_window & (kpos >= 0) & (kpos < seq_len)
            s_ref[:, 0:nw] = jnp.where(valid, _dot_nt(lq, kspan[n * WINDOW:(n + 3) * WINDOW]), NEG_BIG)
            s_ref[:, nw:] = _dot_nt(lq, kc)
            r = _attend(s_ref, e_ref, jnp.concatenate([vspan[n * WINDOW:(n + 3) * WINDOW], vc], axis=0), sink_col)
            rs = slice(n * WINDOW, (n + 1) * WINDOW)
            o_ref[rs, 2 * kh * LANES:(2 * kh + 1) * LANES] = jnp.where(
                even, r[0:WINDOW], r[WINDOW:2 * WINDOW]).astype(BF16)
            o_ref[rs, (2 * kh + 1) * LANES:(2 * kh + 2) * LANES] = jnp.where(
                even, r[2 * WINDOW:3 * WINDOW], r[3 * WINDOW:4 * WINDOW]).astype(BF16)


def _win_attn(sink, qa, kva, kva_c, batch, seq_len, ctx_len):
    tq = min(WIN_Q_TILE, seq_len)
    nblk = seq_len // WINDOW
    per = tq // WINDOW
    nq = seq_len // tq
    kvw = kva.shape[1]
    n_keys = 3 * WINDOW + ctx_len
    kern = functools.partial(_win_attn_kernel, seq_len=seq_len, tq=tq)
    return pl.pallas_call(
        kern,
        out_shape=jax.ShapeDtypeStruct((batch * seq_len, BRANCH_W), BF16),
        grid_spec=pltpu.PrefetchScalarGridSpec(
            num_scalar_prefetch=1,
            grid=(batch, nq),
            in_specs=[
                pl.BlockSpec((tq, BRANCH_W), lambda b, i, s: (b * nq + i, 0)),
                pl.BlockSpec((WINDOW, kvw), lambda b, i, s: (b * nblk + jnp.maximum(i * per - 1, 0), 0)),
                pl.BlockSpec((tq, kvw), lambda b, i, s: (b * nq + i, 0)),
                pl.BlockSpec((WINDOW, kvw), lambda b, i, s: (b * nblk + jnp.minimum((i + 1) * per, nblk - 1), 0)),
                pl.BlockSpec((ctx_len, kvw), lambda b, i, s: (b, 0)),
            ],
            out_specs=pl.BlockSpec((tq, BRANCH_W), lambda b, i, s: (b * nq + i, 0)),
            scratch_shapes=[pltpu.VMEM((4 * WINDOW, n_keys), F32), pltpu.VMEM((4 * WINDOW, n_keys), BF16)],
        ),
        compiler_params=_cparams(2),
        name="win_attn",
    )(sink, qa, kva, kva, kva, kva_c)


def _ctx_attn_kernel(sink_ref, q_ref, k_ref, v_ref, o_ref, *, use_sink):
    p = pl.program_id(1)
    n = q_ref.shape[0]
    lane = lax.broadcasted_iota(I32, (1, LANES), 1)
    even = lane < HEAD_DIM
    lq = _pair_stack(q_ref[...], even)
    s = _dot_nt(lq, k_ref[...])
    sink_col = None
    if use_sink:
        rows = lax.broadcasted_iota(I32, (2 * n, 1), 0)
        sink_col = jnp.where(rows < n, sink_ref[2 * p], sink_ref[2 * p + 1]) * LOG2E
    r = _softmax_pv([s], [v_ref[...]], sink_col)
    o_ref[...] = jnp.where(even, r[0:n], r[n:2 * n]).astype(BF16)


def _ctx_attn(sink, q_arr, q_lane0, kv_arr, k_lane0, v_lane0, kv_shared, batch, ctx_len, use_sink):
    def kv_map(lane0):
        if kv_shared:
            return lambda b, p, s: (b, lane0 + p // 2)
        return lambda b, p, s: (b, lane0 + p)

    kern = functools.partial(_ctx_attn_kernel, use_sink=use_sink)
    return pl.pallas_call(
        kern,
        out_shape=jax.ShapeDtypeStruct((batch * ctx_len, BRANCH_W), BF16),
        grid_spec=pltpu.PrefetchScalarGridSpec(
            num_scalar_prefetch=1,
            grid=(batch, BRANCH_W // LANES),
            in_specs=[
                pl.BlockSpec((ctx_len, LANES), lambda b, p, s: (b, q_lane0 + p)),
                pl.BlockSpec((ctx_len, LANES), kv_map(k_lane0)),
                pl.BlockSpec((ctx_len, LANES), kv_map(v_lane0)),
            ],
            out_specs=pl.BlockSpec((ctx_len, LANES), lambda b, p, s: (b, p)),
        ),
        compiler_params=_cparams(2),
        name="ctx_attn_sink" if use_sink else "ctx_attn",
    )(sink, q_arr, kv_arr, kv_arr)


def _na_kernel(q_ref, kp_ref, km_ref, kn_ref, vp_ref, vm_ref, vn_ref, kc_ref, vc_ref, b0_ref, b1_ref, o_ref,
               s_ref, e_ref):
    gq = NA_ROW_GROUP * GRID_W
    nb = 3 * gq
    lane = lax.broadcasted_iota(I32, (1, LANES), 1)
    even = lane < HEAD_DIM
    for p in range(NA_HEADS // 2):
        pl_ = slice(p * LANES, (p + 1) * LANES)
        kc = kc_ref[:, pl_]
        vc = vc_ref[:, pl_]
        spans = (
            (jnp.concatenate([kp_ref[gq:2 * gq, pl_], km_ref[:, pl_]], axis=0),
             jnp.concatenate([vp_ref[gq:2 * gq, pl_], vm_ref[:, pl_]], axis=0)),
            (jnp.concatenate([km_ref[:, pl_], kn_ref[0:gq, pl_]], axis=0),
             jnp.concatenate([vm_ref[:, pl_], vn_ref[0:gq, pl_]], axis=0)),
        )
        for g, b_ref in enumerate((b0_ref, b1_ref)):
            kspan, vspan = spans[g]
            lq = _pair_stack(q_ref[g * gq:(g + 1) * gq, pl_], even)
            bias = jnp.concatenate([b_ref[0, 2 * p], b_ref[0, 2 * p + 1]], axis=0)
            s_ref[:, 0:nb] = _dot_nt(lq, kspan) + bias
            s_ref[:, nb:] = _dot_nt(lq, kc)
            r = _attend(s_ref, e_ref, jnp.concatenate([vspan, vc], axis=0), None)
            o_ref[g * gq:(g + 1) * gq, pl_] = jnp.where(even, r[0:gq], r[gq:2 * gq]).astype(BF16)


_N_DR = 2 * NA_ROWS_MAX - 1
_N_DC = 2 * NA_COLS - 1


def _na_bias_kernel(rpb_ref, o_ref):
    h = pl.program_id(0)
    qc = lax.broadcasted_iota(I32, (GRID_W, LANES), 0)
    lane = lax.broadcasted_iota(I32, (GRID_W, LANES), 1)
    kc = lane & (GRID_W - 1)
    dc = jnp.clip(kc - qc, -(NA_COLS - 1), NA_COLS - 1) + NA_COLS - 1
    col_start = jnp.clip(qc - NA_COLS // 2, 0, GRID_W - NA_COLS)
    col_ok = (kc >= col_start) & (kc < col_start + NA_COLS)
    neg = jnp.full((GRID_W, LANES), NEG_BIG, F32)
    col_bias = []
    for dr in range(_N_DR):
        c = neg
        for d in range(_N_DC):
            c = jnp.where(dc == d, rpb_ref[(h * _N_DR + dr) * _N_DC + d], c)
        col_bias.append(jnp.where(col_ok, c * LOG2E, neg))
    row_ok = (
        lambda i, j: 4 <= j < 4 + NA_ROWS_MAX,
        lambda i, j: i <= j < i + NA_ROWS_MAX,
        lambda i, j: 0 <= j < NA_ROWS_MAX,
    )
    for v in range(3):
        for i in range(NA_ROW_GROUP):
            for m in range(3 * NA_ROW_GROUP // 2):
                halves = [col_bias[j - i + 3] if row_ok[v](i, j) else neg for j in (2 * m, 2 * m + 1)]
                o_ref[v, 0, i * GRID_W:(i + 1) * GRID_W, m * LANES:(m + 1) * LANES] = jnp.where(
                    lane < GRID_W, halves[0], halves[1])


def _na_bias_tables(rpb):
    shape = (3, NA_HEADS, NA_ROW_GROUP * GRID_W, 3 * NA_ROW_GROUP * GRID_W)
    return pl.pallas_call(
        _na_bias_kernel,
        out_shape=jax.ShapeDtypeStruct(shape, F32),
        grid_spec=pltpu.PrefetchScalarGridSpec(
            num_scalar_prefetch=1,
            grid=(NA_HEADS,),
            in_specs=[],
            out_specs=pl.BlockSpec((3, 1) + shape[2:], lambda h, r: (0, h, 0, 0)),
        ),
        compiler_params=_cparams(1),
        name="na_bias",
    )(rpb.reshape(-1))


def _na_attn(na, na_c, bias_tab, batch, seq_len, ctx_len):
    rows = seq_len // GRID_W
    assert rows % (2 * NA_ROW_GROUP) == 0 and rows >= 4 * NA_ROW_GROUP
    tq = 2 * NA_ROW_GROUP * GRID_W
    nq = seq_len // tq
    gq = NA_ROW_GROUP * GRID_W
    n_keys = 3 * gq + ctx_len

    def tok_map(part, shift):
        return lambda b, i: (b * nq + jnp.clip(i + shift, 0, nq - 1), part)

    bias_block = (1, NA_HEADS, gq, 3 * gq)
    return pl.pallas_call(
        _na_kernel,
        out_shape=jax.ShapeDtypeStruct((batch * seq_len, BRANCH_W), BF16),
        grid=(batch, nq),
        in_specs=[
            pl.BlockSpec((tq, BRANCH_W), tok_map(0, 0)),
            pl.BlockSpec((tq, BRANCH_W), tok_map(1, -1)),
            pl.BlockSpec((tq, BRANCH_W), tok_map(1, 0)),
            pl.BlockSpec((tq, BRANCH_W), tok_map(1, 1)),
            pl.BlockSpec((tq, BRANCH_W), tok_map(2, -1)),
            pl.BlockSpec((tq, BRANCH_W), tok_map(2, 0)),
            pl.BlockSpec((tq, BRANCH_W), tok_map(2, 1)),
            pl.BlockSpec((ctx_len, BRANCH_W), lambda b, i: (b, 1)),
            pl.BlockSpec((ctx_len, BRANCH_W), lambda b, i: (b, 2)),
            pl.BlockSpec(bias_block, lambda b, i: (jnp.where(i == 0, 0, 1), 0, 0, 0), pipeline_mode=pl.Buffered(1)),
            pl.BlockSpec(bias_block, lambda b, i: (jnp.where(i == nq - 1, 2, 1), 0, 0, 0),
                         pipeline_mode=pl.Buffered(1)),
        ],
        out_specs=pl.BlockSpec((tq, BRANCH_W), lambda b, i: (b * nq + i, 0)),
        scratch_shapes=[pltpu.VMEM((2 * gq, n_keys), F32), pltpu.VMEM((2 * gq, n_keys), BF16)],
        compiler_params=_cparams(2),
        name="na_attn",
    )(na, na, na, na, na, na, na, na_c, na_c, bias_tab, bias_tab)


def _merge_kernel(oa_ref, conv_ref, cprev_ref, cnext_ref, oc_ref, gate_ref, x_ref, mod_ref, wb_ref, bg_ref, cw_ref,
                  wo_ref, gpost_ref, out_ref, y_ref, *, tiles_per_seq):
    tm = x_ref.shape[0]
    ti = pl.program_id(0) % tiles_per_seq
    ux = conv_ref[:, 0:512].astype(F32)
    gb = conv_ref[:, 512:1024].astype(F32)
    gc = conv_ref[:, 1024:1536].astype(F32)
    u = gc * ux
    cprev = cprev_ref[...].astype(F32)
    cnext = cnext_ref[...].astype(F32)
    halo = cprev.shape[0]
    u_before = cprev[halo - 1:halo, 1024:1536] * cprev[halo - 1:halo, 0:512]
    u_after = cnext[0:1, 1024:1536] * cnext[0:1, 0:512]
    u_before = jnp.where(ti == 0, 0.0, u_before)
    u_after = jnp.where(ti == tiles_per_seq - 1, 0.0, u_after)
    row = lax.broadcasted_iota(I32, (tm, 1), 0)
    u_dn = jnp.where(row == 0, u_before, pltpu.roll(u, 1, 0))
    u_up = jnp.where(row == tm - 1, u_after, pltpu.roll(u, tm - 1, 0))
    cw = cw_ref[...]
    ob = (gb * (cw[0:1] * u_dn + cw[1:2] * u + cw[2:3] * u_up)).astype(BF16)
    oa = oa_ref[...]
    oc = oc_ref[...]
    for lo, hi in _col_chunks(D_MODEL):
        acc = None
        for r, o in enumerate((oa, ob, oc)):
            gate = jax.nn.sigmoid(gate_ref[:, r * D_MODEL + lo:r * D_MODEL + hi].astype(F32)
                                  + bg_ref[:, r * D_MODEL + lo:r * D_MODEL + hi])
            term = gate * _dot(o, wb_ref[r, :, lo:hi])
            acc = term if acc is None else acc + term
        y_ref[:, lo:hi] = acc.astype(BF16)
    y2 = _dot(y_ref[...], wo_ref[...])
    out_ref[...] = x_ref[...] + mod_ref[0, 2:3, :] * _rms(y2, gpost_ref[...])


def _merge(oa, conv, oc, gate, x2d, mods, wb, b_gate, conv_w, wo, g_post, seq_len, tm):
    n_tok = x2d.shape[0]
    tiles_per_seq = seq_len // tm
    halo = 16
    hb = tm // halo
    n_halo = n_tok // halo
    if mods.shape[0] == 1:
        mod_map = lambda i: (0, 0, 0)
    else:
        mod_map = lambda i: (i // tiles_per_seq, 0, 0)
    kern = functools.partial(_merge_kernel, tiles_per_seq=tiles_per_seq)
    const2 = lambda i: (0, 0)
    return pl.pallas_call(
        kern,
        out_shape=jax.ShapeDtypeStruct((n_tok, D_MODEL), F32),
        grid=(n_tok // tm,),
        in_specs=[
            pl.BlockSpec((tm, BRANCH_W), lambda i: (i, 0)),
            pl.BlockSpec((tm, 3 * BRANCH_W), lambda i: (i, 0)),
            pl.BlockSpec((halo, 3 * BRANCH_W), lambda i: (jnp.maximum(i * hb - 1, 0), 0)),
            pl.BlockSpec((halo, 3 * BRANCH_W), lambda i: (jnp.minimum((i + 1) * hb, n_halo - 1), 0)),
            pl.BlockSpec((tm, BRANCH_W), lambda i: (i, 0)),
            pl.BlockSpec((tm, N_BRANCH * D_MODEL), lambda i: (i, 0)),
            pl.BlockSpec((tm, D_MODEL), lambda i: (i, 0)),
            pl.BlockSpec((1, 6, D_MODEL), mod_map),
            pl.BlockSpec((N_BRANCH, BRANCH_W, D_MODEL), lambda i: (0, 0, 0)),
            pl.BlockSpec((1, N_BRANCH * D_MODEL), const2),
            pl.BlockSpec((CONV_K, BRANCH_W), const2),
            pl.BlockSpec((D_MODEL, D_MODEL), const2),
            pl.BlockSpec((1, D_MODEL), const2),
        ],
        out_specs=pl.BlockSpec((tm, D_MODEL), lambda i: (i, 0)),
        scratch_shapes=[pltpu.VMEM((tm, D_MODEL), BF16)],
        compiler_params=_cparams(1),
        name="branch_merge",
    )(oa, conv, conv, conv, oc, gate, x2d, mods, wb, b_gate, conv_w, wo, g_post)


def _ffn_kernel(x_ref, mod_ref, gpre_ref, wgu_ref, wdn_ref, gpost_ref, out_ref, act_ref):
    x = x_ref[...]
    h = (_rms(x, gpre_ref[...]) * (1.0 + mod_ref[0, 4:5, :]) + mod_ref[0, 3:4, :]).astype(BF16)
    for lo, hi in _col_chunks(FFN_DENSE):
        g = _dot(h, wgu_ref[:, lo:hi])
        u = _dot(h, wgu_ref[:, FFN_DENSE + lo:FFN_DENSE + hi])
        act_ref[:, lo:hi] = (g * jax.nn.sigmoid(g) * u).astype(BF16)
    y = _dot(act_ref[...], wdn_ref[...])
    out_ref[...] = x + mod_ref[0, 5:6, :] * _rms(y, gpost_ref[...])


def _ffn_dense(x2d, mods, g_pre, wgu, wdn, g_post, seq_len, tm):
    n_tok = x2d.shape[0]
    tiles_per_seq = seq_len // tm
    if mods.shape[0] == 1:
        mod_map = lambda i: (0, 0, 0)
    else:
        mod_map = lambda i: (i // tiles_per_seq, 0, 0)
    const2 = lambda i: (0, 0)
    return pl.pallas_call(
        _ffn_kernel,
        out_shape=jax.ShapeDtypeStruct((n_tok, D_MODEL), F32),
        grid=(n_tok // tm,),
        in_specs=[
            pl.BlockSpec((tm, D_MODEL), lambda i: (i, 0)),
            pl.BlockSpec((1, 6, D_MODEL), mod_map),
            pl.BlockSpec((1, D_MODEL), const2),
            pl.BlockSpec((D_MODEL, 2 * FFN_DENSE), const2),
            pl.BlockSpec((FFN_DENSE, D_MODEL), const2),
            pl.BlockSpec((1, D_MODEL), const2),
        ],
        out_specs=pl.BlockSpec((tm, D_MODEL), lambda i: (i, 0)),
        scratch_shapes=[pltpu.VMEM((tm, FFN_DENSE), BF16)],
        compiler_params=_cparams(1),
        name="ffn_dense",
    )(x2d, mods, g_pre, wgu, wdn, g_post)


def _to_token_tiles(a):
    slabs = jnp.stack([a[:, s * LANES:(s + 1) * LANES] for s in range(D_MODEL // LANES)], axis=0)
    return pltpu.einshape("stl->tsl", slabs)


def _from_token_tiles(t):
    slabs = pltpu.einshape("tsl->stl", t)
    return jnp.concatenate([slabs[s] for s in range(D_MODEL // LANES)], axis=1)


TOKEN_TILE_SHAPE = (D_MODEL // LANES, LANES)


def _router_kernel(x_ref, mod_ref, gpre_ref, wr_ref, br_ref, h_ref, e_ref, gt_ref):
    x = x_ref[...]
    h = _rms(x, gpre_ref[...]) * (1.0 + mod_ref[0, 4:5, :]) + mod_ref[0, 3:4, :]
    h_ref[...] = _to_token_tiles(h)
    logits = lax.dot_general(wr_ref[...], h, (((1,), (1,)), ((), ())), preferred_element_type=F32,
                             precision=lax.Precision.HIGHEST) + br_ref[...]
    eid = lax.broadcasted_iota(I32, logits.shape, 0)
    m1 = logits.max(axis=0, keepdims=True)
    i1 = jnp.min(jnp.where(logits == m1, eid, N_EXPERTS), axis=0, keepdims=True)
    rest = jnp.where(eid == i1, -jnp.inf, logits)
    m2 = rest.max(axis=0, keepdims=True)
    i2 = jnp.min(jnp.where(rest == m2, eid, N_EXPERTS), axis=0, keepdims=True)
    e2 = jnp.exp(m2 - m1)
    den = 1.0 + e2
    e_ref[...] = jnp.concatenate([i1, i2], axis=0)
    gt_ref[...] = jnp.concatenate([1.0 / den, e2 / den], axis=0)


def _router(x2d, mods, g_pre, w_router_t, b_router, seq_len, tm):
    n_tok = x2d.shape[0]
    tiles_per_seq = seq_len // tm
    const2 = lambda i: (0, 0)
    return pl.pallas_call(
        _router_kernel,
        out_shape=[
            jax.ShapeDtypeStruct((n_tok,) + TOKEN_TILE_SHAPE, F32),
            jax.ShapeDtypeStruct((TOP_K, n_tok), I32),
            jax.ShapeDtypeStruct((TOP_K, n_tok), F32),
        ],
        grid=(n_tok // tm,),
        in_specs=[
            pl.BlockSpec((tm, D_MODEL), lambda i: (i, 0)),
            pl.BlockSpec((1, 6, D_MODEL), lambda i: (i // tiles_per_seq, 0, 0)),
            pl.BlockSpec((1, D_MODEL), const2),
            pl.BlockSpec((N_EXPERTS, D_MODEL), const2),
            pl.BlockSpec((N_EXPERTS, 1), const2),
        ],
        out_specs=[
            pl.BlockSpec((tm,) + TOKEN_TILE_SHAPE, lambda i: (i, 0, 0)),
            pl.BlockSpec((TOP_K, tm), lambda i: (0, i)),
            pl.BlockSpec((TOP_K, tm), lambda i: (0, i)),
        ],
        compiler_params=_cparams(1),
        name="moe_router",
    )(x2d, mods, g_pre, w_router_t, b_router)


def _rank_kernel(e_ref, rank_ref, cnt_ref, carry_ref):
    tm = e_ref.shape[1]

    @pl.when(pl.program_id(0) == 0)
    def _():
        carry_ref[...] = jnp.zeros_like(carry_ref)

    e = e_ref[...]
    eid = lax.broadcasted_iota(I32, (N_EXPERTS, tm), 0)
    oh0 = eid == e[0:1, :]
    oh1 = eid == e[1:2, :]
    oh = jnp.concatenate([oh0, oh1], axis=0).astype(F32)
    tri = (lax.broadcasted_iota(I32, (tm, tm), 0) <= lax.broadcasted_iota(I32, (tm, tm), 1)).astype(BF16)
    incl = _dot(oh.astype(BF16), tri)
    excl = incl - oh
    tot = incl[:, tm - 1:tm]
    carry = carry_ref[:, 0:1]
    rank0 = carry + excl[0:N_EXPERTS]
    rank1 = carry + tot[0:N_EXPERTS] + excl[N_EXPERTS:2 * N_EXPERTS]
    r0 = jnp.sum(jnp.where(oh0, rank0, 0.0), axis=0, keepdims=True)
    r1 = jnp.sum(jnp.where(oh1, rank1, 0.0), axis=0, keepdims=True)
    rank_ref[...] = jnp.concatenate([r0, r1], axis=0).astype(I32)
    new_carry = carry + tot[0:N_EXPERTS] + tot[N_EXPERTS:2 * N_EXPERTS]
    carry_ref[...] = jnp.broadcast_to(new_carry, carry_ref.shape)
    cnt_ref[...] = jnp.broadcast_to(new_carry, cnt_ref.shape).astype(I32)


def _ranks(e_idx, tm):
    n_tok = e_idx.shape[1]
    return pl.pallas_call(
        _rank_kernel,
        out_shape=[
            jax.ShapeDtypeStruct((TOP_K, n_tok), I32),
            jax.ShapeDtypeStruct((N_EXPERTS, LANES), I32),
        ],
        grid=(n_tok // tm,),
        in_specs=[pl.BlockSpec((TOP_K, tm), lambda i: (0, i))],
        out_specs=[
            pl.BlockSpec((TOP_K, tm), lambda i: (0, i)),
            pl.BlockSpec((N_EXPERTS, LANES), lambda i: (0, 0)),
        ],
        scratch_shapes=[pltpu.VMEM((N_EXPERTS, LANES), F32)],
        compiler_params=_cparams(1),
        name="moe_rank",
    )(e_idx)


def _dispatch_kernel(dest_ref, pad_ref, h_ref, xs_ref, zero_ref, sem, zsem):
    tm = h_ref.shape[0]
    n_tok = dest_ref.shape[0] // TOP_K
    base = pl.program_id(0) * tm

    @pl.when(pl.program_id(0) == 0)
    def _():
        zero_ref[...] = jnp.zeros_like(zero_ref)

        zrows = zero_ref.shape[0]

        def zero_copy(r):
            return pltpu.make_async_copy(zero_ref.at[0], xs_ref.at[r], zsem)

        def zero_chunk(j):
            return pltpu.make_async_copy(zero_ref, xs_ref.at[pl.ds(pl.multiple_of(j * zrows, zrows), zrows)], zsem)

        tail = (pad_ref[2 * N_EXPERTS] // zrows, xs_ref.shape[0] // zrows)
        for e in range(N_EXPERTS):
            lo, hi = pad_ref[e], pad_ref[N_EXPERTS + e]
            lax.fori_loop(lo, hi, lambda r, c: (zero_copy(r).start(), c)[1], 0)
        lax.fori_loop(tail[0], tail[1], lambda j, c: (zero_chunk(j).start(), c)[1], 0)
        for e in range(N_EXPERTS):
            lo, hi = pad_ref[e], pad_ref[N_EXPERTS + e]
            lax.fori_loop(lo, hi, lambda r, c: (zero_copy(r).wait(), c)[1], 0)
        lax.fori_loop(tail[0], tail[1], lambda j, c: (zero_chunk(j).wait(), c)[1], 0)

    def start(t, carry):
        for k in range(TOP_K):
            d = dest_ref[k * n_tok + base + t]
            pltpu.make_async_copy(h_ref.at[t], xs_ref.at[d], sem).start(priority=k)
        return carry

    lax.fori_loop(0, tm, start, 0, unroll=ISSUE_UNROLL)
    for k in range(TOP_K):
        pltpu.make_async_copy(h_ref, xs_ref.at[pl.ds(0, tm)], sem).wait()


def _dispatch(dest_flat, pad_rows, h, n_rows, tm):
    n_tok = h.shape[0]
    return pl.pallas_call(
        _dispatch_kernel,
        out_shape=jax.ShapeDtypeStruct((n_rows,) + TOKEN_TILE_SHAPE, F32),
        grid_spec=pltpu.PrefetchScalarGridSpec(
            num_scalar_prefetch=2,
            grid=(n_tok // tm,),
            in_specs=[pl.BlockSpec((tm,) + TOKEN_TILE_SHAPE, lambda i, d, p: (i, 0, 0))],
            out_specs=pl.BlockSpec(memory_space=pl.ANY),
            scratch_shapes=[
                pltpu.VMEM((64,) + TOKEN_TILE_SHAPE, F32),
                pltpu.SemaphoreType.DMA,
                pltpu.SemaphoreType.DMA,
            ],
        ),
        compiler_params=_cparams(1),
        name="moe_dispatch",
    )(dest_flat, pad_rows, h)


def _expert_kernel(be_ref, nb_ref, xs_ref, wgu_ref, wdn_ref, y_ref, act_ref):
    @pl.when(pl.program_id(0) < nb_ref[0])
    def _():
        xb = _from_token_tiles(xs_ref[...]).astype(BF16)
        for lo, hi in _col_chunks(FFN_EXPERT):
            g = _dot(xb, wgu_ref[0, :, lo:hi])
            u = _dot(xb, wgu_ref[0, :, FFN_EXPERT + lo:FFN_EXPERT + hi])
            act_ref[:, lo:hi] = (g * jax.nn.sigmoid(g) * u).astype(BF16)
        y_ref[...] = _to_token_tiles(_dot(act_ref[...], wdn_ref[0]))

    @pl.when(pl.program_id(0) >= nb_ref[0])
    def _():
        y_ref[...] = jnp.zeros_like(y_ref)


def _expert_ffn(block_e, n_used, xs, wgu, wdn, bm):
    n_rows = xs.shape[0]
    n_blocks = n_rows // bm

    def blk(j, be, nb):
        return jnp.minimum(j, nb[0] - 1)

    return pl.pallas_call(
        _expert_kernel,
        out_shape=jax.ShapeDtypeStruct((n_rows,) + TOKEN_TILE_SHAPE, F32),
        grid_spec=pltpu.PrefetchScalarGridSpec(
            num_scalar_prefetch=2,
            grid=(n_blocks,),
            in_specs=[
                pl.BlockSpec((bm,) + TOKEN_TILE_SHAPE, lambda j, be, nb: (blk(j, be, nb), 0, 0)),
                pl.BlockSpec((1, D_MODEL, 2 * FFN_EXPERT), lambda j, be, nb: (be[blk(j, be, nb)], 0, 0)),
                pl.BlockSpec((1, FFN_EXPERT, D_MODEL), lambda j, be, nb: (be[blk(j, be, nb)], 0, 0),
                             pipeline_mode=pl.Buffered(1)),
            ],
            out_specs=pl.BlockSpec((bm,) + TOKEN_TILE_SHAPE, lambda j, be, nb: (j, 0, 0)),
            scratch_shapes=[pltpu.VMEM((bm, FFN_EXPERT), BF16)],
        ),
        compiler_params=_cparams(1),
        name="moe_experts",
    )(block_e, n_used, xs, wgu, wdn)


def _combine_kernel(dest_ref, ys_ref, gt_ref, x_ref, mod_ref, gpost_ref, out_ref, buf_ref, sem):
    tm = x_ref.shape[0]
    n_tok = dest_ref.shape[0] // TOP_K
    i = pl.program_id(0)
    slot = lax.rem(i, 2)

    def issue(step, into):
        base = step * tm

        def start(t, carry):
            for k in range(TOP_K):
                d = dest_ref[k * n_tok + base + t]
                pltpu.make_async_copy(ys_ref.at[d], buf_ref.at[into, k, t], sem.at[into]).start(priority=k)
            return carry

        lax.fori_loop(0, tm, start, 0, unroll=ISSUE_UNROLL)

    @pl.when(i == 0)
    def _():
        issue(0, 0)

    @pl.when(i + 1 < pl.num_programs(0))
    def _():
        issue(i + 1, 1 - slot)

    for k in range(TOP_K):
        pltpu.make_async_copy(ys_ref.at[pl.ds(0, tm)], buf_ref.at[slot, k], sem.at[slot]).wait()
    gt = gt_ref[...]
    y = gt[:, 0:1] * _from_token_tiles(buf_ref[slot, 0]) + gt[:, 1:2] * _from_token_tiles(buf_ref[slot, 1])
    out_ref[...] = x_ref[...] + mod_ref[0, 5:6, :] * _rms(y, gpost_ref[...])


def _combine(dest_flat, ys, gates_t, x2d, mods, g_post, seq_len, tm):
    n_tok = x2d.shape[0]
    tiles_per_seq = seq_len // tm
    return pl.pallas_call(
        _combine_kernel,
        out_shape=jax.ShapeDtypeStruct((n_tok, D_MODEL), F32),
        grid_spec=pltpu.PrefetchScalarGridSpec(
            num_scalar_prefetch=1,
            grid=(n_tok // tm,),
            in_specs=[
                pl.BlockSpec(memory_space=pl.ANY),
                pl.BlockSpec((tm, TOP_K), lambda i, d: (i, 0)),
                pl.BlockSpec((tm, D_MODEL), lambda i, d: (i, 0)),
                pl.BlockSpec((1, 6, D_MODEL), lambda i, d: (i // tiles_per_seq, 0, 0)),
                pl.BlockSpec((1, D_MODEL), lambda i, d: (0, 0)),
            ],
            out_specs=pl.BlockSpec((tm, D_MODEL), lambda i, d: (i, 0)),
            scratch_shapes=[
                pltpu.VMEM((2, TOP_K, tm) + TOKEN_TILE_SHAPE, F32),
                pltpu.SemaphoreType.DMA((2,)),
            ],
        ),
        compiler_params=_cparams(1),
        name="moe_combine",
    )(dest_flat, ys, gates_t, x2d, mods, g_post)


def _moe(x2d, mods, g_pre, g_post, w_router, b_router, wgu, wdn, seq_len, tm):
    n_tok = x2d.shape[0]
    bm = MOE_BLOCK_ROWS
    h, e_idx, gates = _router(x2d, mods, g_pre, w_router.T, b_router.reshape(N_EXPERTS, 1), seq_len, tm)
    rank, cnt = _ranks(e_idx, tm)
    counts = cnt[:, 0]
    padded = (counts + bm - 1) // bm * bm
    ends = jnp.cumsum(padded)
    pstart = ends - padded
    dest = rank
    for e in range(N_EXPERTS):
        dest = dest + jnp.where(e_idx == e, pstart[e], 0)
    dest_flat = dest.reshape(-1).astype(I32)
    n_blocks = n_tok * TOP_K // bm + N_EXPERTS
    blk_start = jnp.arange(n_blocks, dtype=I32) * bm
    block_e = jnp.minimum(jnp.sum(blk_start[:, None] >= ends[None, :], axis=1), N_EXPERTS - 1).astype(I32)
    n_used = (ends[-1:] // bm).astype(I32)
    pad_rows = jnp.concatenate([pstart + counts, ends, ends[-1:]]).astype(I32)
    xs = _dispatch(dest_flat, pad_rows, h, n_blocks * bm, min(DISPATCH_TILE, n_tok))
    ys = _expert_ffn(block_e, n_used, xs, wgu, wdn, bm)
    return _combine(dest_flat, ys, gates.T, x2d, mods, g_post, seq_len, tm)


def _rope_tables(n_tokens):
    pos = jnp.arange(n_tokens)
    row = (pos // GRID_W).astype(F32)
    col = (pos % GRID_W).astype(F32)
    half = HEAD_DIM // 2
    inv = 1.0 / (ROPE_BASE ** (jnp.arange(0, half, 2, dtype=F32) / half))
    ar = row[:, None] * inv[None]
    ac = col[:, None] * inv[None]
    ang = jnp.concatenate([ar, ar, ac, ac], axis=-1)
    ang = jnp.concatenate([ang, ang], axis=-1)
    sign = jnp.where((jnp.arange(LANES) & 16) == 0, -1.0, 1.0).astype(F32)
    return jnp.cos(ang), jnp.sin(ang) * sign[None]


def _arrange_w_in(w):
    scale = HEAD_DIM ** -0.5 * LOG2E
    offs = np.cumsum([0, BRANCH_W, 128, 128, BRANCH_W, BRANCH_W, BRANCH_W, BRANCH_W, BRANCH_W, BRANCH_W, N_BRANCH * D_MODEL])
    parts = [w[:, offs[i]:offs[i + 1]] for i in range(10)]
    qa, ka, va, ux, gb, gc, qn, kn, vn, gx = parts

    def dup(t):
        return jnp.concatenate([t[:, 0:64], t[:, 0:64], t[:, 64:128], t[:, 64:128]], axis=1)

    return jnp.concatenate([qa * scale, dup(ka), dup(va), ux, gb, gc, qn * scale, kn, vn, gx], axis=1).astype(BF16)


def kernel(x, c, ctx, c_ctx, w_ada, b_ada, g_pre_mix, g_post_mix, g_pre_ffn, g_post_ffn, w_in, b_gate, sink, conv_w,
           rpb, w_branch, w_out, w_gu_dense, w_dn_dense, w_router, b_router, w_gu_moe, w_dn_moe):
    batch, seq_len, d = x.shape
    ctx_len = ctx.shape[1]
    tm = min(TOKEN_TILE, seq_len)
    tmc = min(TOKEN_TILE, ctx_len)

    n_vec = batch + 1
    n_vec_pad = -(-n_vec // 8) * 8
    cvecs = jnp.concatenate([c, c_ctx[None], jnp.zeros((n_vec_pad - n_vec, d), F32)], axis=0)
    mods = _ada_mods(cvecs, w_ada, b_ada)

    cos_t, sin_t = _rope_tables(seq_len)
    cos_c = jnp.ones((ctx_len, LANES), F32)
    sin_c = jnp.zeros((ctx_len, LANES), F32)

    x2 = x.reshape(batch * seq_len, d)
    hc2 = ctx.reshape(batch * ctx_len, d)
    for l in range(DEPTH):
        update_ctx = l < DEPTH - 1
        mods_x = mods[l, :batch]
        mods_c = mods[l, batch:batch + 1]
        w_arr = _arrange_w_in(w_in[l])
        wb = w_branch[l].astype(BF16)
        wo = w_out[l].astype(BF16)
        bg = b_gate[l].reshape(1, N_BRANCH * D_MODEL)
        g1 = g_pre_mix[l].reshape(1, d)
        g2 = g_post_mix[l].reshape(1, d)
        g3 = g_pre_ffn[l].reshape(1, d)
        g4 = g_post_ffn[l].reshape(1, d)

        qa, kva, conv, na, gate = _inproj(x2, mods_x, g1, cos_t, sin_t, w_arr, seq_len, tm)
        qa_c, kva_c, conv_c, na_c, gate_c = _inproj(hc2, mods_c, g1, cos_c, sin_c, w_arr, ctx_len, tmc)

        o_a = _win_attn(sink[l], qa, kva, kva_c, batch, seq_len, ctx_len)
        o_c = _na_attn(na, na_c, _na_bias_tables(rpb[l]), batch, seq_len, ctx_len)
        x2 = _merge(o_a, conv, o_c, gate, x2, mods_x, wb, bg, conv_w[l], wo, g2, seq_len, tm)

        if l % 2 == 0:
            wgu = w_gu_dense[l // 2].astype(BF16)
            wdn = w_dn_dense[l // 2].astype(BF16)
            x2 = _ffn_dense(x2, mods_x, g3, wgu, wdn, g4, seq_len, tm)
        else:
            x2 = _moe(x2, mods_x, g3, g4, w_router[l // 2], b_router[l // 2], w_gu_moe[l // 2].astype(BF16),
                      w_dn_moe[l // 2].astype(BF16), seq_len, tm)

        if update_ctx:
            o_a_c = _ctx_attn(sink[l], qa_c, 0, kva_c, 0, 2, True, batch, ctx_len, True)
            o_c_c = _ctx_attn(sink[l], na_c, 0, na_c, 4, 8, False, batch, ctx_len, False)
            hc2 = _merge(o_a_c, conv_c, o_c_c, gate_c, hc2, mods_c, wb, bg, conv_w[l], wo, g2, ctx_len, tmc)
            if l % 2 == 0:
                hc2 = _ffn_dense(hc2, mods_c, g3, wgu, wdn, g4, ctx_len, tmc)
            else:
                hc2 = _moe(hc2, jnp.broadcast_to(mods_c, (batch,) + mods_c.shape[1:]), g3, g4, w_router[l // 2],
                           b_router[l // 2], w_gu_moe[l // 2].astype(BF16), w_dn_moe[l // 2].astype(BF16),
                           ctx_len, tmc)
    return x2.reshape(batch, seq_len, d)
```

```python
import functools

import numpy as np
import jax
import jax.numpy as jnp
from jax import lax
from jax.experimental import pallas as pl
from jax.experimental.pallas import tpu as pltpu

F32 = jnp.float32
BF16 = jnp.bfloat16
I32 = jnp.int32

D_MODEL = 1024
DEPTH = 2
GRID_W = 64
HEAD_DIM = 64
BRANCH_W = 512
N_BRANCH = 3
A_HEADS = 8
A_KV_HEADS = 2
WINDOW = 128
CONV_K = 3
NA_HEADS = 8
NA_ROWS_MAX = 8
NA_COLS = 16
ROPE_BASE = 10000.0
FFN_DENSE = 2816
N_EXPERTS = 8
TOP_K = 2
FFN_EXPERT = 3584
RMS_EPS = 1e-6

LANES = 128
MXU_COLS = 256
VMEM_LIMIT_BYTES = 56 * 1024 * 1024

NEG_BIG = -1e30
LOG2E = float(np.log2(np.e))
TOKEN_TILE = 512
WIN_Q_TILE = 512
NA_ROW_GROUP = 4
MOE_BLOCK_ROWS = 512
ISSUE_UNROLL = 8
DISPATCH_TILE = 2048

_QA = (0, 512)
_KVA = (512, 1024)
_CONV = (1024, 2560)
_NA = (2560, 4096)
_GATE = (4096, 7168)
IN_COLS_ARRANGED = 7168


def _cparams(n_grid_dims, vmem=VMEM_LIMIT_BYTES):
    return pltpu.CompilerParams(dimension_semantics=("arbitrary",) * n_grid_dims, vmem_limit_bytes=vmem)


def _dot(a, b):
    return jnp.dot(a, b, preferred_element_type=F32)


def _dot_nt(a, b):
    return lax.dot_general(a, b, (((1,), (1,)), ((), ())), preferred_element_type=F32)


def _rms(x, g):
    return x * lax.rsqrt(jnp.mean(x * x, axis=-1, keepdims=True) + RMS_EPS) * g


def _col_chunks(n, step=512):
    return [(lo, min(lo + step, n)) for lo in range(0, n, step)]


def _ada_kernel(c_ref, w_ref, b_ref, o_ref):
    cv = c_ref[...]
    a = cv * jax.nn.sigmoid(cv)
    o_ref[0] = jnp.dot(a, w_ref[0], preferred_element_type=F32, precision=lax.Precision.HIGHEST) + b_ref[0]


def _ada_mods(cvecs, w_ada, b_ada):
    n_rows = cvecs.shape[0]
    tn = 1536
    out = pl.pallas_call(
        _ada_kernel,
        out_shape=jax.ShapeDtypeStruct((DEPTH, n_rows, 6 * D_MODEL), F32),
        grid=(DEPTH, 6 * D_MODEL // tn),
        in_specs=[
            pl.BlockSpec((n_rows, D_MODEL), lambda l, j: (0, 0)),
            pl.BlockSpec((1, D_MODEL, tn), lambda l, j: (l, 0, j)),
            pl.BlockSpec((1, 1, tn), lambda l, j: (l, 0, j)),
        ],
        out_specs=pl.BlockSpec((1, n_rows, tn), lambda l, j: (l, 0, j)),
        compiler_params=_cparams(2),
        name="ada_mod",
    )(cvecs, w_ada, b_ada.reshape(DEPTH, 1, 6 * D_MODEL))
    return out.reshape(DEPTH, n_rows, 6, D_MODEL)


def _inproj_kernel(x_ref, mod_ref, g_ref, cos_ref, sin_ref, w_ref, qa_ref, kva_ref, conv_ref, na_ref, gate_ref):
    x = x_ref[...]
    h = (_rms(x, g_ref[...]) * (1.0 + mod_ref[0, 1:2, :]) + mod_ref[0, 0:1, :]).astype(BF16)
    cos = cos_ref[...]
    sin = sin_ref[...]
    lane = lax.broadcasted_iota(I32, (1, LANES), 1)
    first_quarter = (lane & 16) == 0

    def mm(lo, hi):
        return _dot(h, w_ref[:, lo:hi])

    def rope(a):
        outs = []
        for j in range(a.shape[1] // LANES):
            c = a[:, j * LANES:(j + 1) * LANES]
            up = pltpu.roll(c, LANES - 16, 1)
            dn = pltpu.roll(c, 16, 1)
            outs.append(c * cos + jnp.where(first_quarter, up, dn) * sin)
        return jnp.concatenate(outs, axis=1)

    qa_ref[...] = rope(mm(*_QA)).astype(BF16)
    kv = mm(*_KVA)
    kva_ref[:, 0:256] = rope(kv[:, 0:256]).astype(BF16)
    kva_ref[:, 256:512] = kv[:, 256:512].astype(BF16)
    for ref, (base, end) in ((conv_ref, _CONV), (na_ref, _NA), (gate_ref, _GATE)):
        for lo, hi in _col_chunks(end - base):
            ref[:, lo:hi] = mm(base + lo, base + hi).astype(BF16)


def _inproj(x2d, mods, g_pre, cos_t, sin_t, w_arr, seq_len, tm):
    n_tok = x2d.shape[0]
    tiles_per_seq = seq_len // tm
    if mods.shape[0] == 1:
        mod_map = lambda i: (0, 0, 0)
    else:
        mod_map = lambda i: (i // tiles_per_seq, 0, 0)
    widths = (512, 512, 1536, 1536, 3072)
    return pl.pallas_call(
        _inproj_kernel,
        out_shape=[jax.ShapeDtypeStruct((n_tok, w), BF16) for w in widths],
        grid=(n_tok // tm,),
        in_specs=[
            pl.BlockSpec((tm, D_MODEL), lambda i: (i, 0)),
            pl.BlockSpec((1, 6, D_MODEL), mod_map),
            pl.BlockSpec((1, D_MODEL), lambda i: (0, 0)),
            pl.BlockSpec((tm, LANES), lambda i: (i % tiles_per_seq, 0)),
            pl.BlockSpec((tm, LANES), lambda i: (i % tiles_per_seq, 0)),
            pl.BlockSpec((D_MODEL, IN_COLS_ARRANGED), lambda i: (0, 0)),
        ],
        out_specs=[pl.BlockSpec((tm, w), lambda i: (i, 0)) for w in widths],
        compiler_params=_cparams(1),
        name="in_proj",
    )(x2d, mods, g_pre, cos_t, sin_t, w_arr)


def _pair_stack(q, even):
    zero = jnp.zeros_like(q)
    return jnp.concatenate([jnp.where(even, q, zero), jnp.where(even, zero, q)], axis=0)


def _softmax_pv(score_parts, value_parts, sink_col):
    m = score_parts[0].max(axis=1, keepdims=True)
    for s in score_parts[1:]:
        m = jnp.maximum(m, s.max(axis=1, keepdims=True))
    if sink_col is not None:
        m = jnp.maximum(m, sink_col)
    den = None
    acc = None
    for s, v in zip(score_parts, value_parts):
        e = jnp.exp2(s - m)
        d = e.sum(axis=1, keepdims=True)
        den = d if den is None else den + d
        r = _dot(e.astype(BF16), v)
        acc = r if acc is None else acc + r
    if sink_col is not None:
        den = den + jnp.exp2(sink_col - m)
    return acc / den


def _attend(s_ref, e_ref, values, sink_col):
    s = s_ref[...]
    m = s.max(axis=1, keepdims=True)
    if sink_col is not None:
        m = jnp.maximum(m, sink_col)
    e_ref[...] = jnp.exp2(s - m).astype(e_ref.dtype)
    r = _dot(e_ref[...], jnp.concatenate([values, jnp.ones_like(values)], axis=1))
    den = r[:, LANES:2 * LANES]
    if sink_col is not None:
        den = den + jnp.exp2(sink_col - m)
    return r[:, 0:LANES] / den


def _win_attn_kernel(sink_ref, q_ref, kvp_ref, kvm_ref, kvn_ref, kvc_ref, o_ref, s_ref, e_ref, *, seq_len, tq):
    blk0 = pl.program_id(1) * (tq // WINDOW)
    nw = 3 * WINDOW
    lane = lax.broadcasted_iota(I32, (1, LANES), 1)
    even = lane < HEAD_DIM
    n_stack = 4 * WINDOW
    rows = lax.broadcasted_iota(I32, (n_stack, 1), 0)
    piece = rows // WINDOW
    qi = rows - piece * WINDOW
    kj = lax.broadcasted_iota(I32, (1, nw), 1)
    rel = kj - WINDOW - qi
    in_window = (rel <= WINDOW) & (rel >= -WINDOW)
    for kh in range(A_KV_HEADS):
        kl = slice(kh * LANES, (kh + 1) * LANES)
        vl = slice((A_KV_HEADS + kh) * LANES, (A_KV_HEADS + kh + 1) * LANES)
        kspan = jnp.concatenate([kvp_ref[:, kl], kvm_ref[:, kl], kvn_ref[:, kl]], axis=0)
        vspan = jnp.concatenate([kvp_ref[:, vl], kvm_ref[:, vl], kvn_ref[:, vl]], axis=0)
        kc = kvc_ref[:, kl]
        vc = kvc_ref[:, vl]
        sink_col = jnp.zeros((n_stack, 1), F32)
        for j in range(4):
            sink_col = jnp.where(piece == j, sink_ref[kh * 4 + j], sink_col)
        sink_col = sink_col * LOG2E
        for n in range(tq // WINDOW):
            qb = q_ref[n * WINDOW:(n + 1) * WINDOW, 2 * kh * LANES:2 * (kh + 1) * LANES]
            lq = jnp.concatenate([_pair_stack(qb[:, 0:LANES], even), _pair_stack(qb[:, LANES:2 * LANES], even)],
                                 axis=0)
            kpos = (blk0 + n - 1) * WINDOW + kj
            valid = in_window & (kpos >= 0) & (kpos < seq_len)
            s_ref[:, 0:nw] = jnp.where(valid, _dot_nt(lq, kspan[n * WINDOW:(n + 3) * WINDOW]), NEG_BIG)
            s_ref[:, nw:] = _dot_nt(lq, kc)
            r = _attend(s_ref, e_ref, jnp.concatenate([vspan[n * WINDOW:(n + 3) * WINDOW], vc], axis=0), sink_col)
            rs = slice(n * WINDOW, (n + 1) * WINDOW)
            o_ref[rs, 2 * kh * LANES:(2 * kh + 1) * LANES] = jnp.where(
                even, r[0:WINDOW], r[WINDOW:2 * WINDOW]).astype(BF16)
            o_ref[rs, (2 * kh + 1) * LANES:(2 * kh + 2) * LANES] = jnp.where(
                even, r[2 * WINDOW:3 * WINDOW], r[3 * WINDOW:4 * WINDOW]).astype(BF16)


def _win_attn(sink, qa, kva, kva_c, batch, seq_len, ctx_len):
    tq = min(WIN_Q_TILE, seq_len)
    nblk = seq_len // WINDOW
    per = tq // WINDOW
    nq = seq_len // tq
    kvw = kva.shape[1]
    n_keys = 3 * WINDOW + ctx_len
    kern = functools.partial(_win_attn_kernel, seq_len=seq_len, tq=tq)
    return pl.pallas_call(
        kern,
        out_shape=jax.ShapeDtypeStruct((batch * seq_len, BRANCH_W), BF16),
        grid_spec=pltpu.PrefetchScalarGridSpec(
            num_scalar_prefetch=1,
            grid=(batch, nq),
            in_specs=[
                pl.BlockSpec((tq, BRANCH_W), lambda b, i, s: (b * nq + i, 0)),
                pl.BlockSpec((WINDOW, kvw), lambda b, i, s: (b * nblk + jnp.maximum(i * per - 1, 0), 0)),
                pl.BlockSpec((tq, kvw), lambda b, i, s: (b * nq + i, 0)),
                pl.BlockSpec((WINDOW, kvw), lambda b, i, s: (b * nblk + jnp.minimum((i + 1) * per, nblk - 1), 0)),
                pl.BlockSpec((ctx_len, kvw), lambda b, i, s: (b, 0)),
            ],
            out_specs=pl.BlockSpec((tq, BRANCH_W), lambda b, i, s: (b * nq + i, 0)),
            scratch_shapes=[pltpu.VMEM((4 * WINDOW, n_keys), F32), pltpu.VMEM((4 * WINDOW, n_keys), BF16)],
        ),
        compiler_params=_cparams(2),
        name="win_attn",
    )(sink, qa, kva, kva, kva, kva_c)


def _ctx_attn_kernel(sink_ref, q_ref, k_ref, v_ref, o_ref, *, use_sink):
    p = pl.program_id(1)
    n = q_ref.shape[0]
    lane = lax.broadcasted_iota(I32, (1, LANES), 1)
    even = lane < HEAD_DIM
    lq = _pair_stack(q_ref[...], even)
    s = _dot_nt(lq, k_ref[...])
    sink_col = None
    if use_sink:
        rows = lax.broadcasted_iota(I32, (2 * n, 1), 0)
        sink_col = jnp.where(rows < n, sink_ref[2 * p], sink_ref[2 * p + 1]) * LOG2E
    r = _softmax_pv([s], [v_ref[...]], sink_col)
    o_ref[...] = jnp.where(even, r[0:n], r[n:2 * n]).astype(BF16)


def _ctx_attn(sink, q_arr, q_lane0, kv_arr, k_lane0, v_lane0, kv_shared, batch, ctx_len, use_sink):
    def kv_map(lane0):
        if kv_shared:
            return lambda b, p, s: (b, lane0 + p // 2)
        return lambda b, p, s: (b, lane0 + p)

    kern = functools.partial(_ctx_attn_kernel, use_sink=use_sink)
    return pl.pallas_call(
        kern,
        out_shape=jax.ShapeDtypeStruct((batch * ctx_len, BRANCH_W), BF16),
        grid_spec=pltpu.PrefetchScalarGridSpec(
            num_scalar_prefetch=1,
            grid=(batch, BRANCH_W // LANES),
            in_specs=[
                pl.BlockSpec((ctx_len, LANES), lambda b, p, s: (b, q_lane0 + p)),
                pl.BlockSpec((ctx_len, LANES), kv_map(k_lane0)),
                pl.BlockSpec((ctx_len, LANES), kv_map(v_lane0)),
            ],
            out_specs=pl.BlockSpec((ctx_len, LANES), lambda b, p, s: (b, p)),
        ),
        compiler_params=_cparams(2),
        name="ctx_attn_sink" if use_sink else "ctx_attn",
    )(sink, q_arr, kv_arr, kv_arr)


def _na_kernel(q_ref, kp_ref, km_ref, kn_ref, vp_ref, vm_ref, vn_ref, kc_ref, vc_ref, b0_ref, b1_ref, o_ref,
               s_ref, e_ref):
    gq = NA_ROW_GROUP * GRID_W
    nb = 3 * gq
    lane = lax.broadcasted_iota(I32, (1, LANES), 1)
    even = lane < HEAD_DIM
    for p in range(NA_HEADS // 2):
        pl_ = slice(p * LANES, (p + 1) * LANES)
        kc = kc_ref[:, pl_]
        vc = vc_ref[:, pl_]
        spans = (
            (jnp.concatenate([kp_ref[gq:2 * gq, pl_], km_ref[:, pl_]], axis=0),
             jnp.concatenate([vp_ref[gq:2 * gq, pl_], vm_ref[:, pl_]], axis=0)),
            (jnp.concatenate([km_ref[:, pl_], kn_ref[0:gq, pl_]], axis=0),
             jnp.concatenate([vm_ref[:, pl_], vn_ref[0:gq, pl_]], axis=0)),
        )
        for g, b_ref in enumerate((b0_ref, b1_ref)):
            kspan, vspan = spans[g]
            lq = _pair_stack(q_ref[g * gq:(g + 1) * gq, pl_], even)
            bias = jnp.concatenate([b_ref[0, 2 * p], b_ref[0, 2 * p + 1]], axis=0)
            s_ref[:, 0:nb] = _dot_nt(lq, kspan) + bias
            s_ref[:, nb:] = _dot_nt(lq, kc)
            r = _attend(s_ref, e_ref, jnp.concatenate([vspan, vc], axis=0), None)
            o_ref[g * gq:(g + 1) * gq, pl_] = jnp.where(even, r[0:gq], r[gq:2 * gq]).astype(BF16)


_N_DR = 2 * NA_ROWS_MAX - 1
_N_DC = 2 * NA_COLS - 1


def _na_bias_kernel(rpb_ref, o_ref):
    h = pl.program_id(0)
    qc = lax.broadcasted_iota(I32, (GRID_W, LANES), 0)
    lane = lax.broadcasted_iota(I32, (GRID_W, LANES), 1)
    kc = lane & (GRID_W - 1)
    dc = jnp.clip(kc - qc, -(NA_COLS - 1), NA_COLS - 1) + NA_COLS - 1
    col_start = jnp.clip(qc - NA_COLS // 2, 0, GRID_W - NA_COLS)
    col_ok = (kc >= col_start) & (kc < col_start + NA_COLS)
    neg = jnp.full((GRID_W, LANES), NEG_BIG, F32)
    col_bias = []
    for dr in range(_N_DR):
        c = neg
        for d in range(_N_DC):
            c = jnp.where(dc == d, rpb_ref[(h * _N_DR + dr) * _N_DC + d], c)
        col_bias.append(jnp.where(col_ok, c * LOG2E, neg))
    row_ok = (
        lambda i, j: 4 <= j < 4 + NA_ROWS_MAX,
        lambda i, j: i <= j < i + NA_ROWS_MAX,
        lambda i, j: 0 <= j < NA_ROWS_MAX,
    )
    for v in range(3):
        for i in range(NA_ROW_GROUP):
            for m in range(3 * NA_ROW_GROUP // 2):
                halves = [col_bias[j - i + 3] if row_ok[v](i, j) else neg for j in (2 * m, 2 * m + 1)]
                o_ref[v, 0, i * GRID_W:(i + 1) * GRID_W, m * LANES:(m + 1) * LANES] = jnp.where(
                    lane < GRID_W, halves[0], halves[1])


def _na_bias_tables(rpb):
    shape = (3, NA_HEADS, NA_ROW_GROUP * GRID_W, 3 * NA_ROW_GROUP * GRID_W)
    return pl.pallas_call(
        _na_bias_kernel,
        out_shape=jax.ShapeDtypeStruct(shape, F32),
        grid_spec=pltpu.PrefetchScalarGridSpec(
            num_scalar_prefetch=1,
            grid=(NA_HEADS,),
            in_specs=[],
            out_specs=pl.BlockSpec((3, 1) + shape[2:], lambda h, r: (0, h, 0, 0)),
        ),
        compiler_params=_cparams(1),
        name="na_bias",
    )(rpb.reshape(-1))


def _na_attn(na, na_c, bias_tab, batch, seq_len, ctx_len):
    rows = seq_len // GRID_W
    assert rows % (2 * NA_ROW_GROUP) == 0 and rows >= 4 * NA_ROW_GROUP
    tq = 2 * NA_ROW_GROUP * GRID_W
    nq = seq_len // tq
    gq = NA_ROW_GROUP * GRID_W
    n_keys = 3 * gq + ctx_len

    def tok_map(part, shift):
        return lambda b, i: (b * nq + jnp.clip(i + shift, 0, nq - 1), part)

    bias_block = (1, NA_HEADS, gq, 3 * gq)
    return pl.pallas_call(
        _na_kernel,
        out_shape=jax.ShapeDtypeStruct((batch * seq_len, BRANCH_W), BF16),
        grid=(batch, nq),
        in_specs=[
            pl.BlockSpec((tq, BRANCH_W), tok_map(0, 0)),
            pl.BlockSpec((tq, BRANCH_W), tok_map(1, -1)),
            pl.BlockSpec((tq, BRANCH_W), tok_map(1, 0)),
            pl.BlockSpec((tq, BRANCH_W), tok_map(1, 1)),
            pl.BlockSpec((tq, BRANCH_W), tok_map(2, -1)),
            pl.BlockSpec((tq, BRANCH_W), tok_map(2, 0)),
            pl.BlockSpec((tq, BRANCH_W), tok_map(2, 1)),
            pl.BlockSpec((ctx_len, BRANCH_W), lambda b, i: (b, 1)),
            pl.BlockSpec((ctx_len, BRANCH_W), lambda b, i: (b, 2)),
            pl.BlockSpec(bias_block, lambda b, i: (jnp.where(i == 0, 0, 1), 0, 0, 0)),
            pl.BlockSpec(bias_block, lambda b, i: (jnp.where(i == nq - 1, 2, 1), 0, 0, 0)),
        ],
        out_specs=pl.BlockSpec((tq, BRANCH_W), lambda b, i: (b * nq + i, 0)),
        scratch_shapes=[pltpu.VMEM((2 * gq, n_keys), F32), pltpu.VMEM((2 * gq, n_keys), BF16)],
        compiler_params=_cparams(2),
        name="na_attn",
    )(na, na, na, na, na, na, na, na_c, na_c, bias_tab, bias_tab)


def _merge_kernel(oa_ref, conv_ref, cprev_ref, cnext_ref, oc_ref, gate_ref, x_ref, mod_ref, wb_ref, bg_ref, cw_ref,
                  wo_ref, gpost_ref, out_ref, y_ref, *, tiles_per_seq):
    tm = x_ref.shape[0]
    ti = pl.program_id(0) % tiles_per_seq
    ux = conv_ref[:, 0:512].astype(F32)
    gb = conv_ref[:, 512:1024].astype(F32)
    gc = conv_ref[:, 1024:1536].astype(F32)
    u = gc * ux
    cprev = cprev_ref[...].astype(F32)
    cnext = cnext_ref[...].astype(F32)
    halo = cprev.shape[0]
    u_before = cprev[halo - 1:halo, 1024:1536] * cprev[halo - 1:halo, 0:512]
    u_after = cnext[0:1, 1024:1536] * cnext[0:1, 0:512]
    u_before = jnp.where(ti == 0, 0.0, u_before)
    u_after = jnp.where(ti == tiles_per_seq - 1, 0.0, u_after)
    row = lax.broadcasted_iota(I32, (tm, 1), 0)
    u_dn = jnp.where(row == 0, u_before, pltpu.roll(u, 1, 0))
    u_up = jnp.where(row == tm - 1, u_after, pltpu.roll(u, tm - 1, 0))
    cw = cw_ref[...]
    ob = (gb * (cw[0:1] * u_dn + cw[1:2] * u + cw[2:3] * u_up)).astype(BF16)
    oa = oa_ref[...]
    oc = oc_ref[...]
    for lo, hi in _col_chunks(D_MODEL):
        acc = None
        for r, o in enumerate((oa, ob, oc)):
            gate = jax.nn.sigmoid(gate_ref[:, r * D_MODEL + lo:r * D_MODEL + hi].astype(F32)
                                  + bg_ref[:, r * D_MODEL + lo:r * D_MODEL + hi])
            term = gate * _dot(o, wb_ref[r, :, lo:hi])
            acc = term if acc is None else acc + term
        y_ref[:, lo:hi] = acc.astype(BF16)
    y2 = _dot(y_ref[...], wo_ref[...])
    out_ref[...] = x_ref[...] + mod_ref[0, 2:3, :] * _rms(y2, gpost_ref[...])


def _merge(oa, conv, oc, gate, x2d, mods, wb, b_gate, conv_w, wo, g_post, seq_len, tm):
    n_tok = x2d.shape[0]
    tiles_per_seq = seq_len // tm
    halo = 16
    hb = tm // halo
    n_halo = n_tok // halo
    if mods.shape[0] == 1:
        mod_map = lambda i: (0, 0, 0)
    else:
        mod_map = lambda i: (i // tiles_per_seq, 0, 0)
    kern = functools.partial(_merge_kernel, tiles_per_seq=tiles_per_seq)
    const2 = lambda i: (0, 0)
    return pl.pallas_call(
        kern,
        out_shape=jax.ShapeDtypeStruct((n_tok, D_MODEL), F32),
        grid=(n_tok // tm,),
        in_specs=[
            pl.BlockSpec((tm, BRANCH_W), lambda i: (i, 0)),
            pl.BlockSpec((tm, 3 * BRANCH_W), lambda i: (i, 0)),
            pl.BlockSpec((halo, 3 * BRANCH_W), lambda i: (jnp.maximum(i * hb - 1, 0), 0)),
            pl.BlockSpec((halo, 3 * BRANCH_W), lambda i: (jnp.minimum((i + 1) * hb, n_halo - 1), 0)),
            pl.BlockSpec((tm, BRANCH_W), lambda i: (i, 0)),
            pl.BlockSpec((tm, N_BRANCH * D_MODEL), lambda i: (i, 0)),
            pl.BlockSpec((tm, D_MODEL), lambda i: (i, 0)),
            pl.BlockSpec((1, 6, D_MODEL), mod_map),
            pl.BlockSpec((N_BRANCH, BRANCH_W, D_MODEL), lambda i: (0, 0, 0)),
            pl.BlockSpec((1, N_BRANCH * D_MODEL), const2),
            pl.BlockSpec((CONV_K, BRANCH_W), const2),
            pl.BlockSpec((D_MODEL, D_MODEL), const2),
            pl.BlockSpec((1, D_MODEL), const2),
        ],
        out_specs=pl.BlockSpec((tm, D_MODEL), lambda i: (i, 0)),
        scratch_shapes=[pltpu.VMEM((tm, D_MODEL), BF16)],
        compiler_params=_cparams(1),
        name="branch_merge",
    )(oa, conv, conv, conv, oc, gate, x2d, mods, wb, b_gate, conv_w, wo, g_post)


def _ffn_kernel(x_ref, mod_ref, gpre_ref, wgu_ref, wdn_ref, gpost_ref, out_ref, act_ref):
    x = x_ref[...]
    h = (_rms(x, gpre_ref[...]) * (1.0 + mod_ref[0, 4:5, :]) + mod_ref[0, 3:4, :]).astype(BF16)
    for lo, hi in _col_chunks(FFN_DENSE):
        g = _dot(h, wgu_ref[:, lo:hi])
        u = _dot(h, wgu_ref[:, FFN_DENSE + lo:FFN_DENSE + hi])
        act_ref[:, lo:hi] = (g * jax.nn.sigmoid(g) * u).astype(BF16)
    y = _dot(act_ref[...], wdn_ref[...])
    out_ref[...] = x + mod_ref[0, 5:6, :] * _rms(y, gpost_ref[...])


def _ffn_dense(x2d, mods, g_pre, wgu, wdn, g_post, seq_len, tm):
    n_tok = x2d.shape[0]
    tiles_per_seq = seq_len // tm
    if mods.shape[0] == 1:
        mod_map = lambda i: (0, 0, 0)
    else:
        mod_map = lambda i: (i // tiles_per_seq, 0, 0)
    const2 = lambda i: (0, 0)
    return pl.pallas_call(
        _ffn_kernel,
        out_shape=jax.ShapeDtypeStruct((n_tok, D_MODEL), F32),
        grid=(n_tok // tm,),
        in_specs=[
            pl.BlockSpec((tm, D_MODEL), lambda i: (i, 0)),
            pl.BlockSpec((1, 6, D_MODEL), mod_map),
            pl.BlockSpec((1, D_MODEL), const2),
            pl.BlockSpec((D_MODEL, 2 * FFN_DENSE), const2),
            pl.BlockSpec((FFN_DENSE, D_MODEL), const2),
            pl.BlockSpec((1, D_MODEL), const2),
        ],
        out_specs=pl.BlockSpec((tm, D_MODEL), lambda i: (i, 0)),
        scratch_shapes=[pltpu.VMEM((tm, FFN_DENSE), BF16)],
        compiler_params=_cparams(1),
        name="ffn_dense",
    )(x2d, mods, g_pre, wgu, wdn, g_post)


def _to_token_tiles(a):
    slabs = jnp.stack([a[:, s * LANES:(s + 1) * LANES] for s in range(D_MODEL // LANES)], axis=0)
    return pltpu.einshape("stl->tsl", slabs)


def _from_token_tiles(t):
    slabs = pltpu.einshape("tsl->stl", t)
    return jnp.concatenate([slabs[s] for s in range(D_MODEL // LANES)], axis=1)


TOKEN_TILE_SHAPE = (D_MODEL // LANES, LANES)


def _router_kernel(x_ref, mod_ref, gpre_ref, wr_ref, br_ref, h_ref, e_ref, gt_ref):
    x = x_ref[...]
    h = _rms(x, gpre_ref[...]) * (1.0 + mod_ref[0, 4:5, :]) + mod_ref[0, 3:4, :]
    h_ref[...] = _to_token_tiles(h)
    logits = lax.dot_general(wr_ref[...], h, (((1,), (1,)), ((), ())), preferred_element_type=F32,
                             precision=lax.Precision.HIGHEST) + br_ref[...]
    eid = lax.broadcasted_iota(I32, logits.shape, 0)
    m1 = logits.max(axis=0, keepdims=True)
    i1 = jnp.min(jnp.where(logits == m1, eid, N_EXPERTS), axis=0, keepdims=True)
    rest = jnp.where(eid == i1, -jnp.inf, logits)
    m2 = rest.max(axis=0, keepdims=True)
    i2 = jnp.min(jnp.where(rest == m2, eid, N_EXPERTS), axis=0, keepdims=True)
    e2 = jnp.exp(m2 - m1)
    den = 1.0 + e2
    e_ref[...] = jnp.concatenate([i1, i2], axis=0)
    gt_ref[...] = jnp.concatenate([1.0 / den, e2 / den], axis=0)


def _router(x2d, mods, g_pre, w_router_t, b_router, seq_len, tm):
    n_tok = x2d.shape[0]
    tiles_per_seq = seq_len // tm
    const2 = lambda i: (0, 0)
    return pl.pallas_call(
        _router_kernel,
        out_shape=[
            jax.ShapeDtypeStruct((n_tok,) + TOKEN_TILE_SHAPE, F32),
            jax.ShapeDtypeStruct((TOP_K, n_tok), I32),
            jax.ShapeDtypeStruct((TOP_K, n_tok), F32),
        ],
        grid=(n_tok // tm,),
        in_specs=[
            pl.BlockSpec((tm, D_MODEL), lambda i: (i, 0)),
            pl.BlockSpec((1, 6, D_MODEL), lambda i: (i // tiles_per_seq, 0, 0)),
            pl.BlockSpec((1, D_MODEL), const2),
            pl.BlockSpec((N_EXPERTS, D_MODEL), const2),
            pl.BlockSpec((N_EXPERTS, 1), const2),
        ],
        out_specs=[
            pl.BlockSpec((tm,) + TOKEN_TILE_SHAPE, lambda i: (i, 0, 0)),
            pl.BlockSpec((TOP_K, tm), lambda i: (0, i)),
            pl.BlockSpec((TOP_K, tm), lambda i: (0, i)),
        ],
        compiler_params=_cparams(1),
        name="moe_router",
    )(x2d, mods, g_pre, w_router_t, b_router)


def _rank_kernel(e_ref, rank_ref, cnt_ref, carry_ref):
    tm = e_ref.shape[1]

    @pl.when(pl.program_id(0) == 0)
    def _():
        carry_ref[...] = jnp.zeros_like(carry_ref)

    e = e_ref[...]
    eid = lax.broadcasted_iota(I32, (N_EXPERTS, tm), 0)
    oh0 = eid == e[0:1, :]
    oh1 = eid == e[1:2, :]
    oh = jnp.concatenate([oh0, oh1], axis=0).astype(F32)
    tri = (lax.broadcasted_iota(I32, (tm, tm), 0) <= lax.broadcasted_iota(I32, (tm, tm), 1)).astype(BF16)
    incl = _dot(oh.astype(BF16), tri)
    excl = incl - oh
    tot = incl[:, tm - 1:tm]
    carry = carry_ref[:, 0:1]
    rank0 = carry + excl[0:N_EXPERTS]
    rank1 = carry + tot[0:N_EXPERTS] + excl[N_EXPERTS:2 * N_EXPERTS]
    r0 = jnp.sum(jnp.where(oh0, rank0, 0.0), axis=0, keepdims=True)
    r1 = jnp.sum(jnp.where(oh1, rank1, 0.0), axis=0, keepdims=True)
    rank_ref[...] = jnp.concatenate([r0, r1], axis=0).astype(I32)
    new_carry = carry + tot[0:N_EXPERTS] + tot[N_EXPERTS:2 * N_EXPERTS]
    carry_ref[...] = jnp.broadcast_to(new_carry, carry_ref.shape)
    cnt_ref[...] = jnp.broadcast_to(new_carry, cnt_ref.shape).astype(I32)


def _ranks(e_idx, tm):
    n_tok = e_idx.shape[1]
    return pl.pallas_call(
        _rank_kernel,
        out_shape=[
            jax.ShapeDtypeStruct((TOP_K, n_tok), I32),
            jax.ShapeDtypeStruct((N_EXPERTS, LANES), I32),
        ],
        grid=(n_tok // tm,),
        in_specs=[pl.BlockSpec((TOP_K, tm), lambda i: (0, i))],
        out_specs=[
            pl.BlockSpec((TOP_K, tm), lambda i: (0, i)),
            pl.BlockSpec((N_EXPERTS, LANES), lambda i: (0, 0)),
        ],
        scratch_shapes=[pltpu.VMEM((N_EXPERTS, LANES), F32)],
        compiler_params=_cparams(1),
        name="moe_rank",
    )(e_idx)


def _dispatch_kernel(dest_ref, pad_ref, h_ref, xs_ref, zero_ref, sem, zsem):
    tm = h_ref.shape[0]
    n_tok = dest_ref.shape[0] // TOP_K
    base = pl.program_id(0) * tm

    @pl.when(pl.program_id(0) == 0)
    def _():
        zero_ref[...] = jnp.zeros_like(zero_ref)

        zrows = zero_ref.shape[0]

        def zero_copy(r):
            return pltpu.make_async_copy(zero_ref.at[0], xs_ref.at[r], zsem)

        def zero_chunk(j):
            return pltpu.make_async_copy(zero_ref, xs_ref.at[pl.ds(pl.multiple_of(j * zrows, zrows), zrows)], zsem)

        tail = (pad_ref[2 * N_EXPERTS] // zrows, xs_ref.shape[0] // zrows)
        for e in range(N_EXPERTS):
            lo, hi = pad_ref[e], pad_ref[N_EXPERTS + e]
            lax.fori_loop(lo, hi, lambda r, c: (zero_copy(r).start(), c)[1], 0)
        lax.fori_loop(tail[0], tail[1], lambda j, c: (zero_chunk(j).start(), c)[1], 0)
        for e in range(N_EXPERTS):
            lo, hi = pad_ref[e], pad_ref[N_EXPERTS + e]
            lax.fori_loop(lo, hi, lambda r, c: (zero_copy(r).wait(), c)[1], 0)
        lax.fori_loop(tail[0], tail[1], lambda j, c: (zero_chunk(j).wait(), c)[1], 0)

    def start(t, carry):
        for k in range(TOP_K):
            d = dest_ref[k * n_tok + base + t]
            pltpu.make_async_copy(h_ref.at[t], xs_ref.at[d], sem).start(priority=k)
        return carry

    lax.fori_loop(0, tm, start, 0, unroll=ISSUE_UNROLL)
    for k in range(TOP_K):
        pltpu.make_async_copy(h_ref, xs_ref.at[pl.ds(0, tm)], sem).wait()


def _dispatch(dest_flat, pad_rows, h, n_rows, tm):
    n_tok = h.shape[0]
    return pl.pallas_call(
        _dispatch_kernel,
        out_shape=jax.ShapeDtypeStruct((n_rows,) + TOKEN_TILE_SHAPE, F32),
        grid_spec=pltpu.PrefetchScalarGridSpec(
            num_scalar_prefetch=2,
            grid=(n_tok // tm,),
            in_specs=[pl.BlockSpec((tm,) + TOKEN_TILE_SHAPE, lambda i, d, p: (i, 0, 0))],
            out_specs=pl.BlockSpec(memory_space=pl.ANY),
            scratch_shapes=[
                pltpu.VMEM((64,) + TOKEN_TILE_SHAPE, F32),
                pltpu.SemaphoreType.DMA,
                pltpu.SemaphoreType.DMA,
            ],
        ),
        compiler_params=_cparams(1),
        name="moe_dispatch",
    )(dest_flat, pad_rows, h)


def _expert_kernel(be_ref, nb_ref, xs_ref, wgu_ref, wdn_ref, y_ref, act_ref):
    @pl.when(pl.program_id(0) < nb_ref[0])
    def _():
        xb = _from_token_tiles(xs_ref[...]).astype(BF16)
        for lo, hi in _col_chunks(FFN_EXPERT):
            g = _dot(xb, wgu_ref[0, :, lo:hi])
            u = _dot(xb, wgu_ref[0, :, FFN_EXPERT + lo:FFN_EXPERT + hi])
            act_ref[:, lo:hi] = (g * jax.nn.sigmoid(g) * u).astype(BF16)
        y_ref[...] = _to_token_tiles(_dot(act_ref[...], wdn_ref[0]))

    @pl.when(pl.program_id(0) >= nb_ref[0])
    def _():
        y_ref[...] = jnp.zeros_like(y_ref)


def _expert_ffn(block_e, n_used, xs, wgu, wdn, bm):
    n_rows = xs.shape[0]
    n_blocks = n_rows // bm

    def blk(j, be, nb):
        return jnp.minimum(j, nb[0] - 1)

    return pl.pallas_call(
        _expert_kernel,
        out_shape=jax.ShapeDtypeStruct((n_rows,) + TOKEN_TILE_SHAPE, F32),
        grid_spec=pltpu.PrefetchScalarGridSpec(
            num_scalar_prefetch=2,
            grid=(n_blocks,),
            in_specs=[
                pl.BlockSpec((bm,) + TOKEN_TILE_SHAPE, lambda j, be, nb: (blk(j, be, nb), 0, 0)),
                pl.BlockSpec((1, D_MODEL, 2 * FFN_EXPERT), lambda j, be, nb: (be[blk(j, be, nb)], 0, 0)),
                pl.BlockSpec((1, FFN_EXPERT, D_MODEL), lambda j, be, nb: (be[blk(j, be, nb)], 0, 0),
                             pipeline_mode=pl.Buffered(1)),
            ],
            out_specs=pl.BlockSpec((bm,) + TOKEN_TILE_SHAPE, lambda j, be, nb: (j, 0, 0)),
            scratch_shapes=[pltpu.VMEM((bm, FFN_EXPERT), BF16)],
        ),
        compiler_params=_cparams(1),
        name="moe_experts",
    )(block_e, n_used, xs, wgu, wdn)


def _combine_kernel(dest_ref, ys_ref, gt_ref, x_ref, mod_ref, gpost_ref, out_ref, buf_ref, sem):
    tm = x_ref.shape[0]
    n_tok = dest_ref.shape[0] // TOP_K
    i = pl.program_id(0)
    slot = lax.rem(i, 2)

    def issue(step, into):
        base = step * tm

        def start(t, carry):
            for k in range(TOP_K):
                d = dest_ref[k * n_tok + base + t]
                pltpu.make_async_copy(ys_ref.at[d], buf_ref.at[into, k, t], sem.at[into]).start(priority=k)
            return carry

        lax.fori_loop(0, tm, start, 0, unroll=ISSUE_UNROLL)

    @pl.when(i == 0)
    def _():
        issue(0, 0)

    @pl.when(i + 1 < pl.num_programs(0))
    def _():
        issue(i + 1, 1 - slot)

    for k in range(TOP_K):
        pltpu.make_async_copy(ys_ref.at[pl.ds(0, tm)], buf_ref.at[slot, k], sem.at[slot]).wait()
    gt = gt_ref[...]
    y = gt[:, 0:1] * _from_token_tiles(buf_ref[slot, 0]) + gt[:, 1:2] * _from_token_tiles(buf_ref[slot, 1])
    out_ref[...] = x_ref[...] + mod_ref[0, 5:6, :] * _rms(y, gpost_ref[...])


def _combine(dest_flat, ys, gates_t, x2d, mods, g_post, seq_len, tm):
    n_tok = x2d.shape[0]
    tiles_per_seq = seq_len // tm
    return pl.pallas_call(
        _combine_kernel,
        out_shape=jax.ShapeDtypeStruct((n_tok, D_MODEL), F32),
        grid_spec=pltpu.PrefetchScalarGridSpec(
            num_scalar_prefetch=1,
            grid=(n_tok // tm,),
            in_specs=[
                pl.BlockSpec(memory_space=pl.ANY),
                pl.BlockSpec((tm, TOP_K), lambda i, d: (i, 0)),
                pl.BlockSpec((tm, D_MODEL), lambda i, d: (i, 0)),
                pl.BlockSpec((1, 6, D_MODEL), lambda i, d: (i // tiles_per_seq, 0, 0)),
                pl.BlockSpec((1, D_MODEL), lambda i, d: (0, 0)),
            ],
            out_specs=pl.BlockSpec((tm, D_MODEL), lambda i, d: (i, 0)),
            scratch_shapes=[
                pltpu.VMEM((2, TOP_K, tm) + TOKEN_TILE_SHAPE, F32),
                pltpu.SemaphoreType.DMA((2,)),
            ],
        ),
        compiler_params=_cparams(1),
        name="moe_combine",
    )(dest_flat, ys, gates_t, x2d, mods, g_post)


def _moe(x2d, mods, g_pre, g_post, w_router, b_router, wgu, wdn, seq_len, tm):
    n_tok = x2d.shape[0]
    bm = MOE_BLOCK_ROWS
    h, e_idx, gates = _router(x2d, mods, g_pre, w_router.T, b_router.reshape(N_EXPERTS, 1), seq_len, tm)
    rank, cnt = _ranks(e_idx, tm)
    counts = cnt[:, 0]
    padded = (counts + bm - 1) // bm * bm
    ends = jnp.cumsum(padded)
    pstart = ends - padded
    dest = rank
    for e in range(N_EXPERTS):
        dest = dest + jnp.where(e_idx == e, pstart[e], 0)
    dest_flat = dest.reshape(-1).astype(I32)
    n_blocks = n_tok * TOP_K // bm + N_EXPERTS
    blk_start = jnp.arange(n_blocks, dtype=I32) * bm
    block_e = jnp.minimum(jnp.sum(blk_start[:, None] >= ends[None, :], axis=1), N_EXPERTS - 1).astype(I32)
    n_used = (ends[-1:] // bm).astype(I32)
    pad_rows = jnp.concatenate([pstart + counts, ends, ends[-1:]]).astype(I32)
    xs = _dispatch(dest_flat, pad_rows, h, n_blocks * bm, min(DISPATCH_TILE, n_tok))
    ys = _expert_ffn(block_e, n_used, xs, wgu, wdn, bm)
    return _combine(dest_flat, ys, gates.T, x2d, mods, g_post, seq_len, tm)


def _rope_tables(n_tokens):
    pos = jnp.arange(n_tokens)
    row = (pos // GRID_W).astype(F32)
    col = (pos % GRID_W).astype(F32)
    half = HEAD_DIM // 2
    inv = 1.0 / (ROPE_BASE ** (jnp.arange(0, half, 2, dtype=F32) / half))
    ar = row[:, None] * inv[None]
    ac = col[:, None] * inv[None]
    ang = jnp.concatenate([ar, ar, ac, ac], axis=-1)
    ang = jnp.concatenate([ang, ang], axis=-1)
    sign = jnp.where((jnp.arange(LANES) & 16) == 0, -1.0, 1.0).astype(F32)
    return jnp.cos(ang), jnp.sin(ang) * sign[None]


def _arrange_w_in(w):
    scale = HEAD_DIM ** -0.5 * LOG2E
    offs = np.cumsum([0, BRANCH_W, 128, 128, BRANCH_W, BRANCH_W, BRANCH_W, BRANCH_W, BRANCH_W, BRANCH_W, N_BRANCH * D_MODEL])
    parts = [w[:, offs[i]:offs[i + 1]] for i in range(10)]
    qa, ka, va, ux, gb, gc, qn, kn, vn, gx = parts

    def dup(t):
        return jnp.concatenate([t[:, 0:64], t[:, 0:64], t[:, 64:128], t[:, 64:128]], axis=1)

    return jnp.concatenate([qa * scale, dup(ka), dup(va), ux, gb, gc, qn * scale, kn, vn, gx], axis=1).astype(BF16)


def kernel(x, c, ctx, c_ctx, w_ada, b_ada, g_pre_mix, g_post_mix, g_pre_ffn, g_post_ffn, w_in, b_gate, sink, conv_w,
           rpb, w_branch, w_out, w_gu_dense, w_dn_dense, w_router, b_router, w_gu_moe, w_dn_moe):
    batch, seq_len, d = x.shape
    ctx_len = ctx.shape[1]
    tm = min(TOKEN_TILE, seq_len)
    tmc = min(TOKEN_TILE, ctx_len)

    n_vec = batch + 1
    n_vec_pad = -(-n_vec // 8) * 8
    cvecs = jnp.concatenate([c, c_ctx[None], jnp.zeros((n_vec_pad - n_vec, d), F32)], axis=0)
    mods = _ada_mods(cvecs, w_ada, b_ada)

    cos_t, sin_t = _rope_tables(seq_len)
    cos_c = jnp.ones((ctx_len, LANES), F32)
    sin_c = jnp.zeros((ctx_len, LANES), F32)

    x2 = x.reshape(batch * seq_len, d)
    hc2 = ctx.reshape(batch * ctx_len, d)
    for l in range(DEPTH):
        update_ctx = l < DEPTH - 1
        mods_x = mods[l, :batch]
        mods_c = mods[l, batch:batch + 1]
        w_arr = _arrange_w_in(w_in[l])
        wb = w_branch[l].astype(BF16)
        wo = w_out[l].astype(BF16)
        bg = b_gate[l].reshape(1, N_BRANCH * D_MODEL)
        g1 = g_pre_mix[l].reshape(1, d)
        g2 = g_post_mix[l].reshape(1, d)
        g3 = g_pre_ffn[l].reshape(1, d)
        g4 = g_post_ffn[l].reshape(1, d)

        qa, kva, conv, na, gate = _inproj(x2, mods_x, g1, cos_t, sin_t, w_arr, seq_len, tm)
        qa_c, kva_c, conv_c, na_c, gate_c = _inproj(hc2, mods_c, g1, cos_c, sin_c, w_arr, ctx_len, tmc)

        o_a = _win_attn(sink[l], qa, kva, kva_c, batch, seq_len, ctx_len)
        o_c = _na_attn(na, na_c, _na_bias_tables(rpb[l]), batch, seq_len, ctx_len)
        x2 = _merge(o_a, conv, o_c, gate, x2, mods_x, wb, bg, conv_w[l], wo, g2, seq_len, tm)

        if l % 2 == 0:
            wgu = w_gu_dense[l // 2].astype(BF16)
            wdn = w_dn_dense[l // 2].astype(BF16)
            x2 = _ffn_dense(x2, mods_x, g3, wgu, wdn, g4, seq_len, tm)
        else:
            x2 = _moe(x2, mods_x, g3, g4, w_router[l // 2], b_router[l // 2], w_gu_moe[l // 2].astype(BF16),
                      w_dn_moe[l // 2].astype(BF16), seq_len, tm)

        if update_ctx:
            o_a_c = _ctx_attn(sink[l], qa_c, 0, kva_c, 0, 2, True, batch, ctx_len, True)
            o_c_c = _ctx_attn(sink[l], na_c, 0, na_c, 4, 8, False, batch, ctx_len, False)
            hc2 = _merge(o_a_c, conv_c, o_c_c, gate_c, hc2, mods_c, wb, bg, conv_w[l], wo, g2, ctx_len, tmc)
            if l % 2 == 0:
                hc2 = _ffn_dense(hc2, mods_c, g3, wgu, wdn, g4, ctx_len, tmc)
            else:
                hc2 = _moe(hc2, jnp.broadcast_to(mods_c, (batch,) + mods_c.shape[1:]), g3, g4, w_router[l // 2],
                           b_router[l // 2], w_gu_moe[l // 2].astype(BF16), w_dn_moe[l // 2].astype(BF16),
                           ctx_len, tmc)
    return x2.reshape(batch, seq_len, d)
```

```python
import functools

import numpy as np
import jax
import jax.numpy as jnp
from jax import lax
from jax.experimental import pallas as pl
from jax.experimental.pallas import tpu as pltpu

F32 = jnp.float32
BF16 = jnp.bfloat16
I32 = jnp.int32

D_MODEL = 1024
DEPTH = 2
GRID_W = 64
HEAD_DIM = 64
BRANCH_W = 512
N_BRANCH = 3
A_HEADS = 8
A_KV_HEADS = 2
WINDOW = 128
CONV_K = 3
NA_HEADS = 8
NA_ROWS_MAX = 8
NA_COLS = 16
ROPE_BASE = 10000.0
FFN_DENSE = 2816
N_EXPERTS = 8
TOP_K = 2
FFN_EXPERT = 3584
RMS_EPS = 1e-6

LANES = 128
MXU_COLS = 256
VMEM_LIMIT_BYTES = 56 * 1024 * 1024

NEG_BIG = -1e30
LOG2E = float(np.log2(np.e))
TOKEN_TILE = 512
WIN_Q_TILE = 512
NA_ROW_GROUP = 4
MOE_BLOCK_ROWS = 512
ISSUE_UNROLL = 8
DISPATCH_TILE = 2048

_QA = (0, 512)
_KVA = (512, 1024)
_CONV = (1024, 2560)
_NA = (2560, 4096)
_GATE = (4096, 7168)
IN_COLS_ARRANGED = 7168


def _cparams(n_grid_dims, vmem=VMEM_LIMIT_BYTES):
    return pltpu.CompilerParams(dimension_semantics=("arbitrary",) * n_grid_dims, vmem_limit_bytes=vmem)


def _dot(a, b):
    return jnp.dot(a, b, preferred_element_type=F32)


def _dot_nt(a, b):
    return lax.dot_general(a, b, (((1,), (1,)), ((), ())), preferred_element_type=F32)


def _rms(x, g):
    return x * lax.rsqrt(jnp.mean(x * x, axis=-1, keepdims=True) + RMS_EPS) * g


def _col_chunks(n, step=512):
    return [(lo, min(lo + step, n)) for lo in range(0, n, step)]


def _ada_kernel(c_ref, w_ref, b_ref, o_ref):
    cv = c_ref[...]
    a = cv * jax.nn.sigmoid(cv)
    o_ref[0] = jnp.dot(a, w_ref[0], preferred_element_type=F32, precision=lax.Precision.HIGHEST) + b_ref[0]


def _ada_mods(cvecs, w_ada, b_ada):
    n_rows = cvecs.shape[0]
    tn = 1536
    out = pl.pallas_call(
        _ada_kernel,
        out_shape=jax.ShapeDtypeStruct((DEPTH, n_rows, 6 * D_MODEL), F32),
        grid=(DEPTH, 6 * D_MODEL // tn),
        in_specs=[
            pl.BlockSpec((n_rows, D_MODEL), lambda l, j: (0, 0)),
            pl.BlockSpec((1, D_MODEL, tn), lambda l, j: (l, 0, j)),
            pl.BlockSpec((1, 1, tn), lambda l, j: (l, 0, j)),
        ],
        out_specs=pl.BlockSpec((1, n_rows, tn), lambda l, j: (l, 0, j)),
        compiler_params=_cparams(2),
        name="ada_mod",
    )(cvecs, w_ada, b_ada.reshape(DEPTH, 1, 6 * D_MODEL))
    return out.reshape(DEPTH, n_rows, 6, D_MODEL)


def _inproj_kernel(x_ref, mod_ref, g_ref, cos_ref, sin_ref, w_ref, qa_ref, kva_ref, conv_ref, na_ref, gate_ref):
    x = x_ref[...]
    h = (_rms(x, g_ref[...]) * (1.0 + mod_ref[0, 1:2, :]) + mod_ref[0, 0:1, :]).astype(BF16)
    cos = cos_ref[...]
    sin = sin_ref[...]
    lane = lax.broadcasted_iota(I32, (1, LANES), 1)
    first_quarter = (lane & 16) == 0

    def mm(lo, hi):
        return _dot(h, w_ref[:, lo:hi])

    def rope(a):
        outs = []
        for j in range(a.shape[1] // LANES):
            c = a[:, j * LANES:(j + 1) * LANES]
            up = pltpu.roll(c, LANES - 16, 1)
            dn = pltpu.roll(c, 16, 1)
            outs.append(c * cos + jnp.where(first_quarter, up, dn) * sin)
        return jnp.concatenate(outs, axis=1)

    qa_ref[...] = rope(mm(*_QA)).astype(BF16)
    kv = mm(*_KVA)
    kva_ref[:, 0:256] = rope(kv[:, 0:256]).astype(BF16)
    kva_ref[:, 256:512] = kv[:, 256:512].astype(BF16)
    for ref, (base, end) in ((conv_ref, _CONV), (na_ref, _NA), (gate_ref, _GATE)):
        for lo, hi in _col_chunks(end - base):
            ref[:, lo:hi] = mm(base + lo, base + hi).astype(BF16)


def _inproj(x2d, mods, g_pre, cos_t, sin_t, w_arr, seq_len, tm):
    n_tok = x2d.shape[0]
    tiles_per_seq = seq_len // tm
    if mods.shape[0] == 1:
        mod_map = lambda i: (0, 0, 0)
    else:
        mod_map = lambda i: (i // tiles_per_seq, 0, 0)
    widths = (512, 512, 1536, 1536, 3072)
    return pl.pallas_call(
        _inproj_kernel,
        out_shape=[jax.ShapeDtypeStruct((n_tok, w), BF16) for w in widths],
        grid=(n_tok // tm,),
        in_specs=[
            pl.BlockSpec((tm, D_MODEL), lambda i: (i, 0)),
            pl.BlockSpec((1, 6, D_MODEL), mod_map),
            pl.BlockSpec((1, D_MODEL), lambda i: (0, 0)),
            pl.BlockSpec((tm, LANES), lambda i: (i % tiles_per_seq, 0)),
            pl.BlockSpec((tm, LANES), lambda i: (i % tiles_per_seq, 0)),
            pl.BlockSpec((D_MODEL, IN_COLS_ARRANGED), lambda i: (0, 0)),
        ],
        out_specs=[pl.BlockSpec((tm, w), lambda i: (i, 0)) for w in widths],
        compiler_params=_cparams(1),
        name="in_proj",
    )(x2d, mods, g_pre, cos_t, sin_t, w_arr)


def _pair_stack(q, even):
    zero = jnp.zeros_like(q)
    return jnp.concatenate([jnp.where(even, q, zero), jnp.where(even, zero, q)], axis=0)


def _softmax_pv(score_parts, value_parts, sink_col):
    m = score_parts[0].max(axis=1, keepdims=True)
    for s in score_parts[1:]:
        m = jnp.maximum(m, s.max(axis=1, keepdims=True))
    if sink_col is not None:
        m = jnp.maximum(m, sink_col)
    den = None
    acc = None
    for s, v in zip(score_parts, value_parts):
        e = jnp.exp2(s - m)
        d = e.sum(axis=1, keepdims=True)
        den = d if den is None else den + d
        r = _dot(e.astype(BF16), v)
        acc = r if acc is None else acc + r
    if sink_col is not None:
        den = den + jnp.exp2(sink_col - m)
    return acc / den


def _attend(s_ref, e_ref, values, sink_col):
    s = s_ref[...]
    m = s.max(axis=1, keepdims=True)
    if sink_col is not None:
        m = jnp.maximum(m, sink_col)
    e_ref[...] = jnp.exp2(s - m).astype(e_ref.dtype)
    r = _dot(e_ref[...], jnp.concatenate([values, jnp.ones_like(values)], axis=1))
    den = r[:, LANES:2 * LANES]
    if sink_col is not None:
        den = den + jnp.exp2(sink_col - m)
    return r[:, 0:LANES] / den


def _win_attn_kernel(sink_ref, q_ref, kvp_ref, kvm_ref, kvn_ref, kvc_ref, o_ref, s_ref, e_ref, *, seq_len, tq):
    blk0 = pl.program_id(1) * (tq // WINDOW)
    nw = 3 * WINDOW
    lane = lax.broadcasted_iota(I32, (1, LANES), 1)
    even = lane < HEAD_DIM
    n_stack = 4 * WINDOW
    rows = lax.broadcasted_iota(I32, (n_stack, 1), 0)
    piece = rows // WINDOW
    qi = rows - piece * WINDOW
    kj = lax.broadcasted_iota(I32, (1, nw), 1)
    rel = kj - WINDOW - qi
    in_window = (rel <= WINDOW) & (rel >= -WINDOW)
    for kh in range(A_KV_HEADS):
        kl = slice(kh * LANES, (kh + 1) * LANES)
        vl = slice((A_KV_HEADS + kh) * LANES, (A_KV_HEADS + kh + 1) * LANES)
        kspan = jnp.concatenate([kvp_ref[:, kl], kvm_ref[:, kl], kvn_ref[:, kl]], axis=0)
        vspan = jnp.concatenate([kvp_ref[:, vl], kvm_ref[:, vl], kvn_ref[:, vl]], axis=0)
        kc = kvc_ref[:, kl]
        vc = kvc_ref[:, vl]
        sink_col = jnp.zeros((n_stack, 1), F32)
        for j in range(4):
            sink_col = jnp.where(piece == j, sink_ref[kh * 4 + j], sink_col)
        sink_col = sink_col * LOG2E
        for n in range(tq // WINDOW):
            qb = q_ref[n * WINDOW:(n + 1) * WINDOW, 2 * kh * LANES:2 * (kh + 1) * LANES]
            lq = jnp.concatenate([_pair_stack(qb[:, 0:LANES], even), _pair_stack(qb[:, LANES:2 * LANES], even)],
                                 axis=0)
            kpos = (blk0 + n - 1) * WINDOW + kj
            valid = in_window & (kpos >= 0) & (kpos < seq_len)
            s_ref[:, 0:nw] = jnp.where(valid, _dot_nt(lq, kspan[n * WINDOW:(n + 3) * WINDOW]), NEG_BIG)
            s_ref[:, nw:] = _dot_nt(lq, kc)
            r = _attend(s_ref, e_ref, jnp.concatenate([vspan[n * WINDOW:(n + 3) * WINDOW], vc], axis=0), sink_col)
            rs = slice(n * WINDOW, (n + 1) * WINDOW)
            o_ref[rs, 2 * kh * LANES:(2 * kh + 1) * LANES] = jnp.where(
                even, r[0:WINDOW], r[WINDOW:2 * WINDOW]).astype(BF16)
            o_ref[rs, (2 * kh + 1) * LANES:(2 * kh + 2) * LANES] = jnp.where(
                even, r[2 * WINDOW:3 * WINDOW], r[3 * WINDOW:4 * WINDOW]).astype(BF16)


def _win_attn(sink, qa, kva, kva_c, batch, seq_len, ctx_len):
    tq = min(WIN_Q_TILE, seq_len)
    nblk = seq_len // WINDOW
    per = tq // WINDOW
    nq = seq_len // tq
    kvw = kva.shape[1]
    n_keys = 3 * WINDOW + ctx_len
    kern = functools.partial(_win_attn_kernel, seq_len=seq_len, tq=tq)
    return pl.pallas_call(
        kern,
        out_shape=jax.ShapeDtypeStruct((batch * seq_len, BRANCH_W), BF16),
        grid_spec=pltpu.PrefetchScalarGridSpec(
            num_scalar_prefetch=1,
            grid=(batch, nq),
            in_specs=[
                pl.BlockSpec((tq, BRANCH_W), lambda b, i, s: (b * nq + i, 0)),
                pl.BlockSpec((WINDOW, kvw), lambda b, i, s: (b * nblk + jnp.maximum(i * per - 1, 0), 0)),
                pl.BlockSpec((tq, kvw), lambda b, i, s: (b * nq + i, 0)),
                pl.BlockSpec((WINDOW, kvw), lambda b, i, s: (b * nblk + jnp.minimum((i + 1) * per, nblk - 1), 0)),
                pl.BlockSpec((ctx_len, kvw), lambda b, i, s: (b, 0)),
            ],
            out_specs=pl.BlockSpec((tq, BRANCH_W), lambda b, i, s: (b * nq + i, 0)),
            scratch_shapes=[pltpu.VMEM((4 * WINDOW, n_keys), F32), pltpu.VMEM((4 * WINDOW, n_keys), BF16)],
        ),
        compiler_params=_cparams(2),
        name="win_attn",
    )(sink, qa, kva, kva, kva, kva_c)


def _ctx_attn_kernel(sink_ref, q_ref, k_ref, v_ref, o_ref, *, use_sink):
    p = pl.program_id(1)
    n = q_ref.shape[0]
    lane = lax.broadcasted_iota(I32, (1, LANES), 1)
    even = lane < HEAD_DIM
    lq = _pair_stack(q_ref[...], even)
    s = _dot_nt(lq, k_ref[...])
    sink_col = None
    if use_sink:
        rows = lax.broadcasted_iota(I32, (2 * n, 1), 0)
        sink_col = jnp.where(rows < n, sink_ref[2 * p], sink_ref[2 * p + 1]) * LOG2E
    r = _softmax_pv([s], [v_ref[...]], sink_col)
    o_ref[...] = jnp.where(even, r[0:n], r[n:2 * n]).astype(BF16)


def _ctx_attn(sink, q_arr, q_lane0, kv_arr, k_lane0, v_lane0, kv_shared, batch, ctx_len, use_sink):
    def kv_map(lane0):
        if kv_shared:
            return lambda b, p, s: (b, lane0 + p // 2)
        return lambda b, p, s: (b, lane0 + p)

    kern = functools.partial(_ctx_attn_kernel, use_sink=use_sink)
    return pl.pallas_call(
        kern,
        out_shape=jax.ShapeDtypeStruct((batch * ctx_len, BRANCH_W), BF16),
        grid_spec=pltpu.PrefetchScalarGridSpec(
            num_scalar_prefetch=1,
            grid=(batch, BRANCH_W // LANES),
            in_specs=[
                pl.BlockSpec((ctx_len, LANES), lambda b, p, s: (b, q_lane0 + p)),
                pl.BlockSpec((ctx_len, LANES), kv_map(k_lane0)),
                pl.BlockSpec((ctx_len, LANES), kv_map(v_lane0)),
            ],
            out_specs=pl.BlockSpec((ctx_len, LANES), lambda b, p, s: (b, p)),
        ),
        compiler_params=_cparams(2),
        name="ctx_attn_sink" if use_sink else "ctx_attn",
    )(sink, q_arr, kv_arr, kv_arr)


def _na_kernel(q_ref, kp_ref, km_ref, kn_ref, vp_ref, vm_ref, vn_ref, kc_ref, vc_ref, b0_ref, b1_ref, o_ref,
               s_ref, e_ref):
    gq = NA_ROW_GROUP * GRID_W
    nb = 3 * gq
    lane = lax.broadcasted_iota(I32, (1, LANES), 1)
    even = lane < HEAD_DIM
    for p in range(NA_HEADS // 2):
        pl_ = slice(p * LANES, (p + 1) * LANES)
        kc = kc_ref[:, pl_]
        vc = vc_ref[:, pl_]
        spans = (
            (jnp.concatenate([kp_ref[gq:2 * gq, pl_], km_ref[:, pl_]], axis=0),
             jnp.concatenate([vp_ref[gq:2 * gq, pl_], vm_ref[:, pl_]], axis=0)),
            (jnp.concatenate([km_ref[:, pl_], kn_ref[0:gq, pl_]], axis=0),
             jnp.concatenate([vm_ref[:, pl_], vn_ref[0:gq, pl_]], axis=0)),
        )
        for g, b_ref in enumerate((b0_ref, b1_ref)):
            kspan, vspan = spans[g]
            lq = _pair_stack(q_ref[g * gq:(g + 1) * gq, pl_], even)
            bias = jnp.concatenate([b_ref[0, 2 * p], b_ref[0, 2 * p + 1]], axis=0)
            s_ref[:, 0:nb] = _dot_nt(lq, kspan) + bias
            s_ref[:, nb:] = _dot_nt(lq, kc)
            r = _attend(s_ref, e_ref, jnp.concatenate([vspan, vc], axis=0), None)
            o_ref[g * gq:(g + 1) * gq, pl_] = jnp.where(even, r[0:gq], r[gq:2 * gq]).astype(BF16)


_N_DR = 2 * NA_ROWS_MAX - 1
_N_DC = 2 * NA_COLS - 1


def _na_bias_kernel(rpb_ref, o_ref):
    h = pl.program_id(0)
    qc = lax.broadcasted_iota(I32, (GRID_W, LANES), 0)
    lane = lax.broadcasted_iota(I32, (GRID_W, LANES), 1)
    kc = lane & (GRID_W - 1)
    dc = jnp.clip(kc - qc, -(NA_COLS - 1), NA_COLS - 1) + NA_COLS - 1
    col_start = jnp.clip(qc - NA_COLS // 2, 0, GRID_W - NA_COLS)
    col_ok = (kc >= col_start) & (kc < col_start + NA_COLS)
    neg = jnp.full((GRID_W, LANES), NEG_BIG, F32)
    col_bias = []
    for dr in range(_N_DR):
        c = neg
        for d in range(_N_DC):
            c = jnp.where(dc == d, rpb_ref[(h * _N_DR + dr) * _N_DC + d], c)
        col_bias.append(jnp.where(col_ok, c * LOG2E, neg))
    row_ok = (
        lambda i, j: 4 <= j < 4 + NA_ROWS_MAX,
        lambda i, j: i <= j < i + NA_ROWS_MAX,
        lambda i, j: 0 <= j < NA_ROWS_MAX,
    )
    for v in range(3):
        for i in range(NA_ROW_GROUP):
            for m in range(3 * NA_ROW_GROUP // 2):
                halves = [col_bias[j - i + 3] if row_ok[v](i, j) else neg for j in (2 * m, 2 * m + 1)]
                o_ref[v, 0, i * GRID_W:(i + 1) * GRID_W, m * LANES:(m + 1) * LANES] = jnp.where(
                    lane < GRID_W, halves[0], halves[1])


def _na_bias_tables(rpb):
    shape = (3, NA_HEADS, NA_ROW_GROUP * GRID_W, 3 * NA_ROW_GROUP * GRID_W)
    return pl.pallas_call(
        _na_bias_kernel,
        out_shape=jax.ShapeDtypeStruct(shape, F32),
        grid_spec=pltpu.PrefetchScalarGridSpec(
            num_scalar_prefetch=1,
            grid=(NA_HEADS,),
            in_specs=[],
            out_specs=pl.BlockSpec((3, 1) + shape[2:], lambda h, r: (0, h, 0, 0)),
        ),
        compiler_params=_cparams(1),
        name="na_bias",
    )(rpb.reshape(-1))


def _na_attn(na, na_c, bias_tab, batch, seq_len, ctx_len):
    rows = seq_len // GRID_W
    assert rows % (2 * NA_ROW_GROUP) == 0 and rows >= 4 * NA_ROW_GROUP
    tq = 2 * NA_ROW_GROUP * GRID_W
    nq = seq_len // tq
    gq = NA_ROW_GROUP * GRID_W
    n_keys = 3 * gq + ctx_len

    def tok_map(part, shift):
        return lambda b, i: (b * nq + jnp.clip(i + shift, 0, nq - 1), part)

    bias_block = (1, NA_HEADS, gq, 3 * gq)
    return pl.pallas_call(
        _na_kernel,
        out_shape=jax.ShapeDtypeStruct((batch * seq_len, BRANCH_W), BF16),
        grid=(batch, nq),
        in_specs=[
            pl.BlockSpec((tq, BRANCH_W), tok_map(0, 0)),
            pl.BlockSpec((tq, BRANCH_W), tok_map(1, -1)),
            pl.BlockSpec((tq, BRANCH_W), tok_map(1, 0)),
            pl.BlockSpec((tq, BRANCH_W), tok_map(1, 1)),
            pl.BlockSpec((tq, BRANCH_W), tok_map(2, -1)),
            pl.BlockSpec((tq, BRANCH_W), tok_map(2, 0)),
            pl.BlockSpec((tq, BRANCH_W), tok_map(2, 1)),
            pl.BlockSpec((ctx_len, BRANCH_W), lambda b, i: (b, 1)),
            pl.BlockSpec((ctx_len, BRANCH_W), lambda b, i: (b, 2)),
            pl.BlockSpec(bias_block, lambda b, i: (jnp.where(i == 0, 0, 1), 0, 0, 0)),
            pl.BlockSpec(bias_block, lambda b, i: (jnp.where(i == nq - 1, 2, 1), 0, 0, 0)),
        ],
        out_specs=pl.BlockSpec((tq, BRANCH_W), lambda b, i: (b * nq + i, 0)),
        scratch_shapes=[pltpu.VMEM((2 * gq, n_keys), F32), pltpu.VMEM((2 * gq, n_keys), BF16)],
        compiler_params=_cparams(2),
        name="na_attn",
    )(na, na, na, na, na, na, na, na_c, na_c, bias_tab, bias_tab)


def _merge_kernel(oa_ref, conv_ref, cprev_ref, cnext_ref, oc_ref, gate_ref, x_ref, mod_ref, wb_ref, bg_ref, cw_ref,
                  wo_ref, gpost_ref, out_ref, y_ref, *, tiles_per_seq):
    tm = x_ref.shape[0]
    ti = pl.program_id(0) % tiles_per_seq
    ux = conv_ref[:, 0:512].astype(F32)
    gb = conv_ref[:, 512:1024].astype(F32)
    gc = conv_ref[:, 1024:1536].astype(F32)
    u = gc * ux
    cprev = cprev_ref[...].astype(F32)
    cnext = cnext_ref[...].astype(F32)
    halo = cprev.shape[0]
    u_before = cprev[halo - 1:halo, 1024:1536] * cprev[halo - 1:halo, 0:512]
    u_after = cnext[0:1, 1024:1536] * cnext[0:1, 0:512]
    u_before = jnp.where(ti == 0, 0.0, u_before)
    u_after = jnp.where(ti == tiles_per_seq - 1, 0.0, u_after)
    row = lax.broadcasted_iota(I32, (tm, 1), 0)
    u_dn = jnp.where(row == 0, u_before, pltpu.roll(u, 1, 0))
    u_up = jnp.where(row == tm - 1, u_after, pltpu.roll(u, tm - 1, 0))
    cw = cw_ref[...]
    ob = (gb * (cw[0:1] * u_dn + cw[1:2] * u + cw[2:3] * u_up)).astype(BF16)
    oa = oa_ref[...]
    oc = oc_ref[...]
    for lo, hi in _col_chunks(D_MODEL):
        acc = None
        for r, o in enumerate((oa, ob, oc)):
            gate = jax.nn.sigmoid(gate_ref[:, r * D_MODEL + lo:r * D_MODEL + hi].astype(F32)
                                  + bg_ref[:, r * D_MODEL + lo:r * D_MODEL + hi])
            term = gate * _dot(o, wb_ref[r, :, lo:hi])
            acc = term if acc is None else acc + term
        y_ref[:, lo:hi] = acc.astype(BF16)
    y2 = _dot(y_ref[...], wo_ref[...])
    out_ref[...] = x_ref[...] + mod_ref[0, 2:3, :] * _rms(y2, gpost_ref[...])


def _merge(oa, conv, oc, gate, x2d, mods, wb, b_gate, conv_w, wo, g_post, seq_len, tm):
    n_tok = x2d.shape[0]
    tiles_per_seq = seq_len // tm
    halo = 16
    hb = tm // halo
    n_halo = n_tok // halo
    if mods.shape[0] == 1:
        mod_map = lambda i: (0, 0, 0)
    else:
        mod_map = lambda i: (i // tiles_per_seq, 0, 0)
    kern = functools.partial(_merge_kernel, tiles_per_seq=tiles_per_seq)
    const2 = lambda i: (0, 0)
    return pl.pallas_call(
        kern,
        out_shape=jax.ShapeDtypeStruct((n_tok, D_MODEL), F32),
        grid=(n_tok // tm,),
        in_specs=[
            pl.BlockSpec((tm, BRANCH_W), lambda i: (i, 0)),
            pl.BlockSpec((tm, 3 * BRANCH_W), lambda i: (i, 0)),
            pl.BlockSpec((halo, 3 * BRANCH_W), lambda i: (jnp.maximum(i * hb - 1, 0), 0)),
            pl.BlockSpec((halo, 3 * BRANCH_W), lambda i: (jnp.minimum((i + 1) * hb, n_halo - 1), 0)),
            pl.BlockSpec((tm, BRANCH_W), lambda i: (i, 0)),
            pl.BlockSpec((tm, N_BRANCH * D_MODEL), lambda i: (i, 0)),
            pl.BlockSpec((tm, D_MODEL), lambda i: (i, 0)),
            pl.BlockSpec((1, 6, D_MODEL), mod_map),
            pl.BlockSpec((N_BRANCH, BRANCH_W, D_MODEL), lambda i: (0, 0, 0)),
            pl.BlockSpec((1, N_BRANCH * D_MODEL), const2),
            pl.BlockSpec((CONV_K, BRANCH_W), const2),
            pl.BlockSpec((D_MODEL, D_MODEL), const2),
            pl.BlockSpec((1, D_MODEL), const2),
        ],
        out_specs=pl.BlockSpec((tm, D_MODEL), lambda i: (i, 0)),
        scratch_shapes=[pltpu.VMEM((tm, D_MODEL), BF16)],
        compiler_params=_cparams(1),
        name="branch_merge",
    )(oa, conv, conv, conv, oc, gate, x2d, mods, wb, b_gate, conv_w, wo, g_post)


def _ffn_kernel(x_ref, mod_ref, gpre_ref, wgu_ref, wdn_ref, gpost_ref, out_ref, act_ref):
    x = x_ref[...]
    h = (_rms(x, gpre_ref[...]) * (1.0 + mod_ref[0, 4:5, :]) + mod_ref[0, 3:4, :]).astype(BF16)
    for lo, hi in _col_chunks(FFN_DENSE):
        g = _dot(h, wgu_ref[:, lo:hi])
        u = _dot(h, wgu_ref[:, FFN_DENSE + lo:FFN_DENSE + hi])
        act_ref[:, lo:hi] = (g * jax.nn.sigmoid(g) * u).astype(BF16)
    y = _dot(act_ref[...], wdn_ref[...])
    out_ref[...] = x + mod_ref[0, 5:6, :] * _rms(y, gpost_ref[...])


def _ffn_dense(x2d, mods, g_pre, wgu, wdn, g_post, seq_len, tm):
    n_tok = x2d.shape[0]
    tiles_per_seq = seq_len // tm
    if mods.shape[0] == 1:
        mod_map = lambda i: (0, 0, 0)
    else:
        mod_map = lambda i: (i // tiles_per_seq, 0, 0)
    const2 = lambda i: (0, 0)
    return pl.pallas_call(
        _ffn_kernel,
        out_shape=jax.ShapeDtypeStruct((n_tok, D_MODEL), F32),
        grid=(n_tok // tm,),
        in_specs=[
            pl.BlockSpec((tm, D_MODEL), lambda i: (i, 0)),
            pl.BlockSpec((1, 6, D_MODEL), mod_map),
            pl.BlockSpec((1, D_MODEL), const2),
            pl.BlockSpec((D_MODEL, 2 * FFN_DENSE), const2),
            pl.BlockSpec((FFN_DENSE, D_MODEL), const2),
            pl.BlockSpec((1, D_MODEL), const2),
        ],
        out_specs=pl.BlockSpec((tm, D_MODEL), lambda i: (i, 0)),
        scratch_shapes=[pltpu.VMEM((tm, FFN_DENSE), BF16)],
        compiler_params=_cparams(1),
        name="ffn_dense",
    )(x2d, mods, g_pre, wgu, wdn, g_post)


def _to_token_tiles(a):
    slabs = jnp.stack([a[:, s * LANES:(s + 1) * LANES] for s in range(D_MODEL // LANES)], axis=0)
    return pltpu.einshape("stl->tsl", slabs)


def _from_token_tiles(t):
    slabs = pltpu.einshape("tsl->stl", t)
    return jnp.concatenate([slabs[s] for s in range(D_MODEL // LANES)], axis=1)


TOKEN_TILE_SHAPE = (D_MODEL // LANES, LANES)


def _router_kernel(x_ref, mod_ref, gpre_ref, wr_ref, br_ref, h_ref, e_ref, gt_ref):
    x = x_ref[...]
    h = _rms(x, gpre_ref[...]) * (1.0 + mod_ref[0, 4:5, :]) + mod_ref[0, 3:4, :]
    h_ref[...] = _to_token_tiles(h)
    logits = lax.dot_general(wr_ref[...], h, (((1,), (1,)), ((), ())), preferred_element_type=F32,
                             precision=lax.Precision.HIGHEST) + br_ref[...]
    eid = lax.broadcasted_iota(I32, logits.shape, 0)
    m1 = logits.max(axis=0, keepdims=True)
    i1 = jnp.min(jnp.where(logits == m1, eid, N_EXPERTS), axis=0, keepdims=True)
    rest = jnp.where(eid == i1, -jnp.inf, logits)
    m2 = rest.max(axis=0, keepdims=True)
    i2 = jnp.min(jnp.where(rest == m2, eid, N_EXPERTS), axis=0, keepdims=True)
    e2 = jnp.exp(m2 - m1)
    den = 1.0 + e2
    e_ref[...] = jnp.concatenate([i1, i2], axis=0)
    gt_ref[...] = jnp.concatenate([1.0 / den, e2 / den], axis=0)


def _router(x2d, mods, g_pre, w_router_t, b_router, seq_len, tm):
    n_tok = x2d.shape[0]
    tiles_per_seq = seq_len // tm
    const2 = lambda i: (0, 0)
    return pl.pallas_call(
        _router_kernel,
        out_shape=[
            jax.ShapeDtypeStruct((n_tok,) + TOKEN_TILE_SHAPE, F32),
            jax.ShapeDtypeStruct((TOP_K, n_tok), I32),
            jax.ShapeDtypeStruct((TOP_K, n_tok), F32),
        ],
        grid=(n_tok // tm,),
        in_specs=[
            pl.BlockSpec((tm, D_MODEL), lambda i: (i, 0)),
            pl.BlockSpec((1, 6, D_MODEL), lambda i: (i // tiles_per_seq, 0, 0)),
            pl.BlockSpec((1, D_MODEL), const2),
            pl.BlockSpec((N_EXPERTS, D_MODEL), const2),
            pl.BlockSpec((N_EXPERTS, 1), const2),
        ],
        out_specs=[
            pl.BlockSpec((tm,) + TOKEN_TILE_SHAPE, lambda i: (i, 0, 0)),
            pl.BlockSpec((TOP_K, tm), lambda i: (0, i)),
            pl.BlockSpec((TOP_K, tm), lambda i: (0, i)),
        ],
        compiler_params=_cparams(1),
        name="moe_router",
    )(x2d, mods, g_pre, w_router_t, b_router)


def _rank_kernel(e_ref, rank_ref, cnt_ref, carry_ref):
    tm = e_ref.shape[1]

    @pl.when(pl.program_id(0) == 0)
    def _():
        carry_ref[...] = jnp.zeros_like(carry_ref)

    e = e_ref[...]
    eid = lax.broadcasted_iota(I32, (N_EXPERTS, tm), 0)
    oh0 = eid == e[0:1, :]
    oh1 = eid == e[1:2, :]
    oh = jnp.concatenate([oh0, oh1], axis=0).astype(F32)
    tri = (lax.broadcasted_iota(I32, (tm, tm), 0) <= lax.broadcasted_iota(I32, (tm, tm), 1)).astype(BF16)
    incl = _dot(oh.astype(BF16), tri)
    excl = incl - oh
    tot = incl[:, tm - 1:tm]
    carry = carry_ref[:, 0:1]
    rank0 = carry + excl[0:N_EXPERTS]
    rank1 = carry + tot[0:N_EXPERTS] + excl[N_EXPERTS:2 * N_EXPERTS]
    r0 = jnp.sum(jnp.where(oh0, rank0, 0.0), axis=0, keepdims=True)
    r1 = jnp.sum(jnp.where(oh1, rank1, 0.0), axis=0, keepdims=True)
    rank_ref[...] = jnp.concatenate([r0, r1], axis=0).astype(I32)
    new_carry = carry + tot[0:N_EXPERTS] + tot[N_EXPERTS:2 * N_EXPERTS]
    carry_ref[...] = jnp.broadcast_to(new_carry, carry_ref.shape)
    cnt_ref[...] = jnp.broadcast_to(new_carry, cnt_ref.shape).astype(I32)


def _ranks(e_idx, tm):
    n_tok = e_idx.shape[1]
    return pl.pallas_call(
        _rank_kernel,
        out_shape=[
            jax.ShapeDtypeStruct((TOP_K, n_tok), I32),
            jax.ShapeDtypeStruct((N_EXPERTS, LANES), I32),
        ],
        grid=(n_tok // tm,),
        in_specs=[pl.BlockSpec((TOP_K, tm), lambda i: (0, i))],
        out_specs=[
            pl.BlockSpec((TOP_K, tm), lambda i: (0, i)),
            pl.BlockSpec((N_EXPERTS, LANES), lambda i: (0, 0)),
        ],
        scratch_shapes=[pltpu.VMEM((N_EXPERTS, LANES), F32)],
        compiler_params=_cparams(1),
        name="moe_rank",
    )(e_idx)


def _dispatch_kernel(dest_ref, pad_ref, h_ref, xs_ref, zero_ref, sem, zsem, *, tm):
    n_tok = dest_ref.shape[0] // TOP_K
    base = pl.program_id(0) * tm

    @pl.when(pl.program_id(0) == 0)
    def _():
        zero_ref[...] = jnp.zeros_like(zero_ref)

        zrows = zero_ref.shape[0]

        def zero_copy(r):
            return pltpu.make_async_copy(zero_ref.at[0], xs_ref.at[r], zsem)

        def zero_chunk(j):
            return pltpu.make_async_copy(zero_ref, xs_ref.at[pl.ds(pl.multiple_of(j * zrows, zrows), zrows)], zsem)

        tail = (pad_ref[2 * N_EXPERTS] // zrows, xs_ref.shape[0] // zrows)
        for e in range(N_EXPERTS):
            lo, hi = pad_ref[e], pad_ref[N_EXPERTS + e]
            lax.fori_loop(lo, hi, lambda r, c: (zero_copy(r).start(), c)[1], 0)
        lax.fori_loop(tail[0], tail[1], lambda j, c: (zero_chunk(j).start(), c)[1], 0)
        for e in range(N_EXPERTS):
            lo, hi = pad_ref[e], pad_ref[N_EXPERTS + e]
            lax.fori_loop(lo, hi, lambda r, c: (zero_copy(r).wait(), c)[1], 0)
        lax.fori_loop(tail[0], tail[1], lambda j, c: (zero_chunk(j).wait(), c)[1], 0)

    def start(t, carry):
        for k in range(TOP_K):
            d = dest_ref[k * n_tok + base + t]
            pltpu.make_async_copy(h_ref.at[base + t], xs_ref.at[d], sem).start(priority=k)
        return carry

    lax.fori_loop(0, tm, start, 0, unroll=ISSUE_UNROLL)

    def wait_step():
        for k in range(TOP_K):
            pltpu.make_async_copy(h_ref.at[pl.ds(0, tm)], xs_ref.at[pl.ds(0, tm)], sem).wait()

    @pl.when(pl.program_id(0) > 0)
    def _():
        wait_step()

    @pl.when(pl.program_id(0) == pl.num_programs(0) - 1)
    def _():
        wait_step()


def _dispatch(dest_flat, pad_rows, h, n_rows, tm):
    n_tok = h.shape[0]
    return pl.pallas_call(
        functools.partial(_dispatch_kernel, tm=tm),
        out_shape=jax.ShapeDtypeStruct((n_rows,) + TOKEN_TILE_SHAPE, F32),
        grid_spec=pltpu.PrefetchScalarGridSpec(
            num_scalar_prefetch=2,
            grid=(n_tok // tm,),
            in_specs=[pl.BlockSpec(memory_space=pl.ANY)],
            out_specs=pl.BlockSpec(memory_space=pl.ANY),
            scratch_shapes=[
                pltpu.VMEM((64,) + TOKEN_TILE_SHAPE, F32),
                pltpu.SemaphoreType.DMA,
                pltpu.SemaphoreType.DMA,
            ],
        ),
        compiler_params=_cparams(1),
        name="moe_dispatch",
    )(dest_flat, pad_rows, h)


def _expert_kernel(be_ref, nb_ref, xs_ref, wgu_ref, wdn_ref, y_ref, act_ref):
    @pl.when(pl.program_id(0) < nb_ref[0])
    def _():
        xb = _from_token_tiles(xs_ref[...]).astype(BF16)
        for lo, hi in _col_chunks(FFN_EXPERT):
            g = _dot(xb, wgu_ref[0, :, lo:hi])
            u = _dot(xb, wgu_ref[0, :, FFN_EXPERT + lo:FFN_EXPERT + hi])
            act_ref[:, lo:hi] = (g * jax.nn.sigmoid(g) * u).astype(BF16)
        y_ref[...] = _to_token_tiles(_dot(act_ref[...], wdn_ref[0]))

    @pl.when(pl.program_id(0) >= nb_ref[0])
    def _():
        y_ref[...] = jnp.zeros_like(y_ref)


def _expert_ffn(block_e, n_used, xs, wgu, wdn, bm):
    n_rows = xs.shape[0]
    n_blocks = n_rows // bm

    def blk(j, be, nb):
        return jnp.minimum(j, nb[0] - 1)

    return pl.pallas_call(
        _expert_kernel,
        out_shape=jax.ShapeDtypeStruct((n_rows,) + TOKEN_TILE_SHAPE, F32),
        grid_spec=pltpu.PrefetchScalarGridSpec(
            num_scalar_prefetch=2,
            grid=(n_blocks,),
            in_specs=[
                pl.BlockSpec((bm,) + TOKEN_TILE_SHAPE, lambda j, be, nb: (blk(j, be, nb), 0, 0)),
                pl.BlockSpec((1, D_MODEL, 2 * FFN_EXPERT), lambda j, be, nb: (be[blk(j, be, nb)], 0, 0)),
                pl.BlockSpec((1, FFN_EXPERT, D_MODEL), lambda j, be, nb: (be[blk(j, be, nb)], 0, 0),
                             pipeline_mode=pl.Buffered(1)),
            ],
            out_specs=pl.BlockSpec((bm,) + TOKEN_TILE_SHAPE, lambda j, be, nb: (j, 0, 0)),
            scratch_shapes=[pltpu.VMEM((bm, FFN_EXPERT), BF16)],
        ),
        compiler_params=_cparams(1),
        name="moe_experts",
    )(block_e, n_used, xs, wgu, wdn)


def _combine_kernel(dest_ref, ys_ref, gt_ref, x_ref, mod_ref, gpost_ref, out_ref, buf_ref, sem):
    tm = x_ref.shape[0]
    n_tok = dest_ref.shape[0] // TOP_K
    i = pl.program_id(0)
    slot = lax.rem(i, 2)

    def issue(step, into):
        base = step * tm

        def start(t, carry):
            for k in range(TOP_K):
                d = dest_ref[k * n_tok + base + t]
                pltpu.make_async_copy(ys_ref.at[d], buf_ref.at[into, k, t], sem.at[into]).start(priority=k)
            return carry

        lax.fori_loop(0, tm, start, 0, unroll=ISSUE_UNROLL)

    @pl.when(i == 0)
    def _():
        issue(0, 0)

    @pl.when(i + 1 < pl.num_programs(0))
    def _():
        issue(i + 1, 1 - slot)

    for k in range(TOP_K):
        pltpu.make_async_copy(ys_ref.at[pl.ds(0, tm)], buf_ref.at[slot, k], sem.at[slot]).wait()
    gt = gt_ref[...]
    y = gt[:, 0:1] * _from_token_tiles(buf_ref[slot, 0]) + gt[:, 1:2] * _from_token_tiles(buf_ref[slot, 1])
    out_ref[...] = x_ref[...] + mod_ref[0, 5:6, :] * _rms(y, gpost_ref[...])


def _combine(dest_flat, ys, gates_t, x2d, mods, g_post, seq_len, tm):
    n_tok = x2d.shape[0]
    tiles_per_seq = seq_len // tm
    return pl.pallas_call(
        _combine_kernel,
        out_shape=jax.ShapeDtypeStruct((n_tok, D_MODEL), F32),
        grid_spec=pltpu.PrefetchScalarGridSpec(
            num_scalar_prefetch=1,
            grid=(n_tok // tm,),
            in_specs=[
                pl.BlockSpec(memory_space=pl.ANY),
                pl.BlockSpec((tm, TOP_K), lambda i, d: (i, 0)),
                pl.BlockSpec((tm, D_MODEL), lambda i, d: (i, 0)),
                pl.BlockSpec((1, 6, D_MODEL), lambda i, d: (i // tiles_per_seq, 0, 0)),
                pl.BlockSpec((1, D_MODEL), lambda i, d: (0, 0)),
            ],
            out_specs=pl.BlockSpec((tm, D_MODEL), lambda i, d: (i, 0)),
            scratch_shapes=[
                pltpu.VMEM((2, TOP_K, tm) + TOKEN_TILE_SHAPE, F32),
                pltpu.SemaphoreType.DMA((2,)),
            ],
        ),
        compiler_params=_cparams(1),
        name="moe_combine",
    )(dest_flat, ys, gates_t, x2d, mods, g_post)


def _moe(x2d, mods, g_pre, g_post, w_router, b_router, wgu, wdn, seq_len, tm):
    n_tok = x2d.shape[0]
    bm = MOE_BLOCK_ROWS
    h, e_idx, gates = _router(x2d, mods, g_pre, w_router.T, b_router.reshape(N_EXPERTS, 1), seq_len, tm)
    rank, cnt = _ranks(e_idx, tm)
    counts = cnt[:, 0]
    padded = (counts + bm - 1) // bm * bm
    ends = jnp.cumsum(padded)
    pstart = ends - padded
    dest = rank
    for e in range(N_EXPERTS):
        dest = dest + jnp.where(e_idx == e, pstart[e], 0)
    dest_flat = dest.reshape(-1).astype(I32)
    n_blocks = n_tok * TOP_K // bm + N_EXPERTS
    blk_start = jnp.arange(n_blocks, dtype=I32) * bm
    block_e = jnp.minimum(jnp.sum(blk_start[:, None] >= ends[None, :], axis=1), N_EXPERTS - 1).astype(I32)
    n_used = (ends[-1:] // bm).astype(I32)
    pad_rows = jnp.concatenate([pstart + counts, ends, ends[-1:]]).astype(I32)
    xs = _dispatch(dest_flat, pad_rows, h, n_blocks * bm, min(DISPATCH_TILE, n_tok))
    ys = _expert_ffn(block_e, n_used, xs, wgu, wdn, bm)
    return _combine(dest_flat, ys, gates.T, x2d, mods, g_post, seq_len, tm)


def _rope_tables(n_tokens):
    pos = jnp.arange(n_tokens)
    row = (pos // GRID_W).astype(F32)
    col = (pos % GRID_W).astype(F32)
    half = HEAD_DIM // 2
    inv = 1.0 / (ROPE_BASE ** (jnp.arange(0, half, 2, dtype=F32) / half))
    ar = row[:, None] * inv[None]
    ac = col[:, None] * inv[None]
    ang = jnp.concatenate([ar, ar, ac, ac], axis=-1)
    ang = jnp.concatenate([ang, ang], axis=-1)
    sign = jnp.where((jnp.arange(LANES) & 16) == 0, -1.0, 1.0).astype(F32)
    return jnp.cos(ang), jnp.sin(ang) * sign[None]


def _arrange_w_in(w):
    scale = HEAD_DIM ** -0.5 * LOG2E
    offs = np.cumsum([0, BRANCH_W, 128, 128, BRANCH_W, BRANCH_W, BRANCH_W, BRANCH_W, BRANCH_W, BRANCH_W, N_BRANCH * D_MODEL])
    parts = [w[:, offs[i]:offs[i + 1]] for i in range(10)]
    qa, ka, va, ux, gb, gc, qn, kn, vn, gx = parts

    def dup(t):
        return jnp.concatenate([t[:, 0:64], t[:, 0:64], t[:, 64:128], t[:, 64:128]], axis=1)

    return jnp.concatenate([qa * scale, dup(ka), dup(va), ux, gb, gc, qn * scale, kn, vn, gx], axis=1).astype(BF16)


def kernel(x, c, ctx, c_ctx, w_ada, b_ada, g_pre_mix, g_post_mix, g_pre_ffn, g_post_ffn, w_in, b_gate, sink, conv_w,
           rpb, w_branch, w_out, w_gu_dense, w_dn_dense, w_router, b_router, w_gu_moe, w_dn_moe):
    batch, seq_len, d = x.shape
    ctx_len = ctx.shape[1]
    tm = min(TOKEN_TILE, seq_len)
    tmc = min(TOKEN_TILE, ctx_len)

    n_vec = batch + 1
    n_vec_pad = -(-n_vec // 8) * 8
    cvecs = jnp.concatenate([c, c_ctx[None], jnp.zeros((n_vec_pad - n_vec, d), F32)], axis=0)
    mods = _ada_mods(cvecs, w_ada, b_ada)

    cos_t, sin_t = _rope_tables(seq_len)
    cos_c = jnp.ones((ctx_len, LANES), F32)
    sin_c = jnp.zeros((ctx_len, LANES), F32)

    x2 = x.reshape(batch * seq_len, d)
    hc2 = ctx.reshape(batch * ctx_len, d)
    for l in range(DEPTH):
        update_ctx = l < DEPTH - 1
        mods_x = mods[l, :batch]
        mods_c = mods[l, batch:batch + 1]
        w_arr = _arrange_w_in(w_in[l])
        wb = w_branch[l].astype(BF16)
        wo = w_out[l].astype(BF16)
        bg = b_gate[l].reshape(1, N_BRANCH * D_MODEL)
        g1 = g_pre_mix[l].reshape(1, d)
        g2 = g_post_mix[l].reshape(1, d)
        g3 = g_pre_ffn[l].reshape(1, d)
        g4 = g_post_ffn[l].reshape(1, d)

        qa, kva, conv, na, gate = _inproj(x2, mods_x, g1, cos_t, sin_t, w_arr, seq_len, tm)
        qa_c, kva_c, conv_c, na_c, gate_c = _inproj(hc2, mods_c, g1, cos_c, sin_c, w_arr, ctx_len, tmc)

        o_a = _win_attn(sink[l], qa, kva, kva_c, batch, seq_len, ctx_len)
        o_c = _na_attn(na, na_c, _na_bias_tables(rpb[l]), batch, seq_len, ctx_len)
        x2 = _merge(o_a, conv, o_c, gate, x2, mods_x, wb, bg, conv_w[l], wo, g2, seq_len, tm)

        if l % 2 == 0:
            wgu = w_gu_dense[l // 2].astype(BF16)
            wdn = w_dn_dense[l // 2].astype(BF16)
            x2 = _ffn_dense(x2, mods_x, g3, wgu, wdn, g4, seq_len, tm)
        else:
            x2 = _moe(x2, mods_x, g3, g4, w_router[l // 2], b_router[l // 2], w_gu_moe[l // 2].astype(BF16),
                      w_dn_moe[l // 2].astype(BF16), seq_len, tm)

        if update_ctx:
            o_a_c = _ctx_attn(sink[l], qa_c, 0, kva_c, 0, 2, True, batch, ctx_len, True)
            o_c_c = _ctx_attn(sink[l], na_c, 0, na_c, 4, 8, False, batch, ctx_len, False)
            hc2 = _merge(o_a_c, conv_c, o_c_c, gate_c, hc2, mods_c, wb, bg, conv_w[l], wo, g2, ctx_len, tmc)
            if l % 2 == 0:
                hc2 = _ffn_dense(hc2, mods_c, g3, wgu, wdn, g4, ctx_len, tmc)
            else:
                hc2 = _moe(hc2, jnp.broadcast_to(mods_c, (batch,) + mods_c.shape[1:]), g3, g4, w_router[l // 2],
                           b_router[l // 2], w_gu_moe[l // 2].astype(BF16), w_dn_moe[l // 2].astype(BF16),
                           ctx_len, tmc)
    return x2.reshape(batch, seq_len, d)
```
